```python
import math
import jax, jax.numpy as jnp
from jax import lax
import numpy as np

D_MODEL = 2048
BATCH = 2
SEQ = 8192
DEPTH = 1

N_HGRN_HEADS = 8
HGRN_KEY_DIM = 128
HGRN_VAL_DIM = 128
HGRN_FORGET_WIDTH = N_HGRN_HEADS * HGRN_KEY_DIM
HGRN_WIDTH = N_HGRN_HEADS * HGRN_VAL_DIM
CHUNK = 64
N_DIFF_HEADS = 8
DIFF_HEAD_DIM = 64
DIFF_QK_WIDTH = N_DIFF_HEADS * 2 * DIFF_HEAD_DIM
DIFF_V_DIM = 2 * DIFF_HEAD_DIM
DIFF_V_WIDTH = N_DIFF_HEADS * DIFF_V_DIM
ROPE_DIM = DIFF_HEAD_DIM // 4
ROPE_THETA = 500000.0
Q_BLOCK = 128
D_FF = 5632
N_MOD = 9
EPS = 1e-6
MIX_SIZES = (HGRN_FORGET_WIDTH, HGRN_FORGET_WIDTH, HGRN_WIDTH, HGRN_WIDTH,
             DIFF_QK_WIDTH, DIFF_QK_WIDTH, DIFF_V_WIDTH, D_MODEL, D_MODEL)
MIX_WIDTH = 4 * 1024 + 3 * 1024 + 2 * D_MODEL

kernel_name = "hybrid_hgrn2_diffattn_macaron_adaln"


def _split(t, sizes):
    idx = []
    acc = 0
    for s in sizes[:-1]:
        acc += s
        idx.append(acc)
    return jnp.split(t, idx, axis=-1)


def rmsnorm(x, w):
    xf = x.astype(jnp.float32)
    y = xf * lax.rsqrt(jnp.mean(xf * xf, axis=-1, keepdims=True) + EPS)
    return (y * w.astype(jnp.float32)).astype(x.dtype)


def modulate(n, shift, scale):
    return n * (1 + scale[:, None, :]) + shift[:, None, :]


def swiglu(x, w_in, w_out):
    g, u = jnp.split(x @ w_in, 2, axis=-1)
    return (jax.nn.silu(g) * u) @ w_out


def partial_rope(x, positions):
    half = ROPE_DIM // 2
    inv_freq = ROPE_THETA ** (-jnp.arange(half, dtype=jnp.float32) / half)
    ang = positions.astype(jnp.float32)[..., None] * inv_freq
    cos = jnp.cos(ang)[:, :, None, None, :]
    sin = jnp.sin(ang)[:, :, None, None, :]
    xr = x[..., :ROPE_DIM].astype(jnp.float32)
    x1, x2 = xr[..., :half], xr[..., half:]
    rot = jnp.concatenate([x1 * cos - x2 * sin, x2 * cos + x1 * sin], axis=-1)
    return jnp.concatenate([rot.astype(x.dtype), x[..., ROPE_DIM:]], axis=-1)


def hgrn2(q, f_pre, inp, lb):
    B, S, H, K = q.shape
    V = inp.shape[-1]
    N = S // CHUNK
    f = lb + (1 - lb) * jax.nn.sigmoid(f_pre.astype(jnp.float32))
    logf = jnp.log(f)
    k = 1 - f

    def to_chunks(t):
        return t.reshape(B, N, CHUNK, H, t.shape[-1]).transpose(1, 0, 3, 2, 4)

    qc = to_chunks(q.astype(jnp.float32))
    kc = to_chunks(k)
    vc = to_chunks(inp.astype(jnp.float32))
    bc = jnp.cumsum(to_chunks(logf), axis=3)
    causal = jnp.tril(jnp.ones((CHUNK, CHUNK), dtype=bool))

    def step(state, xs):
        qh, kh, vh, bh = xs
        inter = jnp.einsum('bhck,bhkv->bhcv', qh * jnp.exp(bh), state)
        diff = bh[:, :, :, None, :] - bh[:, :, None, :, :]
        decay = jnp.where(causal[:, :, None], jnp.exp(jnp.minimum(diff, 0.0)), 0.0)
        scores = jnp.einsum('bhtk,bhtsk,bhsk->bhts', qh, decay, kh)
        o = inter + jnp.einsum('bhts,bhsv->bhtv', scores, vh)
        b_last = bh[:, :, -1, :]
        new_state = jnp.exp(b_last)[..., None] * state + jnp.einsum(
            'bhsk,bhsv->bhkv', kh * jnp.exp(b_last[:, :, None, :] - bh), vh)
        return new_state, o

    s0 = jnp.zeros((B, H, K, V), jnp.float32)
    _, o = lax.scan(step, s0, (qc, kc, vc, bc))
    return o.transpose(1, 0, 3, 2, 4).reshape(B, S, H, V)


def diff_attention(q, k, v, lam):
    B, S, H, _, dh = q.shape
    scale = 1.0 / math.sqrt(dh)
    qt = q.transpose(0, 2, 3, 1, 4)
    kt = k.transpose(0, 2, 3, 1, 4)
    vt = v.transpose(0, 2, 1, 3)
    kpos = jnp.arange(S)

    def block(i):
        start = i * Q_BLOCK
        qb = lax.dynamic_slice_in_dim(qt, start, Q_BLOCK, axis=3)
        s = jnp.einsum('bhcqd,bhckd->bhcqk', qb, kt).astype(jnp.float32) * scale
        qpos = start + jnp.arange(Q_BLOCK)
        mask = kpos[None, :] <= qpos[:, None]
        p = jax.nn.softmax(jnp.where(mask, s, -jnp.inf), axis=-1)
        w = p[:, :, 0] - lam * p[:, :, 1]
        return jnp.einsum('bhqk,bhkv->bhqv', w.astype(vt.dtype), vt)

    out = lax.map(block, jnp.arange(S // Q_BLOCK))
    return out.transpose(1, 0, 3, 2, 4).reshape(B, S, H, v.shape[-1])


def setup_inputs(seed: int = 0) -> dict:
    key = jax.random.key(seed)
    ks = jax.random.split(key, 32)
    L = DEPTH
    f32 = jnp.float32

    def w(k, shape, fan_in, mult=1.0):
        return jax.random.normal(k, shape, f32) * (mult * fan_in ** -0.5)

    def gain(k, shape):
        return 1.0 + 0.02 * jax.random.normal(k, shape, f32)

    return {
        "x": jax.random.normal(ks[0], (BATCH, SEQ, D_MODEL), f32),
        "c": jax.random.normal(ks[1], (BATCH, D_MODEL), f32),
        "positions": jnp.broadcast_to(jnp.arange(SEQ, dtype=jnp.int32), (BATCH, SEQ)),
        "ada_w": w(ks[2], (L, D_MODEL, N_MOD * D_MODEL), D_MODEL, 0.5),
        "ada_b": 0.02 * jax.random.normal(ks[3], (L, N_MOD * D_MODEL), f32),
        "norm1_w": gain(ks[4], (L, D_MODEL)),
        "ffn1_w_in": w(ks[5], (L, D_MODEL, 2 * D_FF), D_MODEL),
        "ffn1_w_out": w(ks[6], (L, D_FF, D_MODEL), D_FF),
        "norm2_w": gain(ks[7], (L, D_MODEL)),
        "mix_w_in": w(ks[8], (L, D_MODEL, MIX_WIDTH), D_MODEL),
        "hgrn_lb": 0.5 * jax.random.normal(ks[9], (L + 1, HGRN_FORGET_WIDTH), f32),
        "hgrn_norm_w": gain(ks[10], (L, HGRN_VAL_DIM)),
        "hgrn_w_out": w(ks[11], (L, HGRN_WIDTH, D_MODEL), HGRN_WIDTH),
        "diff_q_norm_w": gain(ks[12], (L, DIFF_HEAD_DIM)),
        "diff_k_norm_w": gain(ks[13], (L, DIFF_HEAD_DIM)),
        "diff_lq1": 0.1 * jax.random.normal(ks[14], (L, DIFF_HEAD_DIM), f32),
        "diff_lk1": 0.1 * jax.random.normal(ks[15], (L, DIFF_HEAD_DIM), f32),
        "diff_lq2": 0.1 * jax.random.normal(ks[16], (L, DIFF_HEAD_DIM), f32),
        "diff_lk2": 0.1 * jax.random.normal(ks[17], (L, DIFF_HEAD_DIM), f32),
        "diff_subln_w": gain(ks[18], (L, DIFF_V_DIM)),
        "diff_w_out": w(ks[19], (L, DIFF_V_WIDTH, D_MODEL), DIFF_V_WIDTH),
        "mix_w_o": w(ks[20], (L, D_MODEL, D_MODEL), D_MODEL),
        "norm3_w": gain(ks[21], (L, D_MODEL)),
        "ffn2_w_in": w(ks[22], (L, D_MODEL, 2 * D_FF), D_MODEL),
        "ffn2_w_out": w(ks[23], (L, D_FF, D_MODEL), D_FF),
    }


def reference(x, c, positions, ada_w, ada_b, norm1_w, ffn1_w_in, ffn1_w_out, norm2_w,
              mix_w_in, hgrn_lb, hgrn_norm_w, hgrn_w_out, diff_q_norm_w, diff_k_norm_w,
              diff_lq1, diff_lk1, diff_lq2, diff_lk2, diff_subln_w, diff_w_out, mix_w_o,
              norm3_w, ffn2_w_in, ffn2_w_out):
    B, S, D = x.shape
    lb_all = jnp.cumsum(jax.nn.softmax(hgrn_lb.astype(jnp.float32), axis=0), axis=0)
    h = x
    for l in range(DEPTH):
        mods = jnp.split(jax.nn.silu(c) @ ada_w[l] + ada_b[l], N_MOD, axis=-1)
        sh1, sc1, g1, sh2, sc2, g2, sh3, sc3, g3 = mods

        n1 = modulate(rmsnorm(h, norm1_w[l]), sh1, sc1)
        h = h + 0.5 * g1[:, None, :] * swiglu(n1, ffn1_w_in[l], ffn1_w_out[l])

        u = modulate(rmsnorm(h, norm2_w[l]), sh2, sc2)
        hq, hf, hi, hg, dq, dk, dv, ga, gb = _split(u @ mix_w_in[l], MIX_SIZES)

        lb = lb_all[l].reshape(N_HGRN_HEADS, HGRN_KEY_DIM)
        o_a = hgrn2(hq.reshape(B, S, N_HGRN_HEADS, HGRN_KEY_DIM),
                    hf.reshape(B, S, N_HGRN_HEADS, HGRN_KEY_DIM),
                    hi.reshape(B, S, N_HGRN_HEADS, HGRN_VAL_DIM), lb).astype(u.dtype)
        o_a = rmsnorm(o_a, hgrn_norm_w[l]) * jax.nn.silu(hg.reshape(B, S, N_HGRN_HEADS, HGRN_VAL_DIM))
        y_a = o_a.reshape(B, S, HGRN_WIDTH) @ hgrn_w_out[l]

        q = dq.reshape(B, S, N_DIFF_HEADS, 2, DIFF_HEAD_DIM)
        k = dk.reshape(B, S, N_DIFF_HEADS, 2, DIFF_HEAD_DIM)
        q = partial_rope(rmsnorm(q, diff_q_norm_w[l]), positions)
        k = partial_rope(rmsnorm(k, diff_k_norm_w[l]), positions)
        v = dv.reshape(B, S, N_DIFF_HEADS, DIFF_V_DIM)
        lam_init = 0.8 - 0.6 * math.exp(-0.3 * l)
        lam = (jnp.exp(jnp.sum(diff_lq1[l].astype(jnp.float32) * diff_lk1[l].astype(jnp.float32)))
               - jnp.exp(jnp.sum(diff_lq2[l].astype(jnp.float32) * diff_lk2[l].astype(jnp.float32)))
               + lam_init)
        o_b = diff_attention(q, k, v, lam)
        o_b = rmsnorm(o_b, diff_subln_w[l]) * (1 - lam_init)
        y_b = o_b.reshape(B, S, DIFF_V_WIDTH) @ diff_w_out[l]

        mix = jax.nn.sigmoid(ga) * y_a + jax.nn.sigmoid(gb) * y_b
        h = h + g2[:, None, :] * (mix @ mix_w_o[l])

        n3 = modulate(rmsnorm(h, norm3_w[l]), sh3, sc3)
        h = h + 0.5 * g3[:, None, :] * swiglu(n3, ffn2_w_in[l], ffn2_w_out[l])
    return h
```

```python
import functools
import math

import jax
import jax.numpy as jnp
from jax import lax
from jax.experimental import pallas as pl
from jax.experimental.pallas import tpu as pltpu

F32 = jnp.float32
BF16 = jnp.bfloat16

EPS = 1e-6
ROPE_THETA = 500000.0
LANES = 128
VMEM_LIMIT = 56 * 1024 * 1024

_NT = (((1,), (1,)), ((), ()))
_TN = (((0,), (0,)), ((), ()))


def _params(*sem):
    return pltpu.CompilerParams(dimension_semantics=sem, vmem_limit_bytes=VMEM_LIMIT)


def _dot(a, b):
    return jnp.dot(a, b, preferred_element_type=F32)


def _sigmoid(x):
    return 1.0 / (1.0 + jnp.exp(-x))


def _norm_mod(h, nw, sh, sc):
    var = jnp.mean(h * h, axis=-1, keepdims=True)
    return (h * lax.rsqrt(var + EPS) * nw) * (1.0 + sc) + sh


def _ada_kernel(c_ref, w_ref, b_ref, o_ref):
    c = c_ref[...]
    a = (c * _sigmoid(c)).astype(BF16)
    o_ref[...] = _dot(a, w_ref[...].astype(BF16)) + b_ref[...]


def _ada(c, w, b, tn=1024):
    bsz, d = c.shape
    n = w.shape[1]
    rows = 8
    cp = jnp.zeros((rows, d), F32).at[:bsz].set(c)
    out = pl.pallas_call(
        _ada_kernel,
        grid=(n // tn,),
        in_specs=[pl.BlockSpec((rows, d), lambda j: (0, 0)),
                  pl.BlockSpec((d, tn), lambda j: (0, j)),
                  pl.BlockSpec((1, tn), lambda j: (0, j))],
        out_specs=pl.BlockSpec((rows, tn), lambda j: (0, j)),
        out_shape=jax.ShapeDtypeStruct((rows, n), F32),
        compiler_params=_params("parallel"),
        name="adaln",
    )(cp, w, b.reshape(1, n))
    return out[:bsz]


def _ffn_kernel(h_ref, nw_ref, sh_ref, sc_ref, g_ref, wg_ref, wu_ref, wo_ref, o_ref,
                xn_ref, acc_ref):
    f = pl.program_id(1)

    @pl.when(f == 0)
    def _():
        xn = _norm_mod(h_ref[...], nw_ref[...], sh_ref[...], sc_ref[...])
        xn_ref[...] = xn.astype(BF16)
        acc_ref[...] = jnp.zeros_like(acc_ref)

    x = xn_ref[...]
    g = _dot(x, wg_ref[...])
    u = _dot(x, wu_ref[...])
    a = (g * _sigmoid(g)) * u
    acc_ref[...] += _dot(a.astype(BF16), wo_ref[...])

    @pl.when(f == pl.num_programs(1) - 1)
    def _():
        o_ref[...] = h_ref[...] + (0.5 * g_ref[...]) * acc_ref[...]


def _ffn(h, nw, mods, jsh, jsc, jg, w_in, w_out, seq, tm=512, tf=512):
    t, d = h.shape
    dff = w_out.shape[0]
    nf = dff // tf
    per_b = seq // tm

    def mod_spec(j):
        return pl.BlockSpec((None, 1, d), lambda m, f: ((m // per_b) * 9 + j, 0, 0))

    return pl.pallas_call(
        _ffn_kernel,
        grid=(t // tm, nf),
        in_specs=[pl.BlockSpec((tm, d), lambda m, f: (m, 0)),
                  pl.BlockSpec((1, d), lambda m, f: (0, 0)),
                  mod_spec(jsh), mod_spec(jsc), mod_spec(jg),
                  pl.BlockSpec((d, tf), lambda m, f: (0, f)),
                  pl.BlockSpec((d, tf), lambda m, f: (0, f + nf)),
                  pl.BlockSpec((tf, d), lambda m, f: (f, 0))],
        out_specs=pl.BlockSpec((tm, d), lambda m, f: (m, 0)),
        out_shape=jax.ShapeDtypeStruct((t, d), F32),
        scratch_shapes=[pltpu.VMEM((tm, d), BF16), pltpu.VMEM((tm, d), F32)],
        compiler_params=_params("parallel", "arbitrary"),
        name="ffn",
    )(h, nw.reshape(1, d), mods, mods, mods, w_in, w_in, w_out)


def _proj_kernel(h_ref, nw_ref, sh_ref, sc_ref, w_ref, o_ref, xn_ref):
    @pl.when(pl.program_id(1) == 0)
    def _():
        xn = _norm_mod(h_ref[...], nw_ref[...], sh_ref[...], sc_ref[...])
        xn_ref[...] = xn.astype(BF16)

    o_ref[...] = _dot(xn_ref[...], w_ref[...])


def _proj(h, nw, mods, jsh, jsc, w, seq, tm=1024, tn=1024):
    t, d = h.shape
    n = w.shape[1]
    per_b = seq // tm

    def mod_spec(j):
        return pl.BlockSpec((None, 1, d), lambda m, k: ((m // per_b) * 9 + j, 0, 0))

    return pl.pallas_call(
        _proj_kernel,
        grid=(t // tm, n // tn),
        in_specs=[pl.BlockSpec((tm, d), lambda m, k: (m, 0)),
                  pl.BlockSpec((1, d), lambda m, k: (0, 0)),
                  mod_spec(jsh), mod_spec(jsc),
                  pl.BlockSpec((d, tn), lambda m, k: (0, k))],
        out_specs=pl.BlockSpec((tm, tn), lambda m, k: (m, k)),
        out_shape=jax.ShapeDtypeStruct((t, n), F32),
        scratch_shapes=[pltpu.VMEM((tm, d), BF16)],
        compiler_params=_params("parallel", "arbitrary"),
        name="mix_proj",
    )(h, nw.reshape(1, d), mods, mods, w)


def _qkv_kernel(q_ref, k_ref, v_ref, pos_ref, frq_ref, gm_ref, qw_ref, kw_ref, msk_ref,
                qo_ref, ko_ref, vo_ref, *, n_heads, dh, q_scale):
    pos = pos_ref[...].astype(F32)
    ang = pos * frq_ref[...]
    cosv = jnp.cos(ang)
    sinv = jnp.sin(ang)
    m_lo = msk_ref[0:1, :]
    m_hi = msk_ref[1:2, :]
    m_any = m_lo + m_hi
    c_full = m_any * cosv + (1.0 - m_any)
    s_lo = -(m_lo * sinv)
    s_hi = m_hi * sinv
    gm = gm_ref[...]

    def prep(x, w, scale):
        xsq = x * x
        hi = xsq.astype(BF16)
        lo = (xsq - hi.astype(F32)).astype(BF16)
        ss = _dot(hi, gm) + _dot(lo, gm)
        y = x * lax.rsqrt(ss * (1.0 / dh) + EPS) * w
        rot = y * c_full + pltpu.roll(y, LANES - 8, axis=1) * s_lo + pltpu.roll(y, 8, axis=1) * s_hi
        return (rot * scale).astype(BF16)

    qw = qw_ref[...]
    kw = kw_ref[...]
    for h in range(n_heads):
        sl = slice(h * LANES, (h + 1) * LANES)
        qo_ref[:, sl] = prep(q_ref[:, sl], qw, q_scale)
        ko_ref[:, sl] = prep(k_ref[:, sl], kw, 1.0)
    vo_ref[...] = v_ref[...].astype(BF16)


def _qkv(proj, positions, frq, gm, qw, kw, msk, width, col0, dh, tm=512):
    t = proj.shape[0]
    n_heads = width // LANES
    cb = col0 // width
    kern = functools.partial(_qkv_kernel, n_heads=n_heads, dh=dh, q_scale=1.0 / math.sqrt(dh))
    const = lambda shape: pl.BlockSpec(shape, lambda m: (0, 0))
    out = jax.ShapeDtypeStruct((t, width), BF16)
    return pl.pallas_call(
        kern,
        grid=(t // tm,),
        in_specs=[pl.BlockSpec((tm, width), lambda m: (m, cb)),
                  pl.BlockSpec((tm, width), lambda m: (m, cb + 1)),
                  pl.BlockSpec((tm, width), lambda m: (m, cb + 2)),
                  pl.BlockSpec((tm, 1), lambda m: (m, 0)),
                  const((1, LANES)), const((LANES, LANES)), const((1, LANES)), const((1, LANES)),
                  const((2, LANES))],
        out_specs=[pl.BlockSpec((tm, width), lambda m: (m, 0))] * 3,
        out_shape=[out, out, out],
        compiler_params=_params("parallel"),
        name="qkv_prep",
    )(proj, proj, proj, positions, frq, gm, qw, kw, msk)


def _boundary_rows(b_ref, b, level, c):
    h = 1 << level
    if h == 1:
        odd = (lax.broadcasted_iota(jnp.int32, b.shape, 0) & 1) == 1
        return jnp.where(odd, pltpu.roll(b, 1, axis=0), b)
    if h == 2:
        sub = lax.broadcasted_iota(jnp.int32, (8, LANES), 0)
        pieces = []
        for j in range(c // 8):
            lo = jnp.broadcast_to(b_ref[8 * j + 1:8 * j + 2, :], (8, LANES))
            hi = jnp.broadcast_to(b_ref[8 * j + 5:8 * j + 6, :], (8, LANES))
            pieces.append(jnp.where(sub < 4, lo, hi))
        return jnp.concatenate(pieces, axis=0)
    blk = max(2 * h, 8)
    pieces = []
    for j in range(c // blk):
        r = (j * blk // (2 * h)) * 2 * h + h - 1
        pieces.append(jnp.broadcast_to(b_ref[r:r + 1, :], (blk, LANES)))
    return jnp.concatenate(pieces, axis=0)


def _hgrn_kernel(q_ref, f_ref, i_ref, g_ref, lb_ref, nw_ref, o_ref, st_ref, b_ref, *, chunk,
                 layer):
    c = chunk
    rows = q_ref.shape[0]
    n_levels = c.bit_length() - 1

    @pl.when(pl.program_id(2) == 0)
    def _():
        st_ref[...] = jnp.zeros_like(st_ref)

    lbx = lb_ref[...]
    mx = jnp.max(lbx, axis=0, keepdims=True)
    ex = jnp.exp(lbx - mx)
    lb = (jnp.sum(ex[0:layer + 1, :], axis=0, keepdims=True)
          / jnp.sum(ex, axis=0, keepdims=True))
    nw = nw_ref[...]

    ti = lax.broadcasted_iota(jnp.int32, (c, c), 0)
    si = lax.broadcasted_iota(jnp.int32, (c, c), 1)
    lvl = jnp.where(ti > si, 31 - lax.clz(ti ^ si), jnp.where(ti == si, -1, -2))
    tri = (ti >= si).astype(BF16)

    def body(ci, carry):
        r0 = pl.multiple_of(ci * c, c)
        q = q_ref[pl.ds(r0, c), :]
        fp = f_ref[pl.ds(r0, c), :]
        v = i_ref[pl.ds(r0, c), :]
        g = g_ref[pl.ds(r0, c), :]

        f = lb + (1.0 - lb) * _sigmoid(fp)
        lf = jnp.log(f)
        k = 1.0 - f
        p1 = lf.astype(BF16)
        r1 = lf - p1.astype(F32)
        p2 = r1.astype(BF16)
        p3 = (r1 - p2.astype(F32)).astype(BF16)
        b = _dot(tri, p1) + _dot(tri, p2) + _dot(tri, p3)
        b_ref[...] = b
        b_last = b[c - 1:c, :]

        vb = v.astype(BF16)
        st = st_ref[...]
        inter = lax.dot_general((q * jnp.exp(b)).astype(BF16), st.astype(BF16), _NT,
                                preferred_element_type=F32)
        kh = (k * jnp.exp(b_last - b)).astype(BF16)
        st_ref[...] = st * jnp.exp(b_last) + lax.dot_general(vb, kh, _TN,
                                                             preferred_element_type=F32)

        dg = jnp.sum(q * k, axis=-1, keepdims=True)
        scores = jnp.where(lvl == -1, dg, 0.0)
        for level in range(n_levels):
            m = _boundary_rows(b_ref, b, level, c)
            e = jnp.exp(-jnp.abs(b - m))
            s_l = lax.dot_general((q * e).astype(BF16), (k * e).astype(BF16), _NT,
                                  preferred_element_type=F32)
            scores = jnp.where(lvl == level, s_l, scores)
        o = inter + _dot(scores.astype(BF16), vb)

        var = jnp.mean(o * o, axis=-1, keepdims=True)
        on = o * lax.rsqrt(var + EPS) * nw
        o_ref[pl.ds(r0, c), :] = (on * (g * _sigmoid(g))).astype(o_ref.dtype)
        return carry

    lax.fori_loop(0, rows // c, body, 0)


def _hgrn(proj, lb, nw, layer, bsz, seq, n_heads, col0, rows=512, chunk=128):
    t = proj.shape[0]
    kdim = LANES
    ns = seq // rows
    cb = col0 // kdim

    def col_spec(j):
        return pl.BlockSpec((rows, kdim), lambda b, h, s: (b * ns + s, cb + j * n_heads + h))

    return pl.pallas_call(
        functools.partial(_hgrn_kernel, chunk=chunk, layer=layer),
        grid=(bsz, n_heads, ns),
        in_specs=[col_spec(0), col_spec(1), col_spec(2), col_spec(3),
                  pl.BlockSpec((lb.shape[0], kdim), lambda b, h, s: (0, h)),
                  pl.BlockSpec((1, kdim), lambda b, h, s: (0, 0))],
        out_specs=pl.BlockSpec((rows, kdim), lambda b, h, s: (b * ns + s, h)),
        out_shape=jax.ShapeDtypeStruct((t, n_heads * kdim), BF16),
        scratch_shapes=[pltpu.VMEM((kdim, kdim), F32), pltpu.VMEM((chunk, kdim), F32)],
        compiler_params=_params("parallel", "parallel", "arbitrary"),
        name="hgrn2",
    )(proj, proj, proj, proj, lb, nw.reshape(1, kdim))


def _attn_kernel(q_ref, k_ref, v_ref, lam_ref, sw_ref, o_ref, qs_ref, m_ref, l_ref, acc_ref,
                 *, tq, tk, dh, lam_init):
    qi = pl.program_id(2)
    q = q_ref[...]
    lane = lax.broadcasted_iota(jnp.int32, q.shape, 1)
    zero = jnp.zeros_like(q)
    qs_ref[0:tq, :] = jnp.where(lane < dh, q, zero)
    qs_ref[tq:2 * tq, :] = jnp.where(lane < dh, zero, q)
    m_ref[...] = jnp.full_like(m_ref, -jnp.inf)
    l_ref[...] = jnp.zeros_like(l_ref)
    acc_ref[...] = jnp.zeros_like(acc_ref)

    def step(j, masked):
        k0 = pl.multiple_of(j * tk, tk)
        kb = k_ref[pl.ds(k0, tk), :]
        vb = v_ref[pl.ds(k0, tk), :]
        s = lax.dot_general(qs_ref[...], kb, _NT, preferred_element_type=F32)
        if masked:
            row = lax.broadcasted_iota(jnp.int32, s.shape, 0)
            qpos = qi * tq + jnp.where(row >= tq, row - tq, row)
            kpos = k0 + lax.broadcasted_iota(jnp.int32, s.shape, 1)
            s = jnp.where(kpos <= qpos, s, -jnp.inf)
        m_old = m_ref[...]
        m_new = jnp.maximum(m_old, jnp.max(s, axis=-1, keepdims=True))
        alpha = jnp.exp(m_old - m_new)
        p = jnp.exp(s - m_new)
        l_ref[...] = alpha * l_ref[...] + jnp.sum(p, axis=-1, keepdims=True)
        acc_ref[...] = alpha * acc_ref[...] + _dot(p.astype(BF16), vb)
        m_ref[...] = m_new

    n_full = (qi * tq) // tk
    lax.fori_loop(0, n_full, lambda j, c: (step(j, False), c)[1], 0)
    step(n_full, True)

    lv = lam_ref[...]
    lam = (jnp.exp(jnp.sum(lv[0:1, :] * lv[1:2, :], axis=-1, keepdims=True))
           - jnp.exp(jnp.sum(lv[2:3, :] * lv[3:4, :], axis=-1, keepdims=True)) + lam_init)
    o = (acc_ref[0:tq, :] / l_ref[0:tq, :]) - lam * (acc_ref[tq:2 * tq, :] / l_ref[tq:2 * tq, :])
    var = jnp.mean(o * o, axis=-1, keepdims=True)
    o_ref[...] = (o * lax.rsqrt(var + EPS) * sw_ref[...] * (1.0 - lam_init)).astype(o_ref.dtype)


def _attn(q, k, v, lamv, sw, bsz, seq, n_heads, dh, lam_init, tq=256, tk=512):
    t = q.shape[0]
    nq = seq // tq
    kern = functools.partial(_attn_kernel, tq=tq, tk=tk, dh=dh, lam_init=lam_init)
    return pl.pallas_call(
        kern,
        grid=(bsz, n_heads, nq),
        in_specs=[pl.BlockSpec((tq, LANES), lambda b, h, i: (b * nq + i, h)),
                  pl.BlockSpec((seq, LANES), lambda b, h, i: (b, h)),
                  pl.BlockSpec((seq, LANES), lambda b, h, i: (b, h)),
                  pl.BlockSpec(lamv.shape, lambda b, h, i: (0, 0)),
                  pl.BlockSpec((1, LANES), lambda b, h, i: (0, 0))],
        out_specs=pl.BlockSpec((tq, LANES), lambda b, h, i: (b * nq + i, h)),
        out_shape=jax.ShapeDtypeStruct((t, n_heads * LANES), BF16),
        scratch_shapes=[pltpu.VMEM((2 * tq, LANES), BF16),
                        pltpu.VMEM((2 * tq, 1), F32),
                        pltpu.VMEM((2 * tq, 1), F32),
                        pltpu.VMEM((2 * tq, LANES), F32)],
        compiler_params=_params("parallel", "parallel", "arbitrary"),
        name="diff_attn",
    )(q, k, v, lamv, sw.reshape(1, LANES))


def _merge_kernel(oa_ref, ob_ref, wa_ref, wb_ref, ga_ref, gb_ref, o_ref):
    ya = _dot(oa_ref[...], wa_ref[...])
    yb = _dot(ob_ref[...], wb_ref[...])
    o_ref[...] = (_sigmoid(ga_ref[...]) * ya + _sigmoid(gb_ref[...]) * yb).astype(o_ref.dtype)


def _merge(oa, ob, wa, wb, proj, col_ga, col_gb, tm=1024, tn=1024):
    t, kdim = oa.shape
    n = wa.shape[1]
    return pl.pallas_call(
        _merge_kernel,
        grid=(t // tm, n // tn),
        in_specs=[pl.BlockSpec((tm, kdim), lambda m, j: (m, 0)),
                  pl.BlockSpec((tm, kdim), lambda m, j: (m, 0)),
                  pl.BlockSpec((kdim, tn), lambda m, j: (0, j)),
                  pl.BlockSpec((kdim, tn), lambda m, j: (0, j)),
                  pl.BlockSpec((tm, tn), lambda m, j: (m, col_ga // tn + j)),
                  pl.BlockSpec((tm, tn), lambda m, j: (m, col_gb // tn + j))],
        out_specs=pl.BlockSpec((tm, tn), lambda m, j: (m, j)),
        out_shape=jax.ShapeDtypeStruct((t, n), BF16),
        compiler_params=_params("parallel", "arbitrary"),
        name="merge",
    )(oa, ob, wa, wb, proj, proj)


def _oproj_kernel(x_ref, w_ref, h_ref, g_ref, o_ref):
    o_ref[...] = h_ref[...] + g_ref[...] * _dot(x_ref[...], w_ref[...])


def _oproj(x, w, h, mods, jg, seq, tm=1024, tn=1024):
    t, kdim = x.shape
    n = w.shape[1]
    per_b = seq // tm
    return pl.pallas_call(
        _oproj_kernel,
        grid=(t // tm, n // tn),
        in_specs=[pl.BlockSpec((tm, kdim), lambda m, j: (m, 0)),
                  pl.BlockSpec((kdim, tn), lambda m, j: (0, j)),
                  pl.BlockSpec((tm, tn), lambda m, j: (m, j)),
                  pl.BlockSpec((None, 1, tn), lambda m, j: ((m // per_b) * 9 + jg, 0, j))],
        out_specs=pl.BlockSpec((tm, tn), lambda m, j: (m, j)),
        out_shape=jax.ShapeDtypeStruct((t, n), F32),
        compiler_params=_params("parallel", "arbitrary"),
        name="out_proj",
    )(x, w, h, mods)


def kernel(x, c, positions, ada_w, ada_b, norm1_w, ffn1_w_in, ffn1_w_out, norm2_w, mix_w_in, hgrn_lb, hgrn_norm_w, hgrn_w_out, diff_q_norm_w, diff_k_norm_w, diff_lq1, diff_lk1, diff_lq2, diff_lk2, diff_subln_w, diff_w_out, mix_w_o, norm3_w, ffn2_w_in, ffn2_w_out):
    bsz, seq, d = x.shape
    depth = ada_w.shape[0]
    t = bsz * seq
    kdim = hgrn_norm_w.shape[-1]
    fw = hgrn_lb.shape[-1]
    n_hgrn = fw // kdim
    dh = diff_q_norm_w.shape[-1]
    vdim = diff_subln_w.shape[-1]
    qk_w = diff_w_out.shape[1]
    n_diff = qk_w // vdim
    rope_dim = dh // 4
    half = rope_dim // 2
    assert kdim == LANES and vdim == LANES and 2 * dh == LANES

    inv_freq = ROPE_THETA ** (-jnp.arange(half, dtype=F32) / half)
    lane = jnp.arange(LANES)
    dpos = lane % dh
    frq = jnp.where(dpos < rope_dim, inv_freq[dpos % half], 0.0).reshape(1, LANES)
    msk = jnp.stack([(dpos < half), (dpos >= half) & (dpos < rope_dim)]).astype(F32)
    gm = (lane[:, None] // dh == lane[None, :] // dh).astype(BF16)
    pos2 = positions.reshape(t, 1)

    h = x.reshape(t, d)
    for l in range(depth):
        lam_init = 0.8 - 0.6 * math.exp(-0.3 * l)
        mods = _ada(c, ada_w[l], ada_b[l]).reshape(bsz * 9, 1, d)
        h = _ffn(h, norm1_w[l], mods, 0, 1, 2, ffn1_w_in[l].astype(BF16),
                 ffn1_w_out[l].astype(BF16), seq)
        proj = _proj(h, norm2_w[l], mods, 3, 4, mix_w_in[l].astype(BF16), seq)
        o_a = _hgrn(proj, hgrn_lb, hgrn_norm_w[l], l, bsz, seq, n_hgrn, 0)
        qw = jnp.tile(diff_q_norm_w[l], LANES // dh).reshape(1, LANES)
        kw = jnp.tile(diff_k_norm_w[l], LANES // dh).reshape(1, LANES)
        qr, kr, vr = _qkv(proj, pos2, frq, gm, qw, kw, msk, qk_w, 4 * fw, dh)
        lamv = jnp.stack([diff_lq1[l], diff_lk1[l], diff_lq2[l], diff_lk2[l]])
        o_b = _attn(qr, kr, vr, lamv, diff_subln_w[l], bsz, seq, n_diff, dh, lam_init)
        mix = _merge(o_a, o_b, hgrn_w_out[l].astype(BF16), diff_w_out[l].astype(BF16), proj,
                     4 * fw + 3 * qk_w, 4 * fw + 3 * qk_w + d)
        h = _oproj(mix, mix_w_o[l].astype(BF16), h, mods, 5, seq)
        h = _ffn(h, norm3_w[l], mods, 6, 7, 8, ffn2_w_in[l].astype(BF16),
                 ffn2_w_out[l].astype(BF16), seq)
    return h.reshape(bsz, seq, d)
```

```python
import functools
import math

import jax
import jax.numpy as jnp
from jax import lax
from jax.experimental import pallas as pl
from jax.experimental.pallas import tpu as pltpu

F32 = jnp.float32
BF16 = jnp.bfloat16

EPS = 1e-6
ROPE_THETA = 500000.0
LANES = 128
VMEM_LIMIT = 56 * 1024 * 1024

_NT = (((1,), (1,)), ((), ()))
_TN = (((0,), (0,)), ((), ()))


def _params(*sem):
    return pltpu.CompilerParams(dimension_semantics=sem, vmem_limit_bytes=VMEM_LIMIT)


def _dot(a, b):
    return jnp.dot(a, b, preferred_element_type=F32)


def _sigmoid(x):
    return 1.0 / (1.0 + jnp.exp(-x))


def _norm_mod(h, nw, sh, sc):
    var = jnp.mean(h * h, axis=-1, keepdims=True)
    return (h * lax.rsqrt(var + EPS) * nw) * (1.0 + sc) + sh


def _ada_kernel(c_ref, w_ref, b_ref, o_ref):
    c = c_ref[...]
    a = (c * _sigmoid(c)).astype(BF16)
    o_ref[...] = _dot(a, w_ref[...].astype(BF16)) + b_ref[...]


def _ada(c, w, b, tn=1024):
    bsz, d = c.shape
    n = w.shape[1]
    rows = 8
    cp = jnp.zeros((rows, d), F32).at[:bsz].set(c)
    out = pl.pallas_call(
        _ada_kernel,
        grid=(n // tn,),
        in_specs=[pl.BlockSpec((rows, d), lambda j: (0, 0)),
                  pl.BlockSpec((d, tn), lambda j: (0, j)),
                  pl.BlockSpec((1, tn), lambda j: (0, j))],
        out_specs=pl.BlockSpec((rows, tn), lambda j: (0, j)),
        out_shape=jax.ShapeDtypeStruct((rows, n), F32),
        compiler_params=_params("parallel"),
        name="adaln",
    )(cp, w, b.reshape(1, n))
    return out[:bsz]


def _ffn_kernel(h_ref, nw_ref, sh_ref, sc_ref, g_ref, wg_ref, wu_ref, wo_ref, o_ref,
                xn_ref, acc_ref):
    f = pl.program_id(1)

    @pl.when(f == 0)
    def _():
        xn = _norm_mod(h_ref[...], nw_ref[...], sh_ref[...], sc_ref[...])
        xn_ref[...] = xn.astype(BF16)
        acc_ref[...] = jnp.zeros_like(acc_ref)

    x = xn_ref[...]
    g = _dot(x, wg_ref[...])
    u = _dot(x, wu_ref[...])
    a = (g * _sigmoid(g)) * u
    acc_ref[...] += _dot(a.astype(BF16), wo_ref[...])

    @pl.when(f == pl.num_programs(1) - 1)
    def _():
        o_ref[...] = h_ref[...] + (0.5 * g_ref[...]) * acc_ref[...]


def _ffn(h, nw, mods, jsh, jsc, jg, w_in, w_out, seq, tm=512, tf=512):
    t, d = h.shape
    dff = w_out.shape[0]
    nf = dff // tf
    per_b = seq // tm

    def mod_spec(j):
        return pl.BlockSpec((None, 1, d), lambda m, f: ((m // per_b) * 9 + j, 0, 0))

    return pl.pallas_call(
        _ffn_kernel,
        grid=(t // tm, nf),
        in_specs=[pl.BlockSpec((tm, d), lambda m, f: (m, 0)),
                  pl.BlockSpec((1, d), lambda m, f: (0, 0)),
                  mod_spec(jsh), mod_spec(jsc), mod_spec(jg),
                  pl.BlockSpec((d, tf), lambda m, f: (0, f)),
                  pl.BlockSpec((d, tf), lambda m, f: (0, f + nf)),
                  pl.BlockSpec((tf, d), lambda m, f: (f, 0))],
        out_specs=pl.BlockSpec((tm, d), lambda m, f: (m, 0)),
        out_shape=jax.ShapeDtypeStruct((t, d), F32),
        scratch_shapes=[pltpu.VMEM((tm, d), BF16), pltpu.VMEM((tm, d), F32)],
        compiler_params=_params("parallel", "arbitrary"),
        name="ffn",
    )(h, nw.reshape(1, d), mods, mods, mods, w_in, w_in, w_out)


def _proj_kernel(h_ref, nw_ref, sh_ref, sc_ref, w_ref, o_ref, xn_ref):
    @pl.when(pl.program_id(1) == 0)
    def _():
        xn = _norm_mod(h_ref[...], nw_ref[...], sh_ref[...], sc_ref[...])
        xn_ref[...] = xn.astype(BF16)

    o_ref[...] = _dot(xn_ref[...], w_ref[...])


def _proj(h, nw, mods, jsh, jsc, w, seq, tm=1024, tn=1024):
    t, d = h.shape
    n = w.shape[1]
    per_b = seq // tm

    def mod_spec(j):
        return pl.BlockSpec((None, 1, d), lambda m, k: ((m // per_b) * 9 + j, 0, 0))

    return pl.pallas_call(
        _proj_kernel,
        grid=(t // tm, n // tn),
        in_specs=[pl.BlockSpec((tm, d), lambda m, k: (m, 0)),
                  pl.BlockSpec((1, d), lambda m, k: (0, 0)),
                  mod_spec(jsh), mod_spec(jsc),
                  pl.BlockSpec((d, tn), lambda m, k: (0, k))],
        out_specs=pl.BlockSpec((tm, tn), lambda m, k: (m, k)),
        out_shape=jax.ShapeDtypeStruct((t, n), F32),
        scratch_shapes=[pltpu.VMEM((tm, d), BF16)],
        compiler_params=_params("parallel", "arbitrary"),
        name="mix_proj",
    )(h, nw.reshape(1, d), mods, mods, w)


def _qkv_kernel(q_ref, k_ref, v_ref, pos_ref, frq_ref, gm_ref, qw_ref, kw_ref, msk_ref,
                qo_ref, ko_ref, vo_ref, *, n_heads, dh, q_scale):
    pos = pos_ref[...].astype(F32)
    ang = pos * frq_ref[...]
    cosv = jnp.cos(ang)
    sinv = jnp.sin(ang)
    m_lo = msk_ref[0:1, :]
    m_hi = msk_ref[1:2, :]
    m_any = m_lo + m_hi
    c_full = m_any * cosv + (1.0 - m_any)
    s_lo = -(m_lo * sinv)
    s_hi = m_hi * sinv
    gm = gm_ref[...]

    def prep(x, w, scale):
        xsq = x * x
        hi = xsq.astype(BF16)
        lo = (xsq - hi.astype(F32)).astype(BF16)
        ss = _dot(hi, gm) + _dot(lo, gm)
        y = x * lax.rsqrt(ss * (1.0 / dh) + EPS) * w
        rot = y * c_full + pltpu.roll(y, LANES - 8, axis=1) * s_lo + pltpu.roll(y, 8, axis=1) * s_hi
        return (rot * scale).astype(BF16)

    qw = qw_ref[...]
    kw = kw_ref[...]
    for h in range(n_heads):
        sl = slice(h * LANES, (h + 1) * LANES)
        qo_ref[:, sl] = prep(q_ref[:, sl], qw, q_scale)
        ko_ref[:, sl] = prep(k_ref[:, sl], kw, 1.0)
        vo_ref[sl, :] = v_ref[:, sl].T.astype(BF16)


def _qkv(proj, positions, frq, gm, qw, kw, msk, width, col0, dh, tm=512):
    t = proj.shape[0]
    n_heads = width // LANES
    cb = col0 // width
    kern = functools.partial(_qkv_kernel, n_heads=n_heads, dh=dh,
                             q_scale=math.log2(math.e) / math.sqrt(dh))
    const = lambda shape: pl.BlockSpec(shape, lambda m: (0, 0))
    out = jax.ShapeDtypeStruct((t, width), BF16)
    return pl.pallas_call(
        kern,
        grid=(t // tm,),
        in_specs=[pl.BlockSpec((tm, width), lambda m: (m, cb)),
                  pl.BlockSpec((tm, width), lambda m: (m, cb + 1)),
                  pl.BlockSpec((tm, width), lambda m: (m, cb + 2)),
                  pl.BlockSpec((tm, 1), lambda m: (m, 0)),
                  const((1, LANES)), const((LANES, LANES)), const((1, LANES)), const((1, LANES)),
                  const((2, LANES))],
        out_specs=[pl.BlockSpec((tm, width), lambda m: (m, 0)),
                   pl.BlockSpec((tm, width), lambda m: (m, 0)),
                   pl.BlockSpec((width, tm), lambda m: (0, m))],
        out_shape=[out, out, jax.ShapeDtypeStruct((width, t), BF16)],
        compiler_params=_params("parallel"),
        name="qkv_prep",
    )(proj, proj, proj, positions, frq, gm, qw, kw, msk)


def _boundary_rows(b_ref, b, level, c):
    h = 1 << level
    if h == 1:
        odd = (lax.broadcasted_iota(jnp.int32, b.shape, 0) & 1) == 1
        return jnp.where(odd, pltpu.roll(b, 1, axis=0), b)
    if h == 2:
        sub = lax.broadcasted_iota(jnp.int32, (8, LANES), 0)
        pieces = []
        for j in range(c // 8):
            lo = jnp.broadcast_to(b_ref[8 * j + 1:8 * j + 2, :], (8, LANES))
            hi = jnp.broadcast_to(b_ref[8 * j + 5:8 * j + 6, :], (8, LANES))
            pieces.append(jnp.where(sub < 4, lo, hi))
        return jnp.concatenate(pieces, axis=0)
    blk = max(2 * h, 8)
    pieces = []
    for j in range(c // blk):
        r = (j * blk // (2 * h)) * 2 * h + h - 1
        pieces.append(jnp.broadcast_to(b_ref[r:r + 1, :], (blk, LANES)))
    return jnp.concatenate(pieces, axis=0)


def _hgrn_kernel(q_ref, f_ref, i_ref, g_ref, lb_ref, nw_ref, o_ref, st_ref, b_ref, *, chunk,
                 layer):
    c = chunk
    rows = q_ref.shape[0]
    n_levels = c.bit_length() - 1

    @pl.when(pl.program_id(2) == 0)
    def _():
        st_ref[...] = jnp.zeros_like(st_ref)

    lbx = lb_ref[...]
    mx = jnp.max(lbx, axis=0, keepdims=True)
    ex = jnp.exp(lbx - mx)
    lb = (jnp.sum(ex[0:layer + 1, :], axis=0, keepdims=True)
          / jnp.sum(ex, axis=0, keepdims=True))
    nw = nw_ref[...]

    ti = lax.broadcasted_iota(jnp.int32, (c, c), 0)
    si = lax.broadcasted_iota(jnp.int32, (c, c), 1)
    lvl = jnp.where(ti > si, 31 - lax.clz(ti ^ si), jnp.where(ti == si, -1, -2))
    tri = (ti >= si).astype(BF16)

    def body(ci, carry):
        r0 = pl.multiple_of(ci * c, c)
        q = q_ref[pl.ds(r0, c), :]
        fp = f_ref[pl.ds(r0, c), :]
        v = i_ref[pl.ds(r0, c), :]
        g = g_ref[pl.ds(r0, c), :]

        f = lb + (1.0 - lb) * _sigmoid(fp)
        lf = jnp.log(f)
        k = 1.0 - f
        p1 = lf.astype(BF16)
        r1 = lf - p1.astype(F32)
        p2 = r1.astype(BF16)
        p3 = (r1 - p2.astype(F32)).astype(BF16)
        b = _dot(tri, p1) + _dot(tri, p2) + _dot(tri, p3)
        b_ref[...] = b
        b_last = b[c - 1:c, :]

        vb = v.astype(BF16)
        st = st_ref[...]
        inter = lax.dot_general((q * jnp.exp(b)).astype(BF16), st.astype(BF16), _NT,
                                preferred_element_type=F32)
        kh = (k * jnp.exp(b_last - b)).astype(BF16)
        st_ref[...] = st * jnp.exp(b_last) + lax.dot_general(vb, kh, _TN,
                                                             preferred_element_type=F32)

        dg = jnp.sum(q * k, axis=-1, keepdims=True)
        scores = jnp.where(lvl == -1, dg, 0.0)
        for level in range(n_levels):
            m = _boundary_rows(b_ref, b, level, c)
            e = jnp.exp(-jnp.abs(b - m))
            s_l = lax.dot_general((q * e).astype(BF16), (k * e).astype(BF16), _NT,
                                  preferred_element_type=F32)
            scores = jnp.where(lvl == level, s_l, scores)
        o = inter + _dot(scores.astype(BF16), vb)

        var = jnp.mean(o * o, axis=-1, keepdims=True)
        on = o * lax.rsqrt(var + EPS) * nw
        o_ref[pl.ds(r0, c), :] = (on * (g * _sigmoid(g))).astype(o_ref.dtype)
        return carry

    lax.fori_loop(0, rows // c, body, 0)


def _hgrn(proj, lb, nw, layer, bsz, seq, n_heads, col0, rows=512, chunk=128):
    t = proj.shape[0]
    kdim = LANES
    ns = seq // rows
    cb = col0 // kdim

    def col_spec(j):
        return pl.BlockSpec((rows, kdim), lambda b, h, s: (b * ns + s, cb + j * n_heads + h))

    return pl.pallas_call(
        functools.partial(_hgrn_kernel, chunk=chunk, layer=layer),
        grid=(bsz, n_heads, ns),
        in_specs=[col_spec(0), col_spec(1), col_spec(2), col_spec(3),
                  pl.BlockSpec((lb.shape[0], kdim), lambda b, h, s: (0, h)),
                  pl.BlockSpec((1, kdim), lambda b, h, s: (0, 0))],
        out_specs=pl.BlockSpec((rows, kdim), lambda b, h, s: (b * ns + s, h)),
        out_shape=jax.ShapeDtypeStruct((t, n_heads * kdim), BF16),
        scratch_shapes=[pltpu.VMEM((kdim, kdim), F32), pltpu.VMEM((chunk, kdim), F32)],
        compiler_params=_params("parallel", "parallel", "arbitrary"),
        name="hgrn2",
    )(proj, proj, proj, proj, lb, nw.reshape(1, kdim))


def _attn_kernel(q_ref, k_ref, vt_ref, lam_ref, sw_ref, o_ref, qs_ref, m_ref, l_ref, acc_ref,
                 s_ref, *, tq, tk, dh, lam_init):
    qi = pl.program_id(2)
    q = q_ref[...]
    lane = lax.broadcasted_iota(jnp.int32, q.shape, 1)
    zero = jnp.zeros_like(q)
    qs_ref[0:tq, :] = jnp.where(lane < dh, q, zero)
    qs_ref[tq:2 * tq, :] = jnp.where(lane < dh, zero, q)
    m_ref[...] = jnp.full_like(m_ref, -jnp.inf)
    l_ref[...] = jnp.zeros_like(l_ref)
    acc_ref[...] = jnp.zeros_like(acc_ref)

    def scores(j, slot):
        kb = k_ref[pl.ds(pl.multiple_of(j * tk, tk), tk), :]
        for g in range(2):
            s_ref[slot, :, g * tq:(g + 1) * tq] = lax.dot_general(
                kb, qs_ref[g * tq:(g + 1) * tq, :], _NT, preferred_element_type=F32)

    def step(j, slot, masked):
        k0 = pl.multiple_of(j * tk, tk)
        vt = vt_ref[:, pl.ds(k0, tk)]
        for g in range(2):
            cols = slice(g * tq, (g + 1) * tq)
            st = s_ref[slot, :, cols]
            if masked:
                kpos = k0 + lax.broadcasted_iota(jnp.int32, st.shape, 0)
                qpos = qi * tq + lax.broadcasted_iota(jnp.int32, st.shape, 1)
                st = jnp.where(kpos <= qpos, st, -jnp.inf)
            m_old = m_ref[:, cols]
            m_new = jnp.maximum(m_old, jnp.max(st, axis=0, keepdims=True))
            alpha = jnp.exp2(m_old - m_new)
            pt = jnp.exp2(st - m_new)
            l_ref[:, cols] = alpha * l_ref[:, cols] + jnp.sum(pt, axis=0, keepdims=True)
            acc_ref[:, cols] = alpha * acc_ref[:, cols] + _dot(vt, pt.astype(BF16))
            m_ref[:, cols] = m_new

    n_full = (qi * tq) // tk
    odd = n_full % 2

    @pl.when(odd == 0)
    def _():
        scores(0, 0)

    @pl.when(odd == 1)
    def _():
        scores(0, 1)
        scores(1, 0)
        step(0, 1, False)

    def pair(i, carry):
        j = odd + 2 * i
        scores(j + 1, 1)
        step(j, 0, False)
        scores(j + 2, 0)
        step(j + 1, 1, False)
        return carry

    lax.fori_loop(0, n_full // 2, pair, 0)
    step(n_full, 0, True)

    lv = lam_ref[...]
    lam = (jnp.exp(jnp.sum(lv[0:1, :] * lv[1:2, :], axis=-1, keepdims=True))
           - jnp.exp(jnp.sum(lv[2:3, :] * lv[3:4, :], axis=-1, keepdims=True)) + lam_init)
    ot = acc_ref[...] / l_ref[...]
    o = (ot[:, 0:tq] - lam * ot[:, tq:2 * tq]).T
    var = jnp.mean(o * o, axis=-1, keepdims=True)
    o_ref[...] = (o * lax.rsqrt(var + EPS) * sw_ref[...] * (1.0 - lam_init)).astype(o_ref.dtype)


def _attn(q, k, vt, lamv, sw, bsz, seq, n_heads, dh, lam_init, tq=256, tk=512):
    t = q.shape[0]
    nq = seq // tq
    kern = functools.partial(_attn_kernel, tq=tq, tk=tk, dh=dh, lam_init=lam_init)
    return pl.pallas_call(
        kern,
        grid=(bsz, n_heads, nq),
        in_specs=[pl.BlockSpec((tq, LANES), lambda b, h, i: (b * nq + i, h)),
                  pl.BlockSpec((seq, LANES), lambda b, h, i: (b, h)),
                  pl.BlockSpec((LANES, seq), lambda b, h, i: (h, b)),
                  pl.BlockSpec(lamv.shape, lambda b, h, i: (0, 0)),
                  pl.BlockSpec((1, LANES), lambda b, h, i: (0, 0))],
        out_specs=pl.BlockSpec((tq, LANES), lambda b, h, i: (b * nq + i, h)),
        out_shape=jax.ShapeDtypeStruct((t, n_heads * LANES), BF16),
        scratch_shapes=[pltpu.VMEM((2 * tq, LANES), BF16),
                        pltpu.VMEM((1, 2 * tq), F32),
                        pltpu.VMEM((1, 2 * tq), F32),
                        pltpu.VMEM((LANES, 2 * tq), F32),
                        pltpu.VMEM((2, tk, 2 * tq), F32)],
        compiler_params=_params("parallel", "parallel", "arbitrary"),
        name="diff_attn",
    )(q, k, vt, lamv, sw.reshape(1, LANES))


def _merge_kernel(oa_ref, ob_ref, wa_ref, wb_ref, ga_ref, gb_ref, o_ref):
    ya = _dot(oa_ref[...], wa_ref[...])
    yb = _dot(ob_ref[...], wb_ref[...])
    o_ref[...] = (_sigmoid(ga_ref[...]) * ya + _sigmoid(gb_ref[...]) * yb).astype(o_ref.dtype)


def _merge(oa, ob, wa, wb, proj, col_ga, col_gb, tm=1024, tn=1024):
    t, kdim = oa.shape
    n = wa.shape[1]
    return pl.pallas_call(
        _merge_kernel,
        grid=(t // tm, n // tn),
        in_specs=[pl.BlockSpec((tm, kdim), lambda m, j: (m, 0)),
                  pl.BlockSpec((tm, kdim), lambda m, j: (m, 0)),
                  pl.BlockSpec((kdim, tn), lambda m, j: (0, j)),
                  pl.BlockSpec((kdim, tn), lambda m, j: (0, j)),
                  pl.BlockSpec((tm, tn), lambda m, j: (m, col_ga // tn + j)),
                  pl.BlockSpec((tm, tn), lambda m, j: (m, col_gb // tn + j))],
        out_specs=pl.BlockSpec((tm, tn), lambda m, j: (m, j)),
        out_shape=jax.ShapeDtypeStruct((t, n), BF16),
        compiler_params=_params("parallel", "arbitrary"),
        name="merge",
    )(oa, ob, wa, wb, proj, proj)


def _oproj_kernel(x_ref, w_ref, h_ref, g_ref, o_ref):
    o_ref[...] = h_ref[...] + g_ref[...] * _dot(x_ref[...], w_ref[...])


def _oproj(x, w, h, mods, jg, seq, tm=1024, tn=1024):
    t, kdim = x.shape
    n = w.shape[1]
    per_b = seq // tm
    return pl.pallas_call(
        _oproj_kernel,
        grid=(t // tm, n // tn),
        in_specs=[pl.BlockSpec((tm, kdim), lambda m, j: (m, 0)),
                  pl.BlockSpec((kdim, tn), lambda m, j: (0, j)),
                  pl.BlockSpec((tm, tn), lambda m, j: (m, j)),
                  pl.BlockSpec((None, 1, tn), lambda m, j: ((m // per_b) * 9 + jg, 0, j))],
        out_specs=pl.BlockSpec((tm, tn), lambda m, j: (m, j)),
        out_shape=jax.ShapeDtypeStruct((t, n), F32),
        compiler_params=_params("parallel", "arbitrary"),
        name="out_proj",
    )(x, w, h, mods)


def kernel(x, c, positions, ada_w, ada_b, norm1_w, ffn1_w_in, ffn1_w_out, norm2_w, mix_w_in, hgrn_lb, hgrn_norm_w, hgrn_w_out, diff_q_norm_w, diff_k_norm_w, diff_lq1, diff_lk1, diff_lq2, diff_lk2, diff_subln_w, diff_w_out, mix_w_o, norm3_w, ffn2_w_in, ffn2_w_out):
    bsz, seq, d = x.shape
    depth = ada_w.shape[0]
    t = bsz * seq
    kdim = hgrn_norm_w.shape[-1]
    fw = hgrn_lb.shape[-1]
    n_hgrn = fw // kdim
    dh = diff_q_norm_w.shape[-1]
    vdim = diff_subln_w.shape[-1]
    qk_w = diff_w_out.shape[1]
    n_diff = qk_w // vdim
    rope_dim = dh // 4
    half = rope_dim // 2
    assert kdim == LANES and vdim == LANES and 2 * dh == LANES

    inv_freq = ROPE_THETA ** (-jnp.arange(half, dtype=F32) / half)
    lane = jnp.arange(LANES)
    dpos = lane % dh
    frq = jnp.where(dpos < rope_dim, inv_freq[dpos % half], 0.0).reshape(1, LANES)
    msk = jnp.stack([(dpos < half), (dpos >= half) & (dpos < rope_dim)]).astype(F32)
    gm = (lane[:, None] // dh == lane[None, :] // dh).astype(BF16)
    pos2 = positions.reshape(t, 1)

    h = x.reshape(t, d)
    for l in range(depth):
        lam_init = 0.8 - 0.6 * math.exp(-0.3 * l)
        mods = _ada(c, ada_w[l], ada_b[l]).reshape(bsz * 9, 1, d)
        h = _ffn(h, norm1_w[l], mods, 0, 1, 2, ffn1_w_in[l].astype(BF16),
                 ffn1_w_out[l].astype(BF16), seq)
        proj = _proj(h, norm2_w[l], mods, 3, 4, mix_w_in[l].astype(BF16), seq)
        o_a = _hgrn(proj, hgrn_lb, hgrn_norm_w[l], l, bsz, seq, n_hgrn, 0)
        qw = jnp.tile(diff_q_norm_w[l], LANES // dh).reshape(1, LANES)
        kw = jnp.tile(diff_k_norm_w[l], LANES // dh).reshape(1, LANES)
        qr, kr, vr = _qkv(proj, pos2, frq, gm, qw, kw, msk, qk_w, 4 * fw, dh)
        lamv = jnp.stack([diff_lq1[l], diff_lk1[l], diff_lq2[l], diff_lk2[l]])
        o_b = _attn(qr, kr, vr, lamv, diff_subln_w[l], bsz, seq, n_diff, dh, lam_init)
        mix = _merge(o_a, o_b, hgrn_w_out[l].astype(BF16), diff_w_out[l].astype(BF16), proj,
                     4 * fw + 3 * qk_w, 4 * fw + 3 * qk_w + d)
        h = _oproj(mix, mix_w_o[l].astype(BF16), h, mods, 5, seq)
        h = _ffn(h, norm3_w[l], mods, 6, 7, 8, ffn2_w_in[l].astype(BF16),
                 ffn2_w_out[l].astype(BF16), seq)
    return h.reshape(bsz, seq, d)
```

```python
import functools
import math

import jax
import jax.numpy as jnp
from jax import lax
from jax.experimental import pallas as pl
from jax.experimental.pallas import tpu as pltpu

F32 = jnp.float32
BF16 = jnp.bfloat16

EPS = 1e-6
ROPE_THETA = 500000.0
LANES = 128
VMEM_LIMIT = 56 * 1024 * 1024

_NT = (((1,), (1,)), ((), ()))
_TN = (((0,), (0,)), ((), ()))


def _params(*sem):
    return pltpu.CompilerParams(dimension_semantics=sem, vmem_limit_bytes=VMEM_LIMIT)


def _dot(a, b):
    return jnp.dot(a, b, preferred_element_type=F32)


def _sigmoid(x):
    return 0.5 * jnp.tanh(0.5 * x) + 0.5


def _neg_abs(x):
    bits = lax.bitcast_convert_type(x, jnp.uint32) | jnp.uint32(0x80000000)
    return lax.bitcast_convert_type(bits, F32)


def _norm_mod(h, nw, sh, sc):
    var = jnp.mean(h * h, axis=-1, keepdims=True)
    return (h * lax.rsqrt(var + EPS) * nw) * (1.0 + sc) + sh


def _ada_kernel(c_ref, w_ref, b_ref, o_ref):
    c = c_ref[...]
    a = (c * _sigmoid(c)).astype(BF16)
    o_ref[...] = _dot(a, w_ref[...].astype(BF16)) + b_ref[...]


def _ada(c, w, b, tn=1024):
    bsz, d = c.shape
    n = w.shape[1]
    rows = 8
    cp = jnp.zeros((rows, d), F32).at[:bsz].set(c)
    out = pl.pallas_call(
        _ada_kernel,
        grid=(n // tn,),
        in_specs=[pl.BlockSpec((rows, d), lambda j: (0, 0)),
                  pl.BlockSpec((d, tn), lambda j: (0, j)),
                  pl.BlockSpec((1, tn), lambda j: (0, j))],
        out_specs=pl.BlockSpec((rows, tn), lambda j: (0, j)),
        out_shape=jax.ShapeDtypeStruct((rows, n), F32),
        compiler_params=_params("parallel"),
        name="adaln",
    )(cp, w, b.reshape(1, n))
    return out[:bsz]


def _ffn_kernel(h_ref, nw_ref, sh_ref, sc_ref, g_ref, wg_ref, wu_ref, wo_ref, o_ref,
                xn_ref, acc_ref):
    f = pl.program_id(1)

    @pl.when(f == 0)
    def _():
        xn = _norm_mod(h_ref[...], nw_ref[...], sh_ref[...], sc_ref[...])
        xn_ref[...] = xn.astype(BF16)
        acc_ref[...] = jnp.zeros_like(acc_ref)

    x = xn_ref[...]
    g = _dot(x, wg_ref[...])
    u = _dot(x, wu_ref[...])
    a = (g * _sigmoid(g)) * u
    acc_ref[...] += _dot(a.astype(BF16), wo_ref[...])

    @pl.when(f == pl.num_programs(1) - 1)
    def _():
        o_ref[...] = h_ref[...] + (0.5 * g_ref[...]) * acc_ref[...]


def _ffn(h, nw, mods, jsh, jsc, jg, w_in, w_out, seq, tm=512, tf=512):
    t, d = h.shape
    dff = w_out.shape[0]
    nf = dff // tf
    per_b = seq // tm

    def mod_spec(j):
        return pl.BlockSpec((None, 1, d), lambda m, f: ((m // per_b) * 9 + j, 0, 0))

    return pl.pallas_call(
        _ffn_kernel,
        grid=(t // tm, nf),
        in_specs=[pl.BlockSpec((tm, d), lambda m, f: (m, 0)),
                  pl.BlockSpec((1, d), lambda m, f: (0, 0)),
                  mod_spec(jsh), mod_spec(jsc), mod_spec(jg),
                  pl.BlockSpec((d, tf), lambda m, f: (0, f)),
                  pl.BlockSpec((d, tf), lambda m, f: (0, f + nf)),
                  pl.BlockSpec((tf, d), lambda m, f: (f, 0))],
        out_specs=pl.BlockSpec((tm, d), lambda m, f: (m, 0)),
        out_shape=jax.ShapeDtypeStruct((t, d), F32),
        scratch_shapes=[pltpu.VMEM((tm, d), BF16), pltpu.VMEM((tm, d), F32)],
        compiler_params=_params("parallel", "arbitrary"),
        name="ffn",
    )(h, nw.reshape(1, d), mods, mods, mods, w_in, w_in, w_out)


def _proj_kernel(h_ref, nw_ref, sh_ref, sc_ref, w_ref, o_ref, xn_ref):
    @pl.when(pl.program_id(1) == 0)
    def _():
        xn = _norm_mod(h_ref[...], nw_ref[...], sh_ref[...], sc_ref[...])
        xn_ref[...] = xn.astype(BF16)

    o_ref[...] = _dot(xn_ref[...], w_ref[...])


def _proj(h, nw, mods, jsh, jsc, w, seq, tm=1024, tn=1024):
    t, d = h.shape
    n = w.shape[1]
    per_b = seq // tm

    def mod_spec(j):
        return pl.BlockSpec((None, 1, d), lambda m, k: ((m // per_b) * 9 + j, 0, 0))

    return pl.pallas_call(
        _proj_kernel,
        grid=(t // tm, n // tn),
        in_specs=[pl.BlockSpec((tm, d), lambda m, k: (m, 0)),
                  pl.BlockSpec((1, d), lambda m, k: (0, 0)),
                  mod_spec(jsh), mod_spec(jsc),
                  pl.BlockSpec((d, tn), lambda m, k: (0, k))],
        out_specs=pl.BlockSpec((tm, tn), lambda m, k: (m, k)),
        out_shape=jax.ShapeDtypeStruct((t, n), F32),
        scratch_shapes=[pltpu.VMEM((tm, d), BF16)],
        compiler_params=_params("parallel", "arbitrary"),
        name="mix_proj",
    )(h, nw.reshape(1, d), mods, mods, w)


def _qkv_kernel(q_ref, k_ref, v_ref, pos_ref, frq_ref, gm_ref, qw_ref, kw_ref, msk_ref,
                qo_ref, ko_ref, vo_ref, *, n_heads, dh, q_scale):
    pos = pos_ref[...].astype(F32)
    ang = pos * frq_ref[...]
    cosv = jnp.cos(ang)
    sinv = jnp.sin(ang)
    m_lo = msk_ref[0:1, :]
    m_hi = msk_ref[1:2, :]
    m_any = m_lo + m_hi
    c_full = m_any * cosv + (1.0 - m_any)
    s_lo = -(m_lo * sinv)
    s_hi = m_hi * sinv
    gm = gm_ref[...]

    def prep(x, w, scale):
        xsq = x * x
        hi = xsq.astype(BF16)
        lo = (xsq - hi.astype(F32)).astype(BF16)
        ss = _dot(hi, gm) + _dot(lo, gm)
        y = x * lax.rsqrt(ss * (1.0 / dh) + EPS) * w
        rot = y * c_full + pltpu.roll(y, LANES - 8, axis=1) * s_lo + pltpu.roll(y, 8, axis=1) * s_hi
        return (rot * scale).astype(BF16)

    qw = qw_ref[...]
    kw = kw_ref[...]
    for h in range(n_heads):
        sl = slice(h * LANES, (h + 1) * LANES)
        qo_ref[:, sl] = prep(q_ref[:, sl], qw, q_scale)
        ko_ref[:, sl] = prep(k_ref[:, sl], kw, 1.0)
        vo_ref[sl, :] = v_ref[:, sl].T.astype(BF16)


def _qkv(proj, positions, frq, gm, qw, kw, msk, width, col0, dh, tm=512):
    t = proj.shape[0]
    n_heads = width // LANES
    cb = col0 // width
    kern = functools.partial(_qkv_kernel, n_heads=n_heads, dh=dh,
                             q_scale=math.log2(math.e) / math.sqrt(dh))
    const = lambda shape: pl.BlockSpec(shape, lambda m: (0, 0))
    out = jax.ShapeDtypeStruct((t, width), BF16)
    return pl.pallas_call(
        kern,
        grid=(t // tm,),
        in_specs=[pl.BlockSpec((tm, width), lambda m: (m, cb)),
                  pl.BlockSpec((tm, width), lambda m: (m, cb + 1)),
                  pl.BlockSpec((tm, width), lambda m: (m, cb + 2)),
                  pl.BlockSpec((tm, 1), lambda m: (m, 0)),
                  const((1, LANES)), const((LANES, LANES)), const((1, LANES)), const((1, LANES)),
                  const((2, LANES))],
        out_specs=[pl.BlockSpec((tm, width), lambda m: (m, 0)),
                   pl.BlockSpec((tm, width), lambda m: (m, 0)),
                   pl.BlockSpec((width, tm), lambda m: (0, m))],
        out_shape=[out, out, jax.ShapeDtypeStruct((width, t), BF16)],
        compiler_params=_params("parallel"),
        name="qkv_prep",
    )(proj, proj, proj, positions, frq, gm, qw, kw, msk)


def _boundary_rows(b_ref, b, level, c):
    h = 1 << level
    if h == 1:
        odd = (lax.broadcasted_iota(jnp.int32, b.shape, 0) & 1) == 1
        return jnp.where(odd, pltpu.roll(b, 1, axis=0), b)
    if h == 2:
        sub = lax.broadcasted_iota(jnp.int32, (8, LANES), 0)
        pieces = []
        for j in range(c // 8):
            lo = jnp.broadcast_to(b_ref[8 * j + 1:8 * j + 2, :], (8, LANES))
            hi = jnp.broadcast_to(b_ref[8 * j + 5:8 * j + 6, :], (8, LANES))
            pieces.append(jnp.where(sub < 4, lo, hi))
        return jnp.concatenate(pieces, axis=0)
    blk = max(2 * h, 8)
    pieces = []
    for j in range(c // blk):
        r = (j * blk // (2 * h)) * 2 * h + h - 1
        pieces.append(jnp.broadcast_to(b_ref[r:r + 1, :], (blk, LANES)))
    return jnp.concatenate(pieces, axis=0)


def _hgrn_kernel(q_ref, f_ref, i_ref, g_ref, lb_ref, nw_ref, tri_ref, o_ref, st_ref, b_ref, *,
                 chunk, layer, heads):
    c = chunk
    rows = q_ref.shape[0]
    n_levels = c.bit_length() - 1
    hs = range(heads)

    @pl.when(pl.program_id(2) == 0)
    def _():
        st_ref[...] = jnp.zeros_like(st_ref)

    lbx = lb_ref[...]
    mx = jnp.max(lbx, axis=0, keepdims=True)
    ex = jnp.exp(lbx - mx)
    lb_all = (jnp.sum(ex[0:layer + 1, :], axis=0, keepdims=True)
              / jnp.sum(ex, axis=0, keepdims=True))
    nw = nw_ref[...]

    ti = lax.broadcasted_iota(jnp.int32, (c, c), 0)
    si = lax.broadcasted_iota(jnp.int32, (c, c), 1)
    lvl = jnp.where(ti > si, 31 - lax.clz(ti ^ si), jnp.where(ti == si, -1, -2))

    def body(ci, carry):
        r0 = pl.multiple_of(ci * c, c)
        tri = tri_ref[...]
        sl = [slice(h * LANES, (h + 1) * LANES) for h in hs]
        q = [q_ref[pl.ds(r0, c), sl[h]] for h in hs]
        v = [i_ref[pl.ds(r0, c), sl[h]] for h in hs]
        lf, k = [], []
        for h in hs:
            lb = lb_all[:, sl[h]]
            f = lb + (1.0 - lb) * _sigmoid(f_ref[pl.ds(r0, c), sl[h]])
            lf.append(jnp.log2(f))
            k.append(1.0 - f)
        parts = []
        for h in hs:
            p1 = lf[h].astype(BF16)
            parts += [p1, (lf[h] - p1.astype(F32)).astype(BF16)]
        cs = _dot(tri, jnp.concatenate(parts, axis=1))
        b = []
        for h in hs:
            o2 = 2 * h * LANES
            b.append(cs[:, o2:o2 + LANES] + cs[:, o2 + LANES:o2 + 2 * LANES])
            b_ref[h] = b[h]
        vb = [v[h].astype(BF16) for h in hs]
        qb = [q[h].astype(BF16) for h in hs]
        kb = [k[h].astype(BF16) for h in hs]
        inter = []
        for h in hs:
            b_last = b[h][c - 1:c, :]
            st = st_ref[h]
            inter.append(lax.dot_general((q[h] * jnp.exp2(b[h])).astype(BF16), st.astype(BF16),
                                         _NT, preferred_element_type=F32))
            kh = (k[h] * jnp.exp2(b_last - b[h])).astype(BF16)
            st_ref[h] = st * jnp.exp2(b_last) + lax.dot_general(vb[h], kh, _TN,
                                                                preferred_element_type=F32)
        scores = [jnp.where(lvl == -1, jnp.sum(q[h] * k[h], axis=-1, keepdims=True), 0.0)
                  for h in hs]
        for level in range(n_levels):
            for h in hs:
                m = _boundary_rows(b_ref.at[h], b[h], level, c)
                e = jnp.exp2(_neg_abs(b[h] - m)).astype(BF16)
                s_l = lax.dot_general(qb[h] * e, kb[h] * e, _NT, preferred_element_type=F32)
                scores[h] = jnp.where(lvl == level, s_l, scores[h])
        for h in hs:
            o = inter[h] + _dot(scores[h].astype(BF16), vb[h])
            var = jnp.mean(o * o, axis=-1, keepdims=True)
            on = o * lax.rsqrt(var + EPS) * nw
            g = g_ref[pl.ds(r0, c), sl[h]]
            o_ref[pl.ds(r0, c), sl[h]] = (on * (g * _sigmoid(g))).astype(o_ref.dtype)
        return carry

    lax.fori_loop(0, rows // c, body, 0)


def _hgrn(proj, lb, nw, layer, bsz, seq, n_heads, col0, rows=512, chunk=128, heads=4):
    t = proj.shape[0]
    kdim = LANES
    width = heads * kdim
    ns = seq // rows
    ng = n_heads // heads
    cb = col0 // width

    def col_spec(j):
        return pl.BlockSpec((rows, width), lambda b, h, s: (b * ns + s, cb + j * ng + h))

    return pl.pallas_call(
        functools.partial(_hgrn_kernel, chunk=chunk, layer=layer, heads=heads),
        grid=(bsz, ng, ns),
        in_specs=[col_spec(0), col_spec(1), col_spec(2), col_spec(3),
                  pl.BlockSpec((lb.shape[0], width), lambda b, h, s: (0, h)),
                  pl.BlockSpec((1, kdim), lambda b, h, s: (0, 0)),
                  pl.BlockSpec((chunk, chunk), lambda b, h, s: (0, 0))],
        out_specs=pl.BlockSpec((rows, width), lambda b, h, s: (b * ns + s, h)),
        out_shape=jax.ShapeDtypeStruct((t, n_heads * kdim), BF16),
        scratch_shapes=[pltpu.VMEM((heads, kdim, kdim), F32),
                        pltpu.VMEM((heads, chunk, kdim), F32)],
        compiler_params=_params("parallel", "parallel", "arbitrary"),
        name="hgrn2",
    )(proj, proj, proj, proj, lb, nw.reshape(1, kdim), jnp.tril(jnp.ones((chunk, chunk), BF16)))


def _attn_kernel(q_ref, k_ref, vt_ref, lam_ref, sw_ref, o_ref, qs_ref, m_ref, l_ref, acc_ref,
                 s_ref, *, tq, tk, dh, lam_init):
    qi = pl.program_id(2)
    q = q_ref[...]
    lane = lax.broadcasted_iota(jnp.int32, q.shape, 1)
    zero = jnp.zeros_like(q)
    qs_ref[0:tq, :] = jnp.where(lane < dh, q, zero)
    qs_ref[tq:2 * tq, :] = jnp.where(lane < dh, zero, q)
    m_ref[...] = jnp.full_like(m_ref, -jnp.inf)
    l_ref[...] = jnp.zeros_like(l_ref)
    acc_ref[...] = jnp.zeros_like(acc_ref)

    def scores(j, slot):
        kb = k_ref[pl.ds(pl.multiple_of(j * tk, tk), tk), :]
        for g in range(2):
            s_ref[slot, :, g * tq:(g + 1) * tq] = lax.dot_general(
                kb, qs_ref[g * tq:(g + 1) * tq, :], _NT, preferred_element_type=F32)

    def step(j, slot, masked):
        k0 = pl.multiple_of(j * tk, tk)
        vt = vt_ref[:, pl.ds(k0, tk)]
        for g in range(2):
            cols = slice(g * tq, (g + 1) * tq)
            st = s_ref[slot, :, cols]
            if masked:
                kpos = k0 + lax.broadcasted_iota(jnp.int32, st.shape, 0)
                qpos = qi * tq + lax.broadcasted_iota(jnp.int32, st.shape, 1)
                st = jnp.where(kpos <= qpos, st, -jnp.inf)
            m_old = m_ref[:, cols]
            m_new = jnp.maximum(m_old, jnp.max(st, axis=0, keepdims=True))
            alpha = jnp.exp2(m_old - m_new)
            pt = jnp.exp2(st - m_new)
            l_ref[:, cols] = alpha * l_ref[:, cols] + jnp.sum(pt, axis=0, keepdims=True)
            acc_ref[:, cols] = alpha * acc_ref[:, cols] + _dot(vt, pt.astype(BF16))
            m_ref[:, cols] = m_new

    n_full = (qi * tq) // tk
    odd = n_full % 2

    @pl.when(odd == 0)
    def _():
        scores(0, 0)

    @pl.when(odd == 1)
    def _():
        scores(0, 1)
        scores(1, 0)
        step(0, 1, False)

    def pair(i, carry):
        j = odd + 2 * i
        scores(j + 1, 1)
        step(j, 0, False)
        scores(j + 2, 0)
        step(j + 1, 1, False)
        return carry

    lax.fori_loop(0, n_full // 2, pair, 0)
    step(n_full, 0, True)

    lv = lam_ref[...]
    lam = (jnp.exp(jnp.sum(lv[0:1, :] * lv[1:2, :], axis=-1, keepdims=True))
           - jnp.exp(jnp.sum(lv[2:3, :] * lv[3:4, :], axis=-1, keepdims=True)) + lam_init)
    ot = acc_ref[...] / l_ref[...]
    o = (ot[:, 0:tq] - lam * ot[:, tq:2 * tq]).T
    var = jnp.mean(o * o, axis=-1, keepdims=True)
    o_ref[...] = (o * lax.rsqrt(var + EPS) * sw_ref[...] * (1.0 - lam_init)).astype(o_ref.dtype)


def _attn(q, k, vt, lamv, sw, bsz, seq, n_heads, dh, lam_init, tq=512, tk=512):
    t = q.shape[0]
    nq = seq // tq
    kern = functools.partial(_attn_kernel, tq=tq, tk=tk, dh=dh, lam_init=lam_init)
    return pl.pallas_call(
        kern,
        grid=(bsz, n_heads, nq),
        in_specs=[pl.BlockSpec((tq, LANES), lambda b, h, i: (b * nq + i, h)),
                  pl.BlockSpec((seq, LANES), lambda b, h, i: (b, h)),
                  pl.BlockSpec((LANES, seq), lambda b, h, i: (h, b)),
                  pl.BlockSpec(lamv.shape, lambda b, h, i: (0, 0)),
                  pl.BlockSpec((1, LANES), lambda b, h, i: (0, 0))],
        out_specs=pl.BlockSpec((tq, LANES), lambda b, h, i: (b * nq + i, h)),
        out_shape=jax.ShapeDtypeStruct((t, n_heads * LANES), BF16),
        scratch_shapes=[pltpu.VMEM((2 * tq, LANES), BF16),
                        pltpu.VMEM((1, 2 * tq), F32),
                        pltpu.VMEM((1, 2 * tq), F32),
                        pltpu.VMEM((LANES, 2 * tq), F32),
                        pltpu.VMEM((2, tk, 2 * tq), F32)],
        compiler_params=_params("parallel", "parallel", "arbitrary"),
        name="diff_attn",
    )(q, k, vt, lamv, sw.reshape(1, LANES))


def _merge_kernel(oa_ref, ob_ref, wa_ref, wb_ref, ga_ref, gb_ref, o_ref):
    ya = _dot(oa_ref[...], wa_ref[...])
    yb = _dot(ob_ref[...], wb_ref[...])
    o_ref[...] = (_sigmoid(ga_ref[...]) * ya + _sigmoid(gb_ref[...]) * yb).astype(o_ref.dtype)


def _merge(oa, ob, wa, wb, proj, col_ga, col_gb, tm=1024, tn=1024):
    t, kdim = oa.shape
    n = wa.shape[1]
    return pl.pallas_call(
        _merge_kernel,
        grid=(t // tm, n // tn),
        in_specs=[pl.BlockSpec((tm, kdim), lambda m, j: (m, 0)),
                  pl.BlockSpec((tm, kdim), lambda m, j: (m, 0)),
                  pl.BlockSpec((kdim, tn), lambda m, j: (0, j)),
                  pl.BlockSpec((kdim, tn), lambda m, j: (0, j)),
                  pl.BlockSpec((tm, tn), lambda m, j: (m, col_ga // tn + j)),
                  pl.BlockSpec((tm, tn), lambda m, j: (m, col_gb // tn + j))],
        out_specs=pl.BlockSpec((tm, tn), lambda m, j: (m, j)),
        out_shape=jax.ShapeDtypeStruct((t, n), BF16),
        compiler_params=_params("parallel", "arbitrary"),
        name="merge",
    )(oa, ob, wa, wb, proj, proj)


def _oproj_kernel(x_ref, w_ref, h_ref, g_ref, o_ref):
    o_ref[...] = h_ref[...] + g_ref[...] * _dot(x_ref[...], w_ref[...])


def _oproj(x, w, h, mods, jg, seq, tm=1024, tn=1024):
    t, kdim = x.shape
    n = w.shape[1]
    per_b = seq // tm
    return pl.pallas_call(
        _oproj_kernel,
        grid=(t // tm, n // tn),
        in_specs=[pl.BlockSpec((tm, kdim), lambda m, j: (m, 0)),
                  pl.BlockSpec((kdim, tn), lambda m, j: (0, j)),
                  pl.BlockSpec((tm, tn), lambda m, j: (m, j)),
                  pl.BlockSpec((None, 1, tn), lambda m, j: ((m // per_b) * 9 + jg, 0, j))],
        out_specs=pl.BlockSpec((tm, tn), lambda m, j: (m, j)),
        out_shape=jax.ShapeDtypeStruct((t, n), F32),
        compiler_params=_params("parallel", "arbitrary"),
        name="out_proj",
    )(x, w, h, mods)


def kernel(x, c, positions, ada_w, ada_b, norm1_w, ffn1_w_in, ffn1_w_out, norm2_w, mix_w_in, hgrn_lb, hgrn_norm_w, hgrn_w_out, diff_q_norm_w, diff_k_norm_w, diff_lq1, diff_lk1, diff_lq2, diff_lk2, diff_subln_w, diff_w_out, mix_w_o, norm3_w, ffn2_w_in, ffn2_w_out):
    bsz, seq, d = x.shape
    depth = ada_w.shape[0]
    t = bsz * seq
    kdim = hgrn_norm_w.shape[-1]
    fw = hgrn_lb.shape[-1]
    n_hgrn = fw // kdim
    dh = diff_q_norm_w.shape[-1]
    vdim = diff_subln_w.shape[-1]
    qk_w = diff_w_out.shape[1]
    n_diff = qk_w // vdim
    rope_dim = dh // 4
    half = rope_dim // 2
    assert kdim == LANES and vdim == LANES and 2 * dh == LANES

    inv_freq = ROPE_THETA ** (-jnp.arange(half, dtype=F32) / half)
    lane = jnp.arange(LANES)
    dpos = lane % dh
    frq = jnp.where(dpos < rope_dim, inv_freq[dpos % half], 0.0).reshape(1, LANES)
    msk = jnp.stack([(dpos < half), (dpos >= half) & (dpos < rope_dim)]).astype(F32)
    gm = (lane[:, None] // dh == lane[None, :] // dh).astype(BF16)
    pos2 = positions.reshape(t, 1)

    h = x.reshape(t, d)
    for l in range(depth):
        lam_init = 0.8 - 0.6 * math.exp(-0.3 * l)
        mods = _ada(c, ada_w[l], ada_b[l]).reshape(bsz * 9, 1, d)
        h = _ffn(h, norm1_w[l], mods, 0, 1, 2, ffn1_w_in[l].astype(BF16),
                 ffn1_w_out[l].astype(BF16), seq)
        proj = _proj(h, norm2_w[l], mods, 3, 4, mix_w_in[l].astype(BF16), seq)
        o_a = _hgrn(proj, hgrn_lb, hgrn_norm_w[l], l, bsz, seq, n_hgrn, 0)
        qw = jnp.tile(diff_q_norm_w[l], LANES // dh).reshape(1, LANES)
        kw = jnp.tile(diff_k_norm_w[l], LANES // dh).reshape(1, LANES)
        qr, kr, vr = _qkv(proj, pos2, frq, gm, qw, kw, msk, qk_w, 4 * fw, dh)
        lamv = jnp.stack([diff_lq1[l], diff_lk1[l], diff_lq2[l], diff_lk2[l]])
        o_b = _attn(qr, kr, vr, lamv, diff_subln_w[l], bsz, seq, n_diff, dh, lam_init)
        mix = _merge(o_a, o_b, hgrn_w_out[l].astype(BF16), diff_w_out[l].astype(BF16), proj,
                     4 * fw + 3 * qk_w, 4 * fw + 3 * qk_w + d)
        h = _oproj(mix, mix_w_o[l].astype(BF16), h, mods, 5, seq)
        h = _ffn(h, norm3_w[l], mods, 6, 7, 8, ffn2_w_in[l].astype(BF16),
                 ffn2_w_out[l].astype(BF16), seq)
    return h.reshape(bsz, seq, d)
```

```python
import functools
import math

import jax
import jax.numpy as jnp
from jax import lax
from jax.experimental import pallas as pl
from jax.experimental.pallas import tpu as pltpu

F32 = jnp.float32
BF16 = jnp.bfloat16

EPS = 1e-6
ROPE_THETA = 500000.0
LANES = 128
VMEM_LIMIT = 56 * 1024 * 1024

_NT = (((1,), (1,)), ((), ()))
_TN = (((0,), (0,)), ((), ()))


def _params(*sem):
    return pltpu.CompilerParams(dimension_semantics=sem, vmem_limit_bytes=VMEM_LIMIT)


def _dot(a, b):
    return jnp.dot(a, b, preferred_element_type=F32)


def _sigmoid(x):
    return 0.5 * jnp.tanh(0.5 * x) + 0.5


def _neg_abs(x):
    return -jnp.abs(x)


def _norm_mod(h, nw, sh, sc):
    var = jnp.mean(h * h, axis=-1, keepdims=True)
    return (h * lax.rsqrt(var + EPS) * nw) * (1.0 + sc) + sh


def _ada_kernel(c_ref, w_ref, b_ref, o_ref):
    c = c_ref[...]
    a = (c * _sigmoid(c)).astype(BF16)
    o_ref[...] = _dot(a, w_ref[...].astype(BF16)) + b_ref[...]


def _ada(c, w, b, tn=1024):
    bsz, d = c.shape
    n = w.shape[1]
    rows = 8
    cp = jnp.zeros((rows, d), F32).at[:bsz].set(c)
    out = pl.pallas_call(
        _ada_kernel,
        grid=(n // tn,),
        in_specs=[pl.BlockSpec((rows, d), lambda j: (0, 0)),
                  pl.BlockSpec((d, tn), lambda j: (0, j)),
                  pl.BlockSpec((1, tn), lambda j: (0, j))],
        out_specs=pl.BlockSpec((rows, tn), lambda j: (0, j)),
        out_shape=jax.ShapeDtypeStruct((rows, n), F32),
        compiler_params=_params("parallel"),
        name="adaln",
    )(cp, w, b.reshape(1, n))
    return out[:bsz]


def _ffn_kernel(h_ref, nw_ref, sh_ref, sc_ref, g_ref, wg_ref, wu_ref, wo_ref, o_ref, xn_ref):
    f = pl.program_id(1)

    @pl.when(f == 0)
    def _():
        xn = _norm_mod(h_ref[...], nw_ref[...], sh_ref[...], sc_ref[...])
        xn_ref[...] = xn.astype(BF16)
        o_ref[...] = jnp.zeros_like(o_ref)

    x = xn_ref[...]
    g = _dot(x, wg_ref[...])
    u = _dot(x, wu_ref[...])
    a = (g * _sigmoid(g)) * u
    o_ref[...] += _dot(a.astype(BF16), wo_ref[...])

    @pl.when(f == pl.num_programs(1) - 1)
    def _():
        o_ref[...] = h_ref[...] + (0.5 * g_ref[...]) * o_ref[...]


def _ffn(h, nw, mods, jsh, jsc, jg, w_in, w_out, seq, tm=1024, tf=512):
    t, d = h.shape
    dff = w_out.shape[0]
    nf = dff // tf
    per_b = seq // tm

    def mod_spec(j):
        return pl.BlockSpec((None, 1, d), lambda m, f: ((m // per_b) * 9 + j, 0, 0))

    return pl.pallas_call(
        _ffn_kernel,
        grid=(t // tm, nf),
        in_specs=[pl.BlockSpec((tm, d), lambda m, f: (m, 0), pipeline_mode=pl.Buffered(1)),
                  pl.BlockSpec((1, d), lambda m, f: (0, 0)),
                  mod_spec(jsh), mod_spec(jsc), mod_spec(jg),
                  pl.BlockSpec((d, tf), lambda m, f: (0, f)),
                  pl.BlockSpec((d, tf), lambda m, f: (0, f + nf)),
                  pl.BlockSpec((tf, d), lambda m, f: (f, 0))],
        out_specs=pl.BlockSpec((tm, d), lambda m, f: (m, 0)),
        out_shape=jax.ShapeDtypeStruct((t, d), F32),
        scratch_shapes=[pltpu.VMEM((tm, d), BF16)],
        compiler_params=_params("parallel", "arbitrary"),
        name="ffn",
    )(h, nw.reshape(1, d), mods, mods, mods, w_in, w_in, w_out)


def _proj_kernel(h_ref, nw_ref, sh_ref, sc_ref, w_ref, o_ref, xn_ref):
    @pl.when(pl.program_id(1) == 0)
    def _():
        xn = _norm_mod(h_ref[...], nw_ref[...], sh_ref[...], sc_ref[...])
        xn_ref[...] = xn.astype(BF16)

    o_ref[...] = _dot(xn_ref[...], w_ref[...]).astype(o_ref.dtype)


def _proj(h, nw, mods, jsh, jsc, w, seq, tm=1024, n_col_tiles=8):
    t, d = h.shape
    n = w.shape[1]
    tn = n // n_col_tiles
    assert tn * n_col_tiles == n and tn % LANES == 0
    per_b = seq // tm

    def mod_spec(j):
        return pl.BlockSpec((None, 1, d), lambda m, k: ((m // per_b) * 9 + j, 0, 0))

    return pl.pallas_call(
        _proj_kernel,
        grid=(t // tm, n // tn),
        in_specs=[pl.BlockSpec((tm, d), lambda m, k: (m, 0), pipeline_mode=pl.Buffered(1)),
                  pl.BlockSpec((1, d), lambda m, k: (0, 0)),
                  mod_spec(jsh), mod_spec(jsc),
                  pl.BlockSpec((d, tn), lambda m, k: (0, k))],
        out_specs=pl.BlockSpec((tm, tn), lambda m, k: (m, k)),
        out_shape=jax.ShapeDtypeStruct((t, n), BF16),
        scratch_shapes=[pltpu.VMEM((tm, d), BF16)],
        compiler_params=_params("parallel", "arbitrary"),
        name="mix_proj",
    )(h, nw.reshape(1, d), mods, mods, w)


def _qkv_kernel(q_ref, k_ref, v_ref, pos_ref, frq_ref, gm_ref, qw_ref, kw_ref, msk_ref,
                qo_ref, ko_ref, vo_ref, *, n_heads, dh, q_scale):
    pos = pos_ref[...].astype(F32)
    ang = pos * frq_ref[...]
    cosv = jnp.cos(ang)
    sinv = jnp.sin(ang)
    m_lo = msk_ref[0:1, :]
    m_hi = msk_ref[1:2, :]
    m_any = m_lo + m_hi
    c_full = m_any * cosv + (1.0 - m_any)
    s_lo = -(m_lo * sinv)
    s_hi = m_hi * sinv
    gm = gm_ref[...]

    def prep(x, w, scale):
        xsq = x * x
        hi = xsq.astype(BF16)
        lo = (xsq - hi.astype(F32)).astype(BF16)
        ss = _dot(hi, gm) + _dot(lo, gm)
        y = x * lax.rsqrt(ss * (1.0 / dh) + EPS) * w
        rot = y * c_full + pltpu.roll(y, LANES - 8, axis=1) * s_lo + pltpu.roll(y, 8, axis=1) * s_hi
        return (rot * scale).astype(BF16)

    qw = qw_ref[...]
    kw = kw_ref[...]
    for h in range(n_heads):
        sl = slice(h * LANES, (h + 1) * LANES)
        qo_ref[:, sl] = prep(q_ref[:, sl].astype(F32), qw, q_scale)
        ko_ref[:, sl] = prep(k_ref[:, sl].astype(F32), kw, 1.0)
        vo_ref[sl, :] = v_ref[:, sl].T


def _qkv(proj, positions, frq, gm, qw, kw, msk, width, col0, dh, tm=512):
    t = proj.shape[0]
    n_heads = width // LANES
    cb = col0 // width
    kern = functools.partial(_qkv_kernel, n_heads=n_heads, dh=dh,
                             q_scale=math.log2(math.e) / math.sqrt(dh))
    const = lambda shape: pl.BlockSpec(shape, lambda m: (0, 0))
    out = jax.ShapeDtypeStruct((t, width), BF16)
    return pl.pallas_call(
        kern,
        grid=(t // tm,),
        in_specs=[pl.BlockSpec((tm, width), lambda m: (m, cb)),
                  pl.BlockSpec((tm, width), lambda m: (m, cb + 1)),
                  pl.BlockSpec((tm, width), lambda m: (m, cb + 2)),
                  pl.BlockSpec((tm, 1), lambda m: (m, 0)),
                  const((1, LANES)), const((LANES, LANES)), const((1, LANES)), const((1, LANES)),
                  const((2, LANES))],
        out_specs=[pl.BlockSpec((tm, width), lambda m: (m, 0)),
                   pl.BlockSpec((tm, width), lambda m: (m, 0)),
                   pl.BlockSpec((width, tm), lambda m: (0, m))],
        out_shape=[out, out, jax.ShapeDtypeStruct((width, t), BF16)],
        compiler_params=_params("parallel"),
        name="qkv_prep",
    )(proj, proj, proj, positions, frq, gm, qw, kw, msk)


def _boundary_rows(b_ref, b, level, c):
    h = 1 << level
    if h == 1:
        odd = (lax.broadcasted_iota(jnp.int32, b.shape, 0) & 1) == 1
        return jnp.where(odd, pltpu.roll(b, 1, axis=0), b)
    if h == 2:
        sub = lax.broadcasted_iota(jnp.int32, (8, LANES), 0)
        pieces = []
        for j in range(c // 8):
            lo = jnp.broadcast_to(b_ref[8 * j + 1:8 * j + 2, :], (8, LANES))
            hi = jnp.broadcast_to(b_ref[8 * j + 5:8 * j + 6, :], (8, LANES))
            pieces.append(jnp.where(sub < 4, lo, hi))
        return jnp.concatenate(pieces, axis=0)
    blk = max(2 * h, 8)
    pieces = []
    for j in range(c // blk):
        r = (j * blk // (2 * h)) * 2 * h + h - 1
        pieces.append(jnp.broadcast_to(b_ref[r:r + 1, :], (blk, LANES)))
    return jnp.concatenate(pieces, axis=0)


def _hgrn_kernel(q_ref, f_ref, i_ref, g_ref, lb_ref, nw_ref, tri_ref, o_ref, st_ref, b_ref, *,
                 chunk, layer, heads):
    c = chunk
    rows = q_ref.shape[0]
    n_levels = c.bit_length() - 1
    hs = range(heads)

    @pl.when(pl.program_id(2) == 0)
    def _():
        st_ref[...] = jnp.zeros_like(st_ref)

    lbx = lb_ref[...]
    mx = jnp.max(lbx, axis=0, keepdims=True)
    ex = jnp.exp(lbx - mx)
    lb_all = (jnp.sum(ex[0:layer + 1, :], axis=0, keepdims=True)
              / jnp.sum(ex, axis=0, keepdims=True))
    nw = nw_ref[...]

    ti = lax.broadcasted_iota(jnp.int32, (c, c), 0)
    si = lax.broadcasted_iota(jnp.int32, (c, c), 1)
    lvl = jnp.where(ti > si, 31 - lax.clz(ti ^ si), jnp.where(ti == si, -1, -2))

    def body(ci, carry):
        r0 = pl.multiple_of(ci * c, c)
        tri = tri_ref[...]
        sl = [slice(h * LANES, (h + 1) * LANES) for h in hs]
        qb = [q_ref[pl.ds(r0, c), sl[h]] for h in hs]
        vb = [i_ref[pl.ds(r0, c), sl[h]] for h in hs]
        q = [qb[h].astype(F32) for h in hs]
        lf, k = [], []
        for h in hs:
            lb = lb_all[:, sl[h]]
            f = lb + (1.0 - lb) * _sigmoid(f_ref[pl.ds(r0, c), sl[h]].astype(F32))
            lf.append(jnp.log2(f))
            k.append(1.0 - f)
        parts = []
        for h in hs:
            p1 = lf[h].astype(BF16)
            parts += [p1, (lf[h] - p1.astype(F32)).astype(BF16)]
        cs = _dot(tri, jnp.concatenate(parts, axis=1))
        b = []
        for h in hs:
            o2 = 2 * h * LANES
            b.append(cs[:, o2:o2 + LANES] + cs[:, o2 + LANES:o2 + 2 * LANES])
            b_ref[h] = b[h]
        kb = [k[h].astype(BF16) for h in hs]
        inter = []
        for h in hs:
            b_last = b[h][c - 1:c, :]
            st = st_ref[h]
            inter.append(lax.dot_general((q[h] * jnp.exp2(b[h])).astype(BF16), st.astype(BF16),
                                         _NT, preferred_element_type=F32))
            kh = (k[h] * jnp.exp2(b_last - b[h])).astype(BF16)
            st_ref[h] = st * jnp.exp2(b_last) + lax.dot_general(vb[h], kh, _TN,
                                                                preferred_element_type=F32)
        scores = [jnp.where(lvl == -1, jnp.sum(q[h] * k[h], axis=-1, keepdims=True), 0.0)
                  for h in hs]
        for level in range(n_levels):
            for h in hs:
                m = _boundary_rows(b_ref.at[h], b[h], level, c)
                e = jnp.exp2(_neg_abs(b[h] - m)).astype(BF16)
                s_l = lax.dot_general(qb[h] * e, kb[h] * e, _NT, preferred_element_type=F32)
                scores[h] = jnp.where(lvl == level, s_l, scores[h])
        for h in hs:
            o = inter[h] + _dot(scores[h].astype(BF16), vb[h])
            var = jnp.mean(o * o, axis=-1, keepdims=True)
            on = o * lax.rsqrt(var + EPS) * nw
            g = g_ref[pl.ds(r0, c), sl[h]].astype(F32)
            o_ref[pl.ds(r0, c), sl[h]] = (on * (g * _sigmoid(g))).astype(o_ref.dtype)
        return carry

    lax.fori_loop(0, rows // c, body, 0)


def _hgrn(proj, lb, nw, layer, bsz, seq, n_heads, col0, rows=512, chunk=128, heads=4):
    t = proj.shape[0]
    kdim = LANES
    width = heads * kdim
    ns = seq // rows
    ng = n_heads // heads
    cb = col0 // width

    def col_spec(j):
        return pl.BlockSpec((rows, width), lambda b, h, s: (b * ns + s, cb + j * ng + h))

    return pl.pallas_call(
        functools.partial(_hgrn_kernel, chunk=chunk, layer=layer, heads=heads),
        grid=(bsz, ng, ns),
        in_specs=[col_spec(0), col_spec(1), col_spec(2), col_spec(3),
                  pl.BlockSpec((lb.shape[0], width), lambda b, h, s: (0, h)),
                  pl.BlockSpec((1, kdim), lambda b, h, s: (0, 0)),
                  pl.BlockSpec((chunk, chunk), lambda b, h, s: (0, 0))],
        out_specs=pl.BlockSpec((rows, width), lambda b, h, s: (b * ns + s, h)),
        out_shape=jax.ShapeDtypeStruct((t, n_heads * kdim), BF16),
        scratch_shapes=[pltpu.VMEM((heads, kdim, kdim), F32),
                        pltpu.VMEM((heads, chunk, kdim), F32)],
        compiler_params=_params("parallel", "parallel", "arbitrary"),
        name="hgrn2",
    )(proj, proj, proj, proj, lb, nw.reshape(1, kdim), jnp.tril(jnp.ones((chunk, chunk), BF16)))


def _attn_kernel(q_ref, k_ref, vt_ref, lam_ref, sw_ref, o_ref, qs_ref, m_ref, l_ref, acc_ref,
                 s_ref, *, tq, tk, dh, lam_init):
    qi = pl.program_id(2)
    q = q_ref[...]
    lane = lax.broadcasted_iota(jnp.int32, q.shape, 1)
    zero = jnp.zeros_like(q)
    qs_ref[0:tq, :] = jnp.where(lane < dh, q, zero)
    qs_ref[tq:2 * tq, :] = jnp.where(lane < dh, zero, q)
    m_ref[...] = jnp.full_like(m_ref, -jnp.inf)
    l_ref[...] = jnp.zeros_like(l_ref)
    acc_ref[...] = jnp.zeros_like(acc_ref)

    def scores(j, slot):
        kb = k_ref[pl.ds(pl.multiple_of(j * tk, tk), tk), :]
        for g in range(2):
            s_ref[slot, :, g * tq:(g + 1) * tq] = lax.dot_general(
                kb, qs_ref[g * tq:(g + 1) * tq, :], _NT, preferred_element_type=F32)

    def step(j, slot, masked):
        k0 = pl.multiple_of(j * tk, tk)
        vt = vt_ref[:, pl.ds(k0, tk)]
        for g in range(2):
            cols = slice(g * tq, (g + 1) * tq)
            st = s_ref[slot, :, cols]
            if masked:
                kpos = k0 + lax.broadcasted_iota(jnp.int32, st.shape, 0)
                qpos = qi * tq + lax.broadcasted_iota(jnp.int32, st.shape, 1)
                st = jnp.where(kpos <= qpos, st, -jnp.inf)
            m_old = m_ref[:, cols]
            m_new = jnp.maximum(m_old, jnp.max(st, axis=0, keepdims=True))
            alpha = jnp.exp2(m_old - m_new)
            pt = jnp.exp2(st - m_new)
            l_ref[:, cols] = alpha * l_ref[:, cols] + jnp.sum(pt, axis=0, keepdims=True)
            acc_ref[:, cols] = alpha * acc_ref[:, cols] + _dot(vt, pt.astype(BF16))
            m_ref[:, cols] = m_new

    n_full = (qi * tq) // tk
    odd = n_full % 2

    @pl.when(odd == 0)
    def _():
        scores(0, 0)

    @pl.when(odd == 1)
    def _():
        scores(0, 1)
        scores(1, 0)
        step(0, 1, False)

    def pair(i, carry):
        j = odd + 2 * i
        scores(j + 1, 1)
        step(j, 0, False)
        scores(j + 2, 0)
        step(j + 1, 1, False)
        return carry

    lax.fori_loop(0, n_full // 2, pair, 0)
    step(n_full, 0, True)

    lv = lam_ref[...]
    lam = (jnp.exp(jnp.sum(lv[0:1, :] * lv[1:2, :], axis=-1, keepdims=True))
           - jnp.exp(jnp.sum(lv[2:3, :] * lv[3:4, :], axis=-1, keepdims=True)) + lam_init)
    ot = acc_ref[...] / l_ref[...]
    o = (ot[:, 0:tq] - lam * ot[:, tq:2 * tq]).T
    var = jnp.mean(o * o, axis=-1, keepdims=True)
    o_ref[...] = (o * lax.rsqrt(var + EPS) * sw_ref[...] * (1.0 - lam_init)).astype(o_ref.dtype)


def _attn(q, k, vt, lamv, sw, bsz, seq, n_heads, dh, lam_init, tq=512, tk=512):
    t = q.shape[0]
    nq = seq // tq
    kern = functools.partial(_attn_kernel, tq=tq, tk=tk, dh=dh, lam_init=lam_init)
    return pl.pallas_call(
        kern,
        grid=(bsz, n_heads, nq),
        in_specs=[pl.BlockSpec((tq, LANES), lambda b, h, i: (b * nq + i, h)),
                  pl.BlockSpec((seq, LANES), lambda b, h, i: (b, h)),
                  pl.BlockSpec((LANES, seq), lambda b, h, i: (h, b)),
                  pl.BlockSpec(lamv.shape, lambda b, h, i: (0, 0)),
                  pl.BlockSpec((1, LANES), lambda b, h, i: (0, 0))],
        out_specs=pl.BlockSpec((tq, LANES), lambda b, h, i: (b * nq + i, h)),
        out_shape=jax.ShapeDtypeStruct((t, n_heads * LANES), BF16),
        scratch_shapes=[pltpu.VMEM((2 * tq, LANES), BF16),
                        pltpu.VMEM((1, 2 * tq), F32),
                        pltpu.VMEM((1, 2 * tq), F32),
                        pltpu.VMEM((LANES, 2 * tq), F32),
                        pltpu.VMEM((2, tk, 2 * tq), F32)],
        compiler_params=_params("parallel", "parallel", "arbitrary"),
        name="diff_attn",
    )(q, k, vt, lamv, sw.reshape(1, LANES))


def _merge_kernel(oa_ref, ob_ref, wa_ref, wb_ref, wo_ref, *rest, n_slabs):
    ga_refs = rest[:n_slabs]
    gb_refs = rest[n_slabs:2 * n_slabs]
    h_ref, g_ref, o_ref, mix_ref = rest[2 * n_slabs:]
    oa = oa_ref[...]
    ob = ob_ref[...]
    tn = ga_refs[0].shape[1]
    for j in range(n_slabs):
        cols = slice(j * tn, (j + 1) * tn)
        ya = _dot(oa, wa_ref[:, cols])
        yb = _dot(ob, wb_ref[:, cols])
        ga = ga_refs[j][...].astype(F32)
        gb = gb_refs[j][...].astype(F32)
        mix_ref[:, cols] = (_sigmoid(ga) * ya + _sigmoid(gb) * yb).astype(mix_ref.dtype)
    o_ref[...] = h_ref[...] + g_ref[...] * _dot(mix_ref[...], wo_ref[...])


def _merge(oa, ob, wa, wb, wo, proj, col_ga, col_gb, h, mods, jg, seq, tm=512, tn=1024):
    t, kdim = oa.shape
    d = wa.shape[1]
    n_slabs = d // tn
    per_b = seq // tm
    resident = lambda shape: pl.BlockSpec(shape, lambda m: (0, 0), pipeline_mode=pl.Buffered(1))
    gate = lambda col, j: pl.BlockSpec((tm, tn), lambda m: (m, col // tn + j))
    return pl.pallas_call(
        functools.partial(_merge_kernel, n_slabs=n_slabs),
        grid=(t // tm,),
        in_specs=([pl.BlockSpec((tm, kdim), lambda m: (m, 0)),
                   pl.BlockSpec((tm, kdim), lambda m: (m, 0)),
                   resident((kdim, d)), resident((kdim, d)), resident((d, d))]
                  + [gate(col_ga, j) for j in range(n_slabs)]
                  + [gate(col_gb, j) for j in range(n_slabs)]
                  + [pl.BlockSpec((tm, d), lambda m: (m, 0)),
                     pl.BlockSpec((None, 1, d), lambda m: ((m // per_b) * 9 + jg, 0, 0))]),
        out_specs=pl.BlockSpec((tm, d), lambda m: (m, 0)),
        out_shape=jax.ShapeDtypeStruct((t, d), F32),
        scratch_shapes=[pltpu.VMEM((tm, d), BF16)],
        compiler_params=_params("parallel"),
        name="merge_out_proj",
    )(oa, ob, wa, wb, wo, *([proj] * (2 * n_slabs)), h, mods)


def kernel(x, c, positions, ada_w, ada_b, norm1_w, ffn1_w_in, ffn1_w_out, norm2_w, mix_w_in, hgrn_lb, hgrn_norm_w, hgrn_w_out, diff_q_norm_w, diff_k_norm_w, diff_lq1, diff_lk1, diff_lq2, diff_lk2, diff_subln_w, diff_w_out, mix_w_o, norm3_w, ffn2_w_in, ffn2_w_out):
    bsz, seq, d = x.shape
    depth = ada_w.shape[0]
    t = bsz * seq
    kdim = hgrn_norm_w.shape[-1]
    fw = hgrn_lb.shape[-1]
    n_hgrn = fw // kdim
    dh = diff_q_norm_w.shape[-1]
    vdim = diff_subln_w.shape[-1]
    qk_w = diff_w_out.shape[1]
    n_diff = qk_w // vdim
    rope_dim = dh // 4
    half = rope_dim // 2
    assert kdim == LANES and vdim == LANES and 2 * dh == LANES

    inv_freq = ROPE_THETA ** (-jnp.arange(half, dtype=F32) / half)
    lane = jnp.arange(LANES)
    dpos = lane % dh
    frq = jnp.where(dpos < rope_dim, inv_freq[dpos % half], 0.0).reshape(1, LANES)
    msk = jnp.stack([(dpos < half), (dpos >= half) & (dpos < rope_dim)]).astype(F32)
    gm = (lane[:, None] // dh == lane[None, :] // dh).astype(BF16)
    pos2 = positions.reshape(t, 1)

    h = x.reshape(t, d)
    for l in range(depth):
        lam_init = 0.8 - 0.6 * math.exp(-0.3 * l)
        mods = _ada(c, ada_w[l], ada_b[l]).reshape(bsz * 9, 1, d)
        h = _ffn(h, norm1_w[l], mods, 0, 1, 2, ffn1_w_in[l].astype(BF16),
                 ffn1_w_out[l].astype(BF16), seq)
        proj = _proj(h, norm2_w[l], mods, 3, 4, mix_w_in[l].astype(BF16), seq)
        o_a = _hgrn(proj, hgrn_lb, hgrn_norm_w[l], l, bsz, seq, n_hgrn, 0)
        qw = jnp.tile(diff_q_norm_w[l], LANES // dh).reshape(1, LANES)
        kw = jnp.tile(diff_k_norm_w[l], LANES // dh).reshape(1, LANES)
        qr, kr, vr = _qkv(proj, pos2, frq, gm, qw, kw, msk, qk_w, 4 * fw, dh)
        lamv = jnp.stack([diff_lq1[l], diff_lk1[l], diff_lq2[l], diff_lk2[l]])
        o_b = _attn(qr, kr, vr, lamv, diff_subln_w[l], bsz, seq, n_diff, dh, lam_init)
        h = _merge(o_a, o_b, hgrn_w_out[l].astype(BF16), diff_w_out[l].astype(BF16),
                   mix_w_o[l].astype(BF16), proj, 4 * fw + 3 * qk_w, 4 * fw + 3 * qk_w + d,
                   h, mods, 5, seq)
        h = _ffn(h, norm3_w[l], mods, 6, 7, 8, ffn2_w_in[l].astype(BF16),
                 ffn2_w_out[l].astype(BF16), seq)
    return h.reshape(bsz, seq, d)
```

```python
import functools
import math

import jax
import jax.numpy as jnp
from jax import lax
from jax.experimental import pallas as pl
from jax.experimental.pallas import tpu as pltpu

F32 = jnp.float32
BF16 = jnp.bfloat16

EPS = 1e-6
ROPE_THETA = 500000.0
LANES = 128
VMEM_LIMIT = 56 * 1024 * 1024

_NT = (((1,), (1,)), ((), ()))
_TN = (((0,), (0,)), ((), ()))


def _params(*sem):
    return pltpu.CompilerParams(dimension_semantics=sem, vmem_limit_bytes=VMEM_LIMIT)


def _dot(a, b):
    return jnp.dot(a, b, preferred_element_type=F32)


def _sigmoid(x):
    return 0.5 * jnp.tanh(0.5 * x) + 0.5


def _neg_abs(x):
    return -jnp.abs(x)


def _norm_mod(h, nw, sh, sc):
    var = jnp.mean(h * h, axis=-1, keepdims=True)
    return (h * lax.rsqrt(var + EPS) * nw) * (1.0 + sc) + sh


def _ada_kernel(c_ref, w_ref, b_ref, o_ref):
    c = c_ref[...]
    a = (c * _sigmoid(c)).astype(BF16)
    o_ref[...] = _dot(a, w_ref[...].astype(BF16)) + b_ref[...]


def _ada(c, w, b, tn=1024):
    bsz, d = c.shape
    n = w.shape[1]
    rows = 8
    cp = jnp.zeros((rows, d), F32).at[:bsz].set(c)
    out = pl.pallas_call(
        _ada_kernel,
        grid=(n // tn,),
        in_specs=[pl.BlockSpec((rows, d), lambda j: (0, 0)),
                  pl.BlockSpec((d, tn), lambda j: (0, j)),
                  pl.BlockSpec((1, tn), lambda j: (0, j))],
        out_specs=pl.BlockSpec((rows, tn), lambda j: (0, j)),
        out_shape=jax.ShapeDtypeStruct((rows, n), F32),
        compiler_params=_params("parallel"),
        name="adaln",
    )(cp, w, b.reshape(1, n))
    return out[:bsz]


def _ffn_kernel(h_ref, nw_ref, sh_ref, sc_ref, g_ref, wg_ref, wu_ref, wo_ref, o_ref,
                xn_ref, acc_ref):
    f = pl.program_id(1)

    @pl.when(f == 0)
    def _():
        xn = _norm_mod(h_ref[...], nw_ref[...], sh_ref[...], sc_ref[...])
        xn_ref[...] = xn.astype(BF16)
        acc_ref[...] = jnp.zeros_like(acc_ref)

    x = xn_ref[...]
    g = _dot(x, wg_ref[...])
    u = _dot(x, wu_ref[...])
    a = (g * _sigmoid(g)) * u
    acc_ref[...] += _dot(a.astype(BF16), wo_ref[...])

    @pl.when(f == pl.num_programs(1) - 1)
    def _():
        o_ref[...] = h_ref[...] + (0.5 * g_ref[...]) * acc_ref[...]


def _ffn(h, nw, mods, jsh, jsc, jg, w_in, w_out, seq, tm=512, tf=512):
    t, d = h.shape
    dff = w_out.shape[0]
    nf = dff // tf
    per_b = seq // tm

    def mod_spec(j):
        return pl.BlockSpec((None, 1, d), lambda m, f: ((m // per_b) * 9 + j, 0, 0))

    return pl.pallas_call(
        _ffn_kernel,
        grid=(t // tm, nf),
        in_specs=[pl.BlockSpec((tm, d), lambda m, f: (m, 0)),
                  pl.BlockSpec((1, d), lambda m, f: (0, 0)),
                  mod_spec(jsh), mod_spec(jsc), mod_spec(jg),
                  pl.BlockSpec((d, tf), lambda m, f: (0, f)),
                  pl.BlockSpec((d, tf), lambda m, f: (0, f + nf)),
                  pl.BlockSpec((tf, d), lambda m, f: (f, 0))],
        out_specs=pl.BlockSpec((tm, d), lambda m, f: (m, 0)),
        out_shape=jax.ShapeDtypeStruct((t, d), F32),
        scratch_shapes=[pltpu.VMEM((tm, d), BF16), pltpu.VMEM((tm, d), F32)],
        compiler_params=_params("parallel", "arbitrary"),
        name="ffn",
    )(h, nw.reshape(1, d), mods, mods, mods, w_in, w_in, w_out)


def _proj_kernel(h_ref, nw_ref, sh_ref, sc_ref, w_ref, o_ref, xn_ref):
    @pl.when(pl.program_id(1) == 0)
    def _():
        xn = _norm_mod(h_ref[...], nw_ref[...], sh_ref[...], sc_ref[...])
        xn_ref[...] = xn.astype(BF16)

    o_ref[...] = _dot(xn_ref[...], w_ref[...]).astype(o_ref.dtype)


def _proj(h, nw, mods, jsh, jsc, w, seq, tm=1024, tn=1024):
    t, d = h.shape
    n = w.shape[1]
    per_b = seq // tm

    def mod_spec(j):
        return pl.BlockSpec((None, 1, d), lambda m, k: ((m // per_b) * 9 + j, 0, 0))

    return pl.pallas_call(
        _proj_kernel,
        grid=(t // tm, n // tn),
        in_specs=[pl.BlockSpec((tm, d), lambda m, k: (m, 0)),
                  pl.BlockSpec((1, d), lambda m, k: (0, 0)),
                  mod_spec(jsh), mod_spec(jsc),
                  pl.BlockSpec((d, tn), lambda m, k: (0, k))],
        out_specs=pl.BlockSpec((tm, tn), lambda m, k: (m, k)),
        out_shape=jax.ShapeDtypeStruct((t, n), BF16),
        scratch_shapes=[pltpu.VMEM((tm, d), BF16)],
        compiler_params=_params("parallel", "arbitrary"),
        name="mix_proj",
    )(h, nw.reshape(1, d), mods, mods, w)


def _qkv_kernel(q_ref, k_ref, v_ref, pos_ref, frq_ref, gm_ref, qw_ref, kw_ref, msk_ref,
                qo_ref, ko_ref, vo_ref, *, n_heads, dh, q_scale):
    pos = pos_ref[...].astype(F32)
    ang = pos * frq_ref[...]
    cosv = jnp.cos(ang)
    sinv = jnp.sin(ang)
    m_lo = msk_ref[0:1, :]
    m_hi = msk_ref[1:2, :]
    m_any = m_lo + m_hi
    c_full = m_any * cosv + (1.0 - m_any)
    s_lo = -(m_lo * sinv)
    s_hi = m_hi * sinv
    gm = gm_ref[...]

    def prep(x, w, scale):
        xsq = x * x
        hi = xsq.astype(BF16)
        lo = (xsq - hi.astype(F32)).astype(BF16)
        ss = _dot(hi, gm) + _dot(lo, gm)
        y = x * lax.rsqrt(ss * (1.0 / dh) + EPS) * w
        rot = y * c_full + pltpu.roll(y, LANES - 8, axis=1) * s_lo + pltpu.roll(y, 8, axis=1) * s_hi
        return (rot * scale).astype(BF16)

    qw = qw_ref[...]
    kw = kw_ref[...]
    for h in range(n_heads):
        sl = slice(h * LANES, (h + 1) * LANES)
        qo_ref[:, sl] = prep(q_ref[:, sl].astype(F32), qw, q_scale)
        ko_ref[:, sl] = prep(k_ref[:, sl].astype(F32), kw, 1.0)
        vo_ref[sl, :] = v_ref[:, sl].T


def _qkv(proj, positions, frq, gm, qw, kw, msk, width, col0, dh, tm=512):
    t = proj.shape[0]
    n_heads = width // LANES
    cb = col0 // width
    kern = functools.partial(_qkv_kernel, n_heads=n_heads, dh=dh,
                             q_scale=math.log2(math.e) / math.sqrt(dh))
    const = lambda shape: pl.BlockSpec(shape, lambda m: (0, 0))
    out = jax.ShapeDtypeStruct((t, width), BF16)
    return pl.pallas_call(
        kern,
        grid=(t // tm,),
        in_specs=[pl.BlockSpec((tm, width), lambda m: (m, cb)),
                  pl.BlockSpec((tm, width), lambda m: (m, cb + 1)),
                  pl.BlockSpec((tm, width), lambda m: (m, cb + 2)),
                  pl.BlockSpec((tm, 1), lambda m: (m, 0)),
                  const((1, LANES)), const((LANES, LANES)), const((1, LANES)), const((1, LANES)),
                  const((2, LANES))],
        out_specs=[pl.BlockSpec((tm, width), lambda m: (m, 0)),
                   pl.BlockSpec((tm, width), lambda m: (m, 0)),
                   pl.BlockSpec((width, tm), lambda m: (0, m))],
        out_shape=[out, out, jax.ShapeDtypeStruct((width, t), BF16)],
        compiler_params=_params("parallel"),
        name="qkv_prep",
    )(proj, proj, proj, positions, frq, gm, qw, kw, msk)


def _boundary_rows(b_ref, b, level, c):
    h = 1 << level
    if h == 1:
        odd = (lax.broadcasted_iota(jnp.int32, b.shape, 0) & 1) == 1
        return jnp.where(odd, pltpu.roll(b, 1, axis=0), b)
    if h == 2:
        sub = lax.broadcasted_iota(jnp.int32, (8, LANES), 0)
        pieces = []
        for j in range(c // 8):
            lo = jnp.broadcast_to(b_ref[8 * j + 1:8 * j + 2, :], (8, LANES))
            hi = jnp.broadcast_to(b_ref[8 * j + 5:8 * j + 6, :], (8, LANES))
            pieces.append(jnp.where(sub < 4, lo, hi))
        return jnp.concatenate(pieces, axis=0)
    blk = max(2 * h, 8)
    pieces = []
    for j in range(c // blk):
        r = (j * blk // (2 * h)) * 2 * h + h - 1
        pieces.append(jnp.broadcast_to(b_ref[r:r + 1, :], (blk, LANES)))
    return jnp.concatenate(pieces, axis=0)


def _hgrn_kernel(q_ref, f_ref, i_ref, g_ref, lb_ref, nw_ref, tri_ref, o_ref, st_ref, b_ref, *,
                 chunk, layer, heads):
    c = chunk
    rows = q_ref.shape[0]
    n_levels = c.bit_length() - 1
    hs = range(heads)

    @pl.when(pl.program_id(2) == 0)
    def _():
        st_ref[...] = jnp.zeros_like(st_ref)

    lbx = lb_ref[...]
    mx = jnp.max(lbx, axis=0, keepdims=True)
    ex = jnp.exp(lbx - mx)
    lb_all = (jnp.sum(ex[0:layer + 1, :], axis=0, keepdims=True)
              / jnp.sum(ex, axis=0, keepdims=True))
    nw = nw_ref[...]

    ti = lax.broadcasted_iota(jnp.int32, (c, c), 0)
    si = lax.broadcasted_iota(jnp.int32, (c, c), 1)
    lvl = jnp.where(ti > si, 31 - lax.clz(ti ^ si), jnp.where(ti == si, -1, -2))

    def body(ci, carry):
        r0 = pl.multiple_of(ci * c, c)
        tri = tri_ref[...]
        sl = [slice(h * LANES, (h + 1) * LANES) for h in hs]
        qb = [q_ref[pl.ds(r0, c), sl[h]] for h in hs]
        vb = [i_ref[pl.ds(r0, c), sl[h]] for h in hs]
        q = [qb[h].astype(F32) for h in hs]
        lf, k = [], []
        for h in hs:
            lb = lb_all[:, sl[h]]
            f = lb + (1.0 - lb) * _sigmoid(f_ref[pl.ds(r0, c), sl[h]].astype(F32))
            lf.append(jnp.log2(f))
            k.append(1.0 - f)
        parts = []
        for h in hs:
            p1 = lf[h].astype(BF16)
            parts += [p1, (lf[h] - p1.astype(F32)).astype(BF16)]
        cs = _dot(tri, jnp.concatenate(parts, axis=1))
        b = []
        for h in hs:
            o2 = 2 * h * LANES
            b.append(cs[:, o2:o2 + LANES] + cs[:, o2 + LANES:o2 + 2 * LANES])
            b_ref[h] = b[h]
        kb = [k[h].astype(BF16) for h in hs]
        inter = []
        for h in hs:
            b_last = b[h][c - 1:c, :]
            st = st_ref[h]
            inter.append(lax.dot_general((q[h] * jnp.exp2(b[h])).astype(BF16), st.astype(BF16),
                                         _NT, preferred_element_type=F32))
            kh = (k[h] * jnp.exp2(b_last - b[h])).astype(BF16)
            st_ref[h] = st * jnp.exp2(b_last) + lax.dot_general(vb[h], kh, _TN,
                                                                preferred_element_type=F32)
        scores = [jnp.where(lvl == -1, jnp.sum(q[h] * k[h], axis=-1, keepdims=True), 0.0)
                  for h in hs]
        for level in range(n_levels):
            for h in hs:
                m = _boundary_rows(b_ref.at[h], b[h], level, c)
                e = jnp.exp2(_neg_abs(b[h] - m)).astype(BF16)
                s_l = lax.dot_general(qb[h] * e, kb[h] * e, _NT, preferred_element_type=F32)
                scores[h] = jnp.where(lvl == level, s_l, scores[h])
        for h in hs:
            o = inter[h] + _dot(scores[h].astype(BF16), vb[h])
            var = jnp.mean(o * o, axis=-1, keepdims=True)
            on = o * lax.rsqrt(var + EPS) * nw
            g = g_ref[pl.ds(r0, c), sl[h]].astype(F32)
            o_ref[pl.ds(r0, c), sl[h]] = (on * (g * _sigmoid(g))).astype(o_ref.dtype)
        return carry

    lax.fori_loop(0, rows // c, body, 0)


def _hgrn(proj, lb, nw, layer, bsz, seq, n_heads, col0, rows=512, chunk=128, heads=4):
    t = proj.shape[0]
    kdim = LANES
    width = heads * kdim
    ns = seq // rows
    ng = n_heads // heads
    cb = col0 // width

    def col_spec(j):
        return pl.BlockSpec((rows, width), lambda b, h, s: (b * ns + s, cb + j * ng + h))

    return pl.pallas_call(
        functools.partial(_hgrn_kernel, chunk=chunk, layer=layer, heads=heads),
        grid=(bsz, ng, ns),
        in_specs=[col_spec(0), col_spec(1), col_spec(2), col_spec(3),
                  pl.BlockSpec((lb.shape[0], width), lambda b, h, s: (0, h)),
                  pl.BlockSpec((1, kdim), lambda b, h, s: (0, 0)),
                  pl.BlockSpec((chunk, chunk), lambda b, h, s: (0, 0))],
        out_specs=pl.BlockSpec((rows, width), lambda b, h, s: (b * ns + s, h)),
        out_shape=jax.ShapeDtypeStruct((t, n_heads * kdim), BF16),
        scratch_shapes=[pltpu.VMEM((heads, kdim, kdim), F32),
                        pltpu.VMEM((heads, chunk, kdim), F32)],
        compiler_params=_params("parallel", "parallel", "arbitrary"),
        name="hgrn2",
    )(proj, proj, proj, proj, lb, nw.reshape(1, kdim), jnp.tril(jnp.ones((chunk, chunk), BF16)))


def _attn_kernel(q_ref, k_ref, vt_ref, lam_ref, sw_ref, bias_ref, o_ref, qs_ref, m_ref, l_ref,
                 acc_ref, s_ref, mx_ref, *, tq, tk, dh, lam_init):
    qi = pl.program_id(2)
    q = q_ref[...]
    lane = lax.broadcasted_iota(jnp.int32, q.shape, 1)
    zero = jnp.zeros_like(q)
    qs_ref[0:tq, :] = jnp.where(lane < dh, q, zero)
    qs_ref[tq:2 * tq, :] = jnp.where(lane < dh, zero, q)
    m_ref[...] = jnp.full_like(m_ref, -jnp.inf)
    l_ref[...] = jnp.zeros_like(l_ref)
    acc_ref[...] = jnp.zeros_like(acc_ref)

    def scores(j, slot):
        kb = k_ref[pl.ds(pl.multiple_of(j * tk, tk), tk), :]
        for g in range(2):
            cols = slice(g * tq, (g + 1) * tq)
            st = lax.dot_general(kb, qs_ref[cols, :], _NT, preferred_element_type=F32)
            s_ref[slot, :, cols] = st
            mx_ref[slot, :, cols] = jnp.max(st, axis=0, keepdims=True)

    def step(j, slot, masked):
        k0 = pl.multiple_of(j * tk, tk)
        vt = vt_ref[:, pl.ds(k0, tk)]
        for g in range(2):
            cols = slice(g * tq, (g + 1) * tq)
            st = s_ref[slot, :, cols]
            if masked:
                st = st + bias_ref[...]
                m_cur = jnp.max(st, axis=0, keepdims=True)
            else:
                m_cur = mx_ref[slot, :, cols]
            m_old = m_ref[:, cols]
            m_new = jnp.maximum(m_old, m_cur)
            alpha = jnp.exp2(m_old - m_new)
            pt = jnp.exp2(st - m_new)
            l_ref[:, cols] = alpha * l_ref[:, cols] + jnp.sum(pt, axis=0, keepdims=True)
            acc_ref[:, cols] = alpha * acc_ref[:, cols] + _dot(vt, pt.astype(BF16))
            m_ref[:, cols] = m_new

    n_full = (qi * tq) // tk
    odd = n_full % 2

    @pl.when(odd == 0)
    def _():
        scores(0, 0)

    @pl.when(odd == 1)
    def _():
        scores(0, 1)
        scores(1, 0)
        step(0, 1, False)

    def pair(i, carry):
        j = odd + 2 * i
        scores(j + 1, 1)
        step(j, 0, False)
        scores(j + 2, 0)
        step(j + 1, 1, False)
        return carry

    lax.fori_loop(0, n_full // 2, pair, 0)
    step(n_full, 0, True)

    lv = lam_ref[...]
    lam = (jnp.exp(jnp.sum(lv[0:1, :] * lv[1:2, :], axis=-1, keepdims=True))
           - jnp.exp(jnp.sum(lv[2:3, :] * lv[3:4, :], axis=-1, keepdims=True)) + lam_init)
    ot = acc_ref[...] / l_ref[...]
    o = (ot[:, 0:tq] - lam * ot[:, tq:2 * tq]).T
    var = jnp.mean(o * o, axis=-1, keepdims=True)
    o_ref[...] = (o * lax.rsqrt(var + EPS) * sw_ref[...] * (1.0 - lam_init)).astype(o_ref.dtype)


def _attn(q, k, vt, lamv, sw, bsz, seq, n_heads, dh, lam_init, tq=512, tk=512):
    t = q.shape[0]
    nq = seq // tq
    assert tq == tk
    kern = functools.partial(_attn_kernel, tq=tq, tk=tk, dh=dh, lam_init=lam_init)
    key = jnp.arange(tk)[:, None]
    qry = jnp.arange(tq)[None, :]
    bias = jnp.where(key <= qry, 0.0, -jnp.inf).astype(F32)
    return pl.pallas_call(
        kern,
        grid=(bsz, n_heads, nq),
        in_specs=[pl.BlockSpec((tq, LANES), lambda b, h, i: (b * nq + i, h)),
                  pl.BlockSpec((seq, LANES), lambda b, h, i: (b, h)),
                  pl.BlockSpec((LANES, seq), lambda b, h, i: (h, b)),
                  pl.BlockSpec(lamv.shape, lambda b, h, i: (0, 0)),
                  pl.BlockSpec((1, LANES), lambda b, h, i: (0, 0)),
                  pl.BlockSpec((tk, tq), lambda b, h, i: (0, 0))],
        out_specs=pl.BlockSpec((tq, LANES), lambda b, h, i: (b * nq + i, h)),
        out_shape=jax.ShapeDtypeStruct((t, n_heads * LANES), BF16),
        scratch_shapes=[pltpu.VMEM((2 * tq, LANES), BF16),
                        pltpu.VMEM((1, 2 * tq), F32),
                        pltpu.VMEM((1, 2 * tq), F32),
                        pltpu.VMEM((LANES, 2 * tq), F32),
                        pltpu.VMEM((2, tk, 2 * tq), F32),
                        pltpu.VMEM((2, 1, 2 * tq), F32)],
        compiler_params=_params("parallel", "parallel", "arbitrary"),
        name="diff_attn",
    )(q, k, vt, lamv, sw.reshape(1, LANES), bias)


def _merge_kernel(oa_ref, ob_ref, wa_ref, wb_ref, wo_ref, *rest, n_slabs):
    ga_refs = rest[:n_slabs]
    gb_refs = rest[n_slabs:2 * n_slabs]
    h_ref, g_ref, o_ref, mix_ref = rest[2 * n_slabs:]
    oa = oa_ref[...]
    ob = ob_ref[...]
    tn = ga_refs[0].shape[1]
    for j in range(n_slabs):
        cols = slice(j * tn, (j + 1) * tn)
        ya = _dot(oa, wa_ref[:, cols])
        yb = _dot(ob, wb_ref[:, cols])
        ga = ga_refs[j][...].astype(F32)
        gb = gb_refs[j][...].astype(F32)
        mix_ref[:, cols] = (_sigmoid(ga) * ya + _sigmoid(gb) * yb).astype(mix_ref.dtype)
    o_ref[...] = h_ref[...] + g_ref[...] * _dot(mix_ref[...], wo_ref[...])


def _merge(oa, ob, wa, wb, wo, proj, col_ga, col_gb, h, mods, jg, seq, tm=512, tn=1024):
    t, kdim = oa.shape
    d = wa.shape[1]
    n_slabs = d // tn
    per_b = seq // tm
    resident = lambda shape: pl.BlockSpec(shape, lambda m: (0, 0), pipeline_mode=pl.Buffered(1))
    gate = lambda col, j: pl.BlockSpec((tm, tn), lambda m: (m, col // tn + j))
    return pl.pallas_call(
        functools.partial(_merge_kernel, n_slabs=n_slabs),
        grid=(t // tm,),
        in_specs=([pl.BlockSpec((tm, kdim), lambda m: (m, 0)),
                   pl.BlockSpec((tm, kdim), lambda m: (m, 0)),
                   resident((kdim, d)), resident((kdim, d)), resident((d, d))]
                  + [gate(col_ga, j) for j in range(n_slabs)]
                  + [gate(col_gb, j) for j in range(n_slabs)]
                  + [pl.BlockSpec((tm, d), lambda m: (m, 0)),
                     pl.BlockSpec((None, 1, d), lambda m: ((m // per_b) * 9 + jg, 0, 0))]),
        out_specs=pl.BlockSpec((tm, d), lambda m: (m, 0)),
        out_shape=jax.ShapeDtypeStruct((t, d), F32),
        scratch_shapes=[pltpu.VMEM((tm, d), BF16)],
        compiler_params=_params("parallel"),
        name="merge_out_proj",
    )(oa, ob, wa, wb, wo, *([proj] * (2 * n_slabs)), h, mods)


def kernel(x, c, positions, ada_w, ada_b, norm1_w, ffn1_w_in, ffn1_w_out, norm2_w, mix_w_in, hgrn_lb, hgrn_norm_w, hgrn_w_out, diff_q_norm_w, diff_k_norm_w, diff_lq1, diff_lk1, diff_lq2, diff_lk2, diff_subln_w, diff_w_out, mix_w_o, norm3_w, ffn2_w_in, ffn2_w_out):
    bsz, seq, d = x.shape
    depth = ada_w.shape[0]
    t = bsz * seq
    kdim = hgrn_norm_w.shape[-1]
    fw = hgrn_lb.shape[-1]
    n_hgrn = fw // kdim
    dh = diff_q_norm_w.shape[-1]
    vdim = diff_subln_w.shape[-1]
    qk_w = diff_w_out.shape[1]
    n_diff = qk_w // vdim
    rope_dim = dh // 4
    half = rope_dim // 2
    assert kdim == LANES and vdim == LANES and 2 * dh == LANES

    inv_freq = ROPE_THETA ** (-jnp.arange(half, dtype=F32) / half)
    lane = jnp.arange(LANES)
    dpos = lane % dh
    frq = jnp.where(dpos < rope_dim, inv_freq[dpos % half], 0.0).reshape(1, LANES)
    msk = jnp.stack([(dpos < half), (dpos >= half) & (dpos < rope_dim)]).astype(F32)
    gm = (lane[:, None] // dh == lane[None, :] // dh).astype(BF16)
    pos2 = positions.reshape(t, 1)

    h = x.reshape(t, d)
    for l in range(depth):
        lam_init = 0.8 - 0.6 * math.exp(-0.3 * l)
        mods = _ada(c, ada_w[l], ada_b[l]).reshape(bsz * 9, 1, d)
        h = _ffn(h, norm1_w[l], mods, 0, 1, 2, ffn1_w_in[l].astype(BF16),
                 ffn1_w_out[l].astype(BF16), seq)
        proj = _proj(h, norm2_w[l], mods, 3, 4, mix_w_in[l].astype(BF16), seq)
        o_a = _hgrn(proj, hgrn_lb, hgrn_norm_w[l], l, bsz, seq, n_hgrn, 0)
        qw = jnp.tile(diff_q_norm_w[l], LANES // dh).reshape(1, LANES)
        kw = jnp.tile(diff_k_norm_w[l], LANES // dh).reshape(1, LANES)
        qr, kr, vr = _qkv(proj, pos2, frq, gm, qw, kw, msk, qk_w, 4 * fw, dh)
        lamv = jnp.stack([diff_lq1[l], diff_lk1[l], diff_lq2[l], diff_lk2[l]])
        o_b = _attn(qr, kr, vr, lamv, diff_subln_w[l], bsz, seq, n_diff, dh, lam_init)
        h = _merge(o_a, o_b, hgrn_w_out[l].astype(BF16), diff_w_out[l].astype(BF16),
                   mix_w_o[l].astype(BF16), proj, 4 * fw + 3 * qk_w, 4 * fw + 3 * qk_w + d,
                   h, mods, 5, seq)
        h = _ffn(h, norm3_w[l], mods, 6, 7, 8, ffn2_w_in[l].astype(BF16),
                 ffn2_w_out[l].astype(BF16), seq)
    return h.reshape(bsz, seq, d)
```

```python
import functools
import math

import jax
import jax.numpy as jnp
from jax import lax
from jax.experimental import pallas as pl
from jax.experimental.pallas import tpu as pltpu

F32 = jnp.float32
BF16 = jnp.bfloat16

EPS = 1e-6
ROPE_THETA = 500000.0
LANES = 128
VMEM_LIMIT = 56 * 1024 * 1024

_NT = (((1,), (1,)), ((), ()))
_TN = (((0,), (0,)), ((), ()))


def _params(*sem):
    return pltpu.CompilerParams(dimension_semantics=sem, vmem_limit_bytes=VMEM_LIMIT)


def _dot(a, b):
    return jnp.dot(a, b, preferred_element_type=F32)


def _sigmoid(x):
    return 0.5 * jnp.tanh(0.5 * x) + 0.5


def _neg_abs(x):
    return -jnp.abs(x)


def _norm_mod(h, nw, sh, sc):
    var = jnp.mean(h * h, axis=-1, keepdims=True)
    return (h * lax.rsqrt(var + EPS) * nw) * (1.0 + sc) + sh


NORM_ROWS = 16


def _norm_mod_to(h_ref, nw_ref, sh_ref, sc_ref, xn_ref):
    nw = nw_ref[...]
    sh = sh_ref[...]
    sc = sc_ref[...]

    def body(i, carry):
        r = pl.multiple_of(i * NORM_ROWS, NORM_ROWS)
        xn = _norm_mod(h_ref[pl.ds(r, NORM_ROWS), :], nw, sh, sc)
        xn_ref[pl.ds(r, NORM_ROWS), :] = xn.astype(xn_ref.dtype)
        return carry

    lax.fori_loop(0, h_ref.shape[0] // NORM_ROWS, body, 0, unroll=8)


def _ada_kernel(c_ref, w_ref, b_ref, o_ref):
    c = c_ref[...]
    a = (c * _sigmoid(c)).astype(BF16)
    o_ref[...] = _dot(a, w_ref[...].astype(BF16)) + b_ref[...]


def _ada(c, w, b, tn=1024):
    bsz, d = c.shape
    n = w.shape[1]
    rows = 8
    cp = jnp.zeros((rows, d), F32).at[:bsz].set(c)
    out = pl.pallas_call(
        _ada_kernel,
        grid=(n // tn,),
        in_specs=[pl.BlockSpec((rows, d), lambda j: (0, 0)),
                  pl.BlockSpec((d, tn), lambda j: (0, j)),
                  pl.BlockSpec((1, tn), lambda j: (0, j))],
        out_specs=pl.BlockSpec((rows, tn), lambda j: (0, j)),
        out_shape=jax.ShapeDtypeStruct((rows, n), F32),
        compiler_params=_params("parallel"),
        name="adaln",
    )(cp, w, b.reshape(1, n))
    return out[:bsz]


def _ffn_kernel(h_ref, nw_ref, sh_ref, sc_ref, g_ref, wg_ref, wu_ref, wo_ref, o_ref,
                xn_ref, acc_ref):
    f = pl.program_id(1)

    @pl.when(f == 0)
    def _():
        _norm_mod_to(h_ref, nw_ref, sh_ref, sc_ref, xn_ref)
        acc_ref[...] = jnp.zeros_like(acc_ref)

    x = xn_ref[...]
    g = _dot(x, wg_ref[...])
    u = _dot(x, wu_ref[...])
    a = (g * _sigmoid(g)) * u
    acc_ref[...] += _dot(a.astype(BF16), wo_ref[...])

    @pl.when(f == pl.num_programs(1) - 1)
    def _():
        o_ref[...] = h_ref[...] + (0.5 * g_ref[...]) * acc_ref[...]


def _ffn(h, nw, mods, jsh, jsc, jg, w_in, w_out, seq, tm=512, tf=512):
    t, d = h.shape
    dff = w_out.shape[0]
    nf = dff // tf
    per_b = seq // tm

    def mod_spec(j):
        return pl.BlockSpec((None, 1, d), lambda m, f: ((m // per_b) * 9 + j, 0, 0))

    return pl.pallas_call(
        _ffn_kernel,
        grid=(t // tm, nf),
        in_specs=[pl.BlockSpec((tm, d), lambda m, f: (m, 0)),
                  pl.BlockSpec((1, d), lambda m, f: (0, 0)),
                  mod_spec(jsh), mod_spec(jsc), mod_spec(jg),
                  pl.BlockSpec((d, tf), lambda m, f: (0, f)),
                  pl.BlockSpec((d, tf), lambda m, f: (0, f + nf)),
                  pl.BlockSpec((tf, d), lambda m, f: (f, 0))],
        out_specs=pl.BlockSpec((tm, d), lambda m, f: (m, 0)),
        out_shape=jax.ShapeDtypeStruct((t, d), F32),
        scratch_shapes=[pltpu.VMEM((tm, d), BF16), pltpu.VMEM((tm, d), F32)],
        compiler_params=_params("parallel", "arbitrary"),
        name="ffn",
    )(h, nw.reshape(1, d), mods, mods, mods, w_in, w_in, w_out)


def _proj_kernel(h_ref, nw_ref, sh_ref, sc_ref, w_ref, o_ref, xn_ref):
    @pl.when(pl.program_id(1) == 0)
    def _():
        _norm_mod_to(h_ref, nw_ref, sh_ref, sc_ref, xn_ref)

    o_ref[...] = _dot(xn_ref[...], w_ref[...]).astype(o_ref.dtype)


def _proj(h, nw, mods, jsh, jsc, w, seq, tm=1024, tn=1024):
    t, d = h.shape
    n = w.shape[1]
    per_b = seq // tm

    def mod_spec(j):
        return pl.BlockSpec((None, 1, d), lambda m, k: ((m // per_b) * 9 + j, 0, 0))

    return pl.pallas_call(
        _proj_kernel,
        grid=(t // tm, n // tn),
        in_specs=[pl.BlockSpec((tm, d), lambda m, k: (m, 0)),
                  pl.BlockSpec((1, d), lambda m, k: (0, 0)),
                  mod_spec(jsh), mod_spec(jsc),
                  pl.BlockSpec((d, tn), lambda m, k: (0, k))],
        out_specs=pl.BlockSpec((tm, tn), lambda m, k: (m, k)),
        out_shape=jax.ShapeDtypeStruct((t, n), BF16),
        scratch_shapes=[pltpu.VMEM((tm, d), BF16)],
        compiler_params=_params("parallel", "arbitrary"),
        name="mix_proj",
    )(h, nw.reshape(1, d), mods, mods, w)


def _qkv_kernel(q_ref, k_ref, v_ref, pos_ref, frq_ref, gm_ref, qw_ref, kw_ref, msk_ref,
                qo_ref, ko_ref, vo_ref, *, n_heads, dh, q_scale):
    pos = pos_ref[...].astype(F32)
    ang = pos * frq_ref[...]
    cosv = jnp.cos(ang)
    sinv = jnp.sin(ang)
    m_lo = msk_ref[0:1, :]
    m_hi = msk_ref[1:2, :]
    m_any = m_lo + m_hi
    c_full = m_any * cosv + (1.0 - m_any)
    s_lo = -(m_lo * sinv)
    s_hi = m_hi * sinv
    gm = gm_ref[...]

    def prep(x, w, scale):
        ss = _dot((x * x).astype(BF16), gm)
        y = x * lax.rsqrt(ss * (1.0 / dh) + EPS) * w
        rot = y * c_full + pltpu.roll(y, LANES - 8, axis=1) * s_lo + pltpu.roll(y, 8, axis=1) * s_hi
        return (rot * scale).astype(BF16)

    qw = qw_ref[...]
    kw = kw_ref[...]
    for h in range(n_heads):
        sl = slice(h * LANES, (h + 1) * LANES)
        qo_ref[:, sl] = prep(q_ref[:, sl].astype(F32), qw, q_scale)
        ko_ref[:, sl] = prep(k_ref[:, sl].astype(F32), kw, 1.0)
        vo_ref[sl, :] = v_ref[:, sl].T


def _qkv(proj, positions, frq, gm, qw, kw, msk, width, col0, dh, tm=512):
    t = proj.shape[0]
    n_heads = width // LANES
    cb = col0 // width
    kern = functools.partial(_qkv_kernel, n_heads=n_heads, dh=dh,
                             q_scale=math.log2(math.e) / math.sqrt(dh))
    const = lambda shape: pl.BlockSpec(shape, lambda m: (0, 0))
    out = jax.ShapeDtypeStruct((t, width), BF16)
    return pl.pallas_call(
        kern,
        grid=(t // tm,),
        in_specs=[pl.BlockSpec((tm, width), lambda m: (m, cb)),
                  pl.BlockSpec((tm, width), lambda m: (m, cb + 1)),
                  pl.BlockSpec((tm, width), lambda m: (m, cb + 2)),
                  pl.BlockSpec((tm, 1), lambda m: (m, 0)),
                  const((1, LANES)), const((LANES, LANES)), const((1, LANES)), const((1, LANES)),
                  const((2, LANES))],
        out_specs=[pl.BlockSpec((tm, width), lambda m: (m, 0)),
                   pl.BlockSpec((tm, width), lambda m: (m, 0)),
                   pl.BlockSpec((width, tm), lambda m: (0, m))],
        out_shape=[out, out, jax.ShapeDtypeStruct((width, t), BF16)],
        compiler_params=_params("parallel"),
        name="qkv_prep",
    )(proj, proj, proj, positions, frq, gm, qw, kw, msk)


def _boundary_rows(b_ref, b, level, c):
    h = 1 << level
    if h == 1:
        odd = (lax.broadcasted_iota(jnp.int32, b.shape, 0) & 1) == 1
        return jnp.where(odd, pltpu.roll(b, 1, axis=0), b)
    if h == 2:
        sub = lax.broadcasted_iota(jnp.int32, (8, LANES), 0)
        pieces = []
        for j in range(c // 8):
            lo = jnp.broadcast_to(b_ref[8 * j + 1:8 * j + 2, :], (8, LANES))
            hi = jnp.broadcast_to(b_ref[8 * j + 5:8 * j + 6, :], (8, LANES))
            pieces.append(jnp.where(sub < 4, lo, hi))
        return jnp.concatenate(pieces, axis=0)
    blk = max(2 * h, 8)
    pieces = []
    for j in range(c // blk):
        r = (j * blk // (2 * h)) * 2 * h + h - 1
        pieces.append(jnp.broadcast_to(b_ref[r:r + 1, :], (blk, LANES)))
    return jnp.concatenate(pieces, axis=0)


def _hgrn_kernel(q_ref, f_ref, i_ref, g_ref, lb_ref, nw_ref, tri_ref, o_ref, st_ref, b_ref, *,
                 chunk, layer, heads):
    c = chunk
    rows = q_ref.shape[0]
    n_levels = c.bit_length() - 1
    hs = range(heads)

    @pl.when(pl.program_id(2) == 0)
    def _():
        st_ref[...] = jnp.zeros_like(st_ref)

    lbx = lb_ref[...]
    mx = jnp.max(lbx, axis=0, keepdims=True)
    ex = jnp.exp(lbx - mx)
    lb_all = (jnp.sum(ex[0:layer + 1, :], axis=0, keepdims=True)
              / jnp.sum(ex, axis=0, keepdims=True))
    nw = nw_ref[...]

    ti = lax.broadcasted_iota(jnp.int32, (c, c), 0)
    si = lax.broadcasted_iota(jnp.int32, (c, c), 1)
    lvl = jnp.where(ti > si, 31 - lax.clz(ti ^ si), jnp.where(ti == si, -1, -2))

    def body(ci, carry):
        r0 = pl.multiple_of(ci * c, c)
        tri = tri_ref[...]
        sl = [slice(h * LANES, (h + 1) * LANES) for h in hs]
        qb = [q_ref[pl.ds(r0, c), sl[h]] for h in hs]
        vb = [i_ref[pl.ds(r0, c), sl[h]] for h in hs]
        q = [qb[h].astype(F32) for h in hs]
        lf, k = [], []
        for h in hs:
            lb = lb_all[:, sl[h]]
            f = lb + (1.0 - lb) * _sigmoid(f_ref[pl.ds(r0, c), sl[h]].astype(F32))
            lf.append(jnp.log2(f))
            k.append(1.0 - f)
        parts = []
        for h in hs:
            p1 = lf[h].astype(BF16)
            parts += [p1, (lf[h] - p1.astype(F32)).astype(BF16)]
        cs = _dot(tri, jnp.concatenate(parts, axis=1))
        b = []
        for h in hs:
            o2 = 2 * h * LANES
            b.append(cs[:, o2:o2 + LANES] + cs[:, o2 + LANES:o2 + 2 * LANES])
            b_ref[h] = b[h]
        kb = [k[h].astype(BF16) for h in hs]
        inter = []
        for h in hs:
            b_last = b[h][c - 1:c, :]
            st = st_ref[h]
            inter.append(lax.dot_general((q[h] * jnp.exp2(b[h])).astype(BF16), st.astype(BF16),
                                         _NT, preferred_element_type=F32))
            kh = (k[h] * jnp.exp2(b_last - b[h])).astype(BF16)
            st_ref[h] = st * jnp.exp2(b_last) + lax.dot_general(vb[h], kh, _TN,
                                                                preferred_element_type=F32)
        scores = [jnp.where(lvl == -1, jnp.sum(q[h] * k[h], axis=-1, keepdims=True), 0.0)
                  for h in hs]
        for level in range(n_levels):
            for h in hs:
                m = _boundary_rows(b_ref.at[h], b[h], level, c)
                e = jnp.exp2(_neg_abs(b[h] - m)).astype(BF16)
                s_l = lax.dot_general(qb[h] * e, kb[h] * e, _NT, preferred_element_type=F32)
                scores[h] = jnp.where(lvl == level, s_l, scores[h])
        for h in hs:
            o = inter[h] + _dot(scores[h].astype(BF16), vb[h])
            var = jnp.mean(o * o, axis=-1, keepdims=True)
            on = o * lax.rsqrt(var + EPS) * nw
            g = g_ref[pl.ds(r0, c), sl[h]].astype(F32)
            o_ref[pl.ds(r0, c), sl[h]] = (on * (g * _sigmoid(g))).astype(o_ref.dtype)
        return carry

    lax.fori_loop(0, rows // c, body, 0)


def _hgrn(proj, lb, nw, layer, bsz, seq, n_heads, col0, rows=512, chunk=128, heads=4):
    t = proj.shape[0]
    kdim = LANES
    width = heads * kdim
    ns = seq // rows
    ng = n_heads // heads
    cb = col0 // width

    def col_spec(j):
        return pl.BlockSpec((rows, width), lambda b, h, s: (b * ns + s, cb + j * ng + h))

    return pl.pallas_call(
        functools.partial(_hgrn_kernel, chunk=chunk, layer=layer, heads=heads),
        grid=(bsz, ng, ns),
        in_specs=[col_spec(0), col_spec(1), col_spec(2), col_spec(3),
                  pl.BlockSpec((lb.shape[0], width), lambda b, h, s: (0, h)),
                  pl.BlockSpec((1, kdim), lambda b, h, s: (0, 0)),
                  pl.BlockSpec((chunk, chunk), lambda b, h, s: (0, 0))],
        out_specs=pl.BlockSpec((rows, width), lambda b, h, s: (b * ns + s, h)),
        out_shape=jax.ShapeDtypeStruct((t, n_heads * kdim), BF16),
        scratch_shapes=[pltpu.VMEM((heads, kdim, kdim), F32),
                        pltpu.VMEM((heads, chunk, kdim), F32)],
        compiler_params=_params("parallel", "parallel", "arbitrary"),
        name="hgrn2",
    )(proj, proj, proj, proj, lb, nw.reshape(1, kdim), jnp.tril(jnp.ones((chunk, chunk), BF16)))


def _attn_kernel(q_ref, k_ref, vt_ref, lam_ref, sw_ref, bias_ref, o_ref, qs_ref, m_ref, l_ref,
                 acc_ref, s_ref, mx_ref, *, tq, tk, dh, lam_init, heads):
    qi = pl.program_id(2)
    groups = 2 * heads
    hl = [slice(hh * LANES, (hh + 1) * LANES) for hh in range(heads)]
    for hh in range(heads):
        q = q_ref[:, hl[hh]]
        lane = lax.broadcasted_iota(jnp.int32, q.shape, 1)
        zero = jnp.zeros_like(q)
        qs_ref[2 * hh * tq:(2 * hh + 1) * tq, :] = jnp.where(lane < dh, q, zero)
        qs_ref[(2 * hh + 1) * tq:(2 * hh + 2) * tq, :] = jnp.where(lane < dh, zero, q)
    m_ref[...] = jnp.full_like(m_ref, -jnp.inf)
    l_ref[...] = jnp.zeros_like(l_ref)
    acc_ref[...] = jnp.zeros_like(acc_ref)

    def scores(j, slot):
        k0 = pl.multiple_of(j * tk, tk)
        kb = [k_ref[pl.ds(k0, tk), hl[hh]] for hh in range(heads)]
        for g in range(groups):
            cols = slice(g * tq, (g + 1) * tq)
            st = lax.dot_general(kb[g // 2], qs_ref[cols, :], _NT, preferred_element_type=F32)
            s_ref[slot, :, cols] = st
            mx_ref[slot, :, cols] = jnp.max(st, axis=0, keepdims=True)

    def step(j, slot, masked):
        k0 = pl.multiple_of(j * tk, tk)
        vts = [vt_ref[hl[hh], pl.ds(k0, tk)] for hh in range(heads)]
        for g in range(groups):
            cols = slice(g * tq, (g + 1) * tq)
            vt = vts[g // 2]
            st = s_ref[slot, :, cols]
            if masked:
                st = st + bias_ref[...]
                m_cur = jnp.max(st, axis=0, keepdims=True)
            else:
                m_cur = mx_ref[slot, :, cols]
            m_old = m_ref[:, cols]
            m_new = jnp.maximum(m_old, m_cur)
            alpha = jnp.exp2(m_old - m_new)
            pt = jnp.exp2(st - m_new)
            l_ref[:, cols] = alpha * l_ref[:, cols] + jnp.sum(pt, axis=0, keepdims=True)
            acc_ref[:, cols] = alpha * acc_ref[:, cols] + _dot(vt, pt.astype(BF16))
            m_ref[:, cols] = m_new

    n_full = (qi * tq) // tk
    odd = n_full % 2

    @pl.when(odd == 0)
    def _():
        scores(0, 0)

    @pl.when(odd == 1)
    def _():
        scores(0, 1)
        scores(1, 0)
        step(0, 1, False)

    def pair(i, carry):
        j = odd + 2 * i
        scores(j + 1, 1)
        step(j, 0, False)
        scores(j + 2, 0)
        step(j + 1, 1, False)
        return carry

    lax.fori_loop(0, n_full // 2, pair, 0)
    step(n_full, 0, True)

    lv = lam_ref[...]
    lam = (jnp.exp(jnp.sum(lv[0:1, :] * lv[1:2, :], axis=-1, keepdims=True))
           - jnp.exp(jnp.sum(lv[2:3, :] * lv[3:4, :], axis=-1, keepdims=True)) + lam_init)
    ot = acc_ref[...] / l_ref[...]
    for hh in range(heads):
        c0 = 2 * hh * tq
        o = (ot[:, c0:c0 + tq] - lam * ot[:, c0 + tq:c0 + 2 * tq]).T
        var = jnp.mean(o * o, axis=-1, keepdims=True)
        o_ref[:, hl[hh]] = (o * lax.rsqrt(var + EPS) * sw_ref[...]
                            * (1.0 - lam_init)).astype(o_ref.dtype)


def _attn(q, k, vt, lamv, sw, bsz, seq, n_heads, dh, lam_init, tq=512, tk=512, heads=2):
    t = q.shape[0]
    nq = seq // tq
    assert tq == tk
    kern = functools.partial(_attn_kernel, tq=tq, tk=tk, dh=dh, lam_init=lam_init, heads=heads)
    key = jnp.arange(tk)[:, None]
    qry = jnp.arange(tq)[None, :]
    bias = jnp.where(key <= qry, 0.0, -jnp.inf).astype(F32)
    width = heads * LANES
    cols = 2 * heads * tq
    return pl.pallas_call(
        kern,
        grid=(bsz, n_heads // heads, nq),
        in_specs=[pl.BlockSpec((tq, width), lambda b, h, i: (b * nq + i, h)),
                  pl.BlockSpec((seq, width), lambda b, h, i: (b, h)),
                  pl.BlockSpec((width, seq), lambda b, h, i: (h, b)),
                  pl.BlockSpec(lamv.shape, lambda b, h, i: (0, 0)),
                  pl.BlockSpec((1, LANES), lambda b, h, i: (0, 0)),
                  pl.BlockSpec((tk, tq), lambda b, h, i: (0, 0))],
        out_specs=pl.BlockSpec((tq, width), lambda b, h, i: (b * nq + i, h)),
        out_shape=jax.ShapeDtypeStruct((t, n_heads * LANES), BF16),
        scratch_shapes=[pltpu.VMEM((cols, LANES), BF16),
                        pltpu.VMEM((1, cols), F32),
                        pltpu.VMEM((1, cols), F32),
                        pltpu.VMEM((LANES, cols), F32),
                        pltpu.VMEM((2, tk, cols), F32),
                        pltpu.VMEM((2, 1, cols), F32)],
        compiler_params=_params("parallel", "parallel", "arbitrary"),
        name="diff_attn",
    )(q, k, vt, lamv, sw.reshape(1, LANES), bias)


def _merge_kernel(oa_ref, ob_ref, wa_ref, wb_ref, wo_ref, *rest, n_slabs):
    ga_refs = rest[:n_slabs]
    gb_refs = rest[n_slabs:2 * n_slabs]
    h_ref, g_ref, o_ref, mix_ref = rest[2 * n_slabs:]
    oa = oa_ref[...]
    ob = ob_ref[...]
    tn = ga_refs[0].shape[1]
    for j in range(n_slabs):
        cols = slice(j * tn, (j + 1) * tn)
        ya = _dot(oa, wa_ref[:, cols])
        yb = _dot(ob, wb_ref[:, cols])
        ga = ga_refs[j][...].astype(F32)
        gb = gb_refs[j][...].astype(F32)
        mix_ref[:, cols] = (_sigmoid(ga) * ya + _sigmoid(gb) * yb).astype(mix_ref.dtype)
    o_ref[...] = h_ref[...] + g_ref[...] * _dot(mix_ref[...], wo_ref[...])


def _merge(oa, ob, wa, wb, wo, proj, col_ga, col_gb, h, mods, jg, seq, tm=512, tn=1024):
    t, kdim = oa.shape
    d = wa.shape[1]
    n_slabs = d // tn
    per_b = seq // tm
    resident = lambda shape: pl.BlockSpec(shape, lambda m: (0, 0), pipeline_mode=pl.Buffered(1))
    gate = lambda col, j: pl.BlockSpec((tm, tn), lambda m: (m, col // tn + j))
    return pl.pallas_call(
        functools.partial(_merge_kernel, n_slabs=n_slabs),
        grid=(t // tm,),
        in_specs=([pl.BlockSpec((tm, kdim), lambda m: (m, 0)),
                   pl.BlockSpec((tm, kdim), lambda m: (m, 0)),
                   resident((kdim, d)), resident((kdim, d)), resident((d, d))]
                  + [gate(col_ga, j) for j in range(n_slabs)]
                  + [gate(col_gb, j) for j in range(n_slabs)]
                  + [pl.BlockSpec((tm, d), lambda m: (m, 0)),
                     pl.BlockSpec((None, 1, d), lambda m: ((m // per_b) * 9 + jg, 0, 0))]),
        out_specs=pl.BlockSpec((tm, d), lambda m: (m, 0)),
        out_shape=jax.ShapeDtypeStruct((t, d), F32),
        scratch_shapes=[pltpu.VMEM((tm, d), BF16)],
        compiler_params=_params("parallel"),
        name="merge_out_proj",
    )(oa, ob, wa, wb, wo, *([proj] * (2 * n_slabs)), h, mods)


def kernel(x, c, positions, ada_w, ada_b, norm1_w, ffn1_w_in, ffn1_w_out, norm2_w, mix_w_in, hgrn_lb, hgrn_norm_w, hgrn_w_out, diff_q_norm_w, diff_k_norm_w, diff_lq1, diff_lk1, diff_lq2, diff_lk2, diff_subln_w, diff_w_out, mix_w_o, norm3_w, ffn2_w_in, ffn2_w_out):
    bsz, seq, d = x.shape
    depth = ada_w.shape[0]
    t = bsz * seq
    kdim = hgrn_norm_w.shape[-1]
    fw = hgrn_lb.shape[-1]
    n_hgrn = fw // kdim
    dh = diff_q_norm_w.shape[-1]
    vdim = diff_subln_w.shape[-1]
    qk_w = diff_w_out.shape[1]
    n_diff = qk_w // vdim
    rope_dim = dh // 4
    half = rope_dim // 2
    assert kdim == LANES and vdim == LANES and 2 * dh == LANES

    inv_freq = ROPE_THETA ** (-jnp.arange(half, dtype=F32) / half)
    lane = jnp.arange(LANES)
    dpos = lane % dh
    frq = jnp.where(dpos < rope_dim, inv_freq[dpos % half], 0.0).reshape(1, LANES)
    msk = jnp.stack([(dpos < half), (dpos >= half) & (dpos < rope_dim)]).astype(F32)
    gm = (lane[:, None] // dh == lane[None, :] // dh).astype(BF16)
    pos2 = positions.reshape(t, 1)

    h = x.reshape(t, d)
    for l in range(depth):
        lam_init = 0.8 - 0.6 * math.exp(-0.3 * l)
        mods = _ada(c, ada_w[l], ada_b[l]).reshape(bsz * 9, 1, d)
        h = _ffn(h, norm1_w[l], mods, 0, 1, 2, ffn1_w_in[l].astype(BF16),
                 ffn1_w_out[l].astype(BF16), seq)
        proj = _proj(h, norm2_w[l], mods, 3, 4, mix_w_in[l].astype(BF16), seq)
        o_a = _hgrn(proj, hgrn_lb, hgrn_norm_w[l], l, bsz, seq, n_hgrn, 0)
        qw = jnp.tile(diff_q_norm_w[l], LANES // dh).reshape(1, LANES)
        kw = jnp.tile(diff_k_norm_w[l], LANES // dh).reshape(1, LANES)
        qr, kr, vr = _qkv(proj, pos2, frq, gm, qw, kw, msk, qk_w, 4 * fw, dh)
        lamv = jnp.stack([diff_lq1[l], diff_lk1[l], diff_lq2[l], diff_lk2[l]])
        o_b = _attn(qr, kr, vr, lamv, diff_subln_w[l], bsz, seq, n_diff, dh, lam_init)
        h = _merge(o_a, o_b, hgrn_w_out[l].astype(BF16), diff_w_out[l].astype(BF16),
                   mix_w_o[l].astype(BF16), proj, 4 * fw + 3 * qk_w, 4 * fw + 3 * qk_w + d,
                   h, mods, 5, seq)
        h = _ffn(h, norm3_w[l], mods, 6, 7, 8, ffn2_w_in[l].astype(BF16),
                 ffn2_w_out[l].astype(BF16), seq)
    return h.reshape(bsz, seq, d)
```

```python
import functools
import math

import jax
import jax.numpy as jnp
from jax import lax
from jax.experimental import pallas as pl
from jax.experimental.pallas import tpu as pltpu

F32 = jnp.float32
BF16 = jnp.bfloat16

EPS = 1e-6
ROPE_THETA = 500000.0
LANES = 128
VMEM_LIMIT = 56 * 1024 * 1024

_NT = (((1,), (1,)), ((), ()))
_TN = (((0,), (0,)), ((), ()))


def _params(*sem):
    return pltpu.CompilerParams(dimension_semantics=sem, vmem_limit_bytes=VMEM_LIMIT)


def _dot(a, b):
    return jnp.dot(a, b, preferred_element_type=F32)


def _sigmoid(x):
    return 0.5 * jnp.tanh(0.5 * x) + 0.5


def _norm_mod(h, nw, sh, sc):
    var = jnp.mean(h * h, axis=-1, keepdims=True)
    return (h * lax.rsqrt(var + EPS) * nw) * (1.0 + sc) + sh


def _with_casts(kernel_fn, n_in, n_out, n_cast, first_of_axis=None):
    def wrapped(*refs):
        ins = refs[:n_in]
        cast_in = refs[n_in:n_in + n_cast]
        outs = refs[n_in + n_cast:n_in + n_cast + n_out]
        cast_out = refs[n_in + n_cast + n_out:n_in + 2 * n_cast + n_out]
        scratch = refs[n_in + 2 * n_cast + n_out:]

        def cast_all():
            for src, dst in zip(cast_in, cast_out):
                dst[...] = src[...].astype(dst.dtype)

        if n_cast and first_of_axis is not None:
            pl.when(pl.program_id(first_of_axis) == 0)(cast_all)
        else:
            cast_all()
        kernel_fn(*ins, *outs, *scratch)
    return wrapped


def _cast_specs(weights, n_steps, step_of):
    specs, shapes = [], []
    for w in weights:
        r, c = w.shape
        rb = r // n_steps
        assert rb * n_steps == r and rb % 16 == 0
        specs.append(pl.BlockSpec((rb, c), lambda *idx: (step_of(*idx), 0)))
        shapes.append(jax.ShapeDtypeStruct((r, c), BF16))
    return specs, shapes


NORM_ROWS = 16


def _norm_mod_to(h_ref, nw_ref, sh_ref, sc_ref, xn_ref):
    nw = nw_ref[...]
    sh = sh_ref[...]
    sc = sc_ref[...]

    def body(i, carry):
        r = pl.multiple_of(i * NORM_ROWS, NORM_ROWS)
        xn = _norm_mod(h_ref[pl.ds(r, NORM_ROWS), :], nw, sh, sc)
        xn_ref[pl.ds(r, NORM_ROWS), :] = xn.astype(xn_ref.dtype)
        return carry

    lax.fori_loop(0, h_ref.shape[0] // NORM_ROWS, body, 0, unroll=8)


def _ada_kernel(c_ref, w_ref, b_ref, o_ref):
    c = c_ref[...]
    a = (c * _sigmoid(c)).astype(BF16)
    o_ref[...] = _dot(a, w_ref[...].astype(BF16)) + b_ref[...]


def _ada(c, w, b, tn=1024):
    bsz, d = c.shape
    n = w.shape[1]
    rows = 8
    cp = jnp.zeros((rows, d), F32).at[:bsz].set(c)
    out = pl.pallas_call(
        _ada_kernel,
        grid=(n // tn,),
        in_specs=[pl.BlockSpec((rows, d), lambda j: (0, 0)),
                  pl.BlockSpec((d, tn), lambda j: (0, j)),
                  pl.BlockSpec((1, tn), lambda j: (0, j))],
        out_specs=pl.BlockSpec((rows, tn), lambda j: (0, j)),
        out_shape=jax.ShapeDtypeStruct((rows, n), F32),
        compiler_params=_params("parallel"),
        name="adaln",
    )(cp, w, b.reshape(1, n))
    return out[:bsz]


def _ffn_kernel(h_ref, nw_ref, sh_ref, sc_ref, g_ref, wg_ref, wu_ref, wo_ref, o_ref,
                xn_ref, acc_ref):
    f = pl.program_id(1)

    @pl.when(f == 0)
    def _():
        _norm_mod_to(h_ref, nw_ref, sh_ref, sc_ref, xn_ref)
        acc_ref[...] = jnp.zeros_like(acc_ref)

    x = xn_ref[...]
    g = _dot(x, wg_ref[...])
    u = _dot(x, wu_ref[...])
    a = (g * _sigmoid(g)) * u
    acc_ref[...] += _dot(a.astype(BF16), wo_ref[...])

    @pl.when(f == pl.num_programs(1) - 1)
    def _():
        o_ref[...] = h_ref[...] + (0.5 * g_ref[...]) * acc_ref[...]


def _ffn(h, nw, mods, jsh, jsc, jg, w_in, w_out, seq, casts=(), tm=512, tf=512):
    t, d = h.shape
    dff = w_out.shape[0]
    nf = dff // tf
    per_b = seq // tm

    def mod_spec(j):
        return pl.BlockSpec((None, 1, d), lambda m, f: ((m // per_b) * 9 + j, 0, 0))

    cast_specs, cast_shapes = _cast_specs(casts, t // tm, lambda m, f: m)
    return pl.pallas_call(
        _with_casts(_ffn_kernel, 8, 1, len(casts), first_of_axis=1),
        grid=(t // tm, nf),
        in_specs=[pl.BlockSpec((tm, d), lambda m, f: (m, 0)),
                  pl.BlockSpec((1, d), lambda m, f: (0, 0)),
                  mod_spec(jsh), mod_spec(jsc), mod_spec(jg),
                  pl.BlockSpec((d, tf), lambda m, f: (0, f)),
                  pl.BlockSpec((d, tf), lambda m, f: (0, f + nf)),
                  pl.BlockSpec((tf, d), lambda m, f: (f, 0))] + cast_specs,
        out_specs=[pl.BlockSpec((tm, d), lambda m, f: (m, 0))] + cast_specs,
        out_shape=[jax.ShapeDtypeStruct((t, d), F32)] + cast_shapes,
        scratch_shapes=[pltpu.VMEM((tm, d), BF16), pltpu.VMEM((tm, d), F32)],
        compiler_params=_params("parallel", "arbitrary"),
        name="ffn",
    )(h, nw.reshape(1, d), mods, mods, mods, w_in, w_in, w_out, *casts)


def _proj_kernel(h_ref, nw_ref, sh_ref, sc_ref, w_ref, o_ref, xn_ref):
    @pl.when(pl.program_id(1) == 0)
    def _():
        _norm_mod_to(h_ref, nw_ref, sh_ref, sc_ref, xn_ref)

    o_ref[...] = _dot(xn_ref[...], w_ref[...]).astype(o_ref.dtype)


def _proj(h, nw, mods, jsh, jsc, w, seq, tm=1024, tn=1024):
    t, d = h.shape
    n = w.shape[1]
    per_b = seq // tm

    def mod_spec(j):
        return pl.BlockSpec((None, 1, d), lambda m, k: ((m // per_b) * 9 + j, 0, 0))

    return pl.pallas_call(
        _proj_kernel,
        grid=(t // tm, n // tn),
        in_specs=[pl.BlockSpec((tm, d), lambda m, k: (m, 0)),
                  pl.BlockSpec((1, d), lambda m, k: (0, 0)),
                  mod_spec(jsh), mod_spec(jsc),
                  pl.BlockSpec((d, tn), lambda m, k: (0, k))],
        out_specs=pl.BlockSpec((tm, tn), lambda m, k: (m, k)),
        out_shape=jax.ShapeDtypeStruct((t, n), BF16),
        scratch_shapes=[pltpu.VMEM((tm, d), BF16)],
        compiler_params=_params("parallel", "arbitrary"),
        name="mix_proj",
    )(h, nw.reshape(1, d), mods, mods, w)


def _qkv_kernel(q_ref, k_ref, v_ref, pos_ref, frq_ref, gm_ref, qw_ref, kw_ref, msk_ref,
                qo_ref, ko_ref, vo_ref, *, n_heads, dh, q_scale):
    pos = pos_ref[...].astype(F32)
    ang = pos * frq_ref[...]
    cosv = jnp.cos(ang)
    sinv = jnp.sin(ang)
    m_lo = msk_ref[0:1, :]
    m_hi = msk_ref[1:2, :]
    m_any = m_lo + m_hi
    c_full = m_any * cosv + (1.0 - m_any)
    s_lo = -(m_lo * sinv)
    s_hi = m_hi * sinv
    gm = gm_ref[...]

    def prep(x, w, scale):
        ss = _dot((x * x).astype(BF16), gm)
        y = x * lax.rsqrt(ss * (1.0 / dh) + EPS) * w
        rot = y * c_full + pltpu.roll(y, LANES - 8, axis=1) * s_lo + pltpu.roll(y, 8, axis=1) * s_hi
        return (rot * scale).astype(BF16)

    qw = qw_ref[...]
    kw = kw_ref[...]
    for h in range(n_heads):
        sl = slice(h * LANES, (h + 1) * LANES)
        qo_ref[:, sl] = prep(q_ref[:, sl].astype(F32), qw, q_scale)
        ko_ref[:, sl] = prep(k_ref[:, sl].astype(F32), kw, 1.0)
        vo_ref[sl, :] = v_ref[:, sl].T


def _qkv(proj, positions, frq, gm, qw, kw, msk, width, col0, dh, casts=(), tm=512):
    t = proj.shape[0]
    n_heads = width // LANES
    cb = col0 // width
    kern = functools.partial(_qkv_kernel, n_heads=n_heads, dh=dh,
                             q_scale=math.log2(math.e) / math.sqrt(dh))
    const = lambda shape: pl.BlockSpec(shape, lambda m: (0, 0))
    out = jax.ShapeDtypeStruct((t, width), BF16)
    cast_specs, cast_shapes = _cast_specs(casts, t // tm, lambda m: m)
    return pl.pallas_call(
        _with_casts(kern, 9, 3, len(casts)),
        grid=(t // tm,),
        in_specs=[pl.BlockSpec((tm, width), lambda m: (m, cb)),
                  pl.BlockSpec((tm, width), lambda m: (m, cb + 1)),
                  pl.BlockSpec((tm, width), lambda m: (m, cb + 2)),
                  pl.BlockSpec((tm, 1), lambda m: (m, 0)),
                  const((1, LANES)), const((LANES, LANES)), const((1, LANES)), const((1, LANES)),
                  const((2, LANES))] + cast_specs,
        out_specs=[pl.BlockSpec((tm, width), lambda m: (m, 0)),
                   pl.BlockSpec((tm, width), lambda m: (m, 0)),
                   pl.BlockSpec((width, tm), lambda m: (0, m))] + cast_specs,
        out_shape=[out, out, jax.ShapeDtypeStruct((width, t), BF16)] + cast_shapes,
        compiler_params=_params("parallel"),
        name="qkv_prep",
    )(proj, proj, proj, positions, frq, gm, qw, kw, msk, *casts)


def _boundary_rows(b_ref, b, level, c):
    h = 1 << level
    if h == 1:
        odd = (lax.broadcasted_iota(jnp.int32, b.shape, 0) & 1) == 1
        return jnp.where(odd, pltpu.roll(b, 1, axis=0), b)
    if h == 2:
        sub = lax.broadcasted_iota(jnp.int32, (8, LANES), 0)
        pieces = []
        for j in range(c // 8):
            lo = jnp.broadcast_to(b_ref[8 * j + 1:8 * j + 2, :], (8, LANES))
            hi = jnp.broadcast_to(b_ref[8 * j + 5:8 * j + 6, :], (8, LANES))
            pieces.append(jnp.where(sub < 4, lo, hi))
        return jnp.concatenate(pieces, axis=0)
    blk = max(2 * h, 8)
    pieces = []
    for j in range(c // blk):
        r = (j * blk // (2 * h)) * 2 * h + h - 1
        pieces.append(jnp.broadcast_to(b_ref[r:r + 1, :], (blk, LANES)))
    return jnp.concatenate(pieces, axis=0)


def _hgrn_kernel(q_ref, f_ref, i_ref, g_ref, lb_ref, nw_ref, tri_ref, o_ref, st_ref, b_ref, *,
                 chunk, layer, heads):
    c = chunk
    rows = q_ref.shape[0]
    n_levels = c.bit_length() - 1
    hs = range(heads)

    @pl.when(pl.program_id(2) == 0)
    def _():
        st_ref[...] = jnp.zeros_like(st_ref)

    lbx = lb_ref[...]
    mx = jnp.max(lbx, axis=0, keepdims=True)
    ex = jnp.exp(lbx - mx)
    lb_all = (jnp.sum(ex[0:layer + 1, :], axis=0, keepdims=True)
              / jnp.sum(ex, axis=0, keepdims=True))
    nw = nw_ref[...]

    ti = lax.broadcasted_iota(jnp.int32, (c, c), 0)
    si = lax.broadcasted_iota(jnp.int32, (c, c), 1)
    lvl = jnp.where(ti > si, 31 - lax.clz(ti ^ si), jnp.where(ti == si, -1, -2))
    row = lax.broadcasted_iota(jnp.int32, (c, LANES), 0)
    sgn = [jnp.where(((row >> level) & 1) == 1, 1.0, -1.0) for level in range(n_levels)]

    def body(ci, carry):
        r0 = pl.multiple_of(ci * c, c)
        tri = tri_ref[...]
        sl = [slice(h * LANES, (h + 1) * LANES) for h in hs]
        qb = [q_ref[pl.ds(r0, c), sl[h]] for h in hs]
        vb = [i_ref[pl.ds(r0, c), sl[h]] for h in hs]
        q = [qb[h].astype(F32) for h in hs]
        lf, k = [], []
        for h in hs:
            lb = lb_all[:, sl[h]]
            f = lb + (1.0 - lb) * _sigmoid(f_ref[pl.ds(r0, c), sl[h]].astype(F32))
            lf.append(jnp.log2(f))
            k.append(1.0 - f)
        parts = []
        for h in hs:
            p1 = lf[h].astype(BF16)
            parts += [p1, (lf[h] - p1.astype(F32)).astype(BF16)]
        cs = _dot(tri, jnp.concatenate(parts, axis=1))
        b = []
        for h in hs:
            o2 = 2 * h * LANES
            b.append(cs[:, o2:o2 + LANES] + cs[:, o2 + LANES:o2 + 2 * LANES])
            b_ref[h] = b[h]
        kb = [k[h].astype(BF16) for h in hs]
        inter = []
        for h in hs:
            b_last = b[h][c - 1:c, :]
            st = st_ref[h]
            inter.append(lax.dot_general((q[h] * jnp.exp2(b[h])).astype(BF16), st.astype(BF16),
                                         _NT, preferred_element_type=F32))
            kh = (k[h] * jnp.exp2(b_last - b[h])).astype(BF16)
            st_ref[h] = st * jnp.exp2(b_last) + lax.dot_general(vb[h], kh, _TN,
                                                                preferred_element_type=F32)
        scores = [jnp.where(lvl == -1, jnp.sum(q[h] * k[h], axis=-1, keepdims=True), 0.0)
                  for h in hs]
        for level in range(n_levels):
            for h in hs:
                m = _boundary_rows(b_ref.at[h], b[h], level, c)
                e = jnp.exp2((b[h] - m) * sgn[level]).astype(BF16)
                s_l = lax.dot_general(qb[h] * e, kb[h] * e, _NT, preferred_element_type=F32)
                scores[h] = jnp.where(lvl == level, s_l, scores[h])
        for h in hs:
            o = inter[h] + _dot(scores[h].astype(BF16), vb[h])
            var = jnp.mean(o * o, axis=-1, keepdims=True)
            on = o * lax.rsqrt(var + EPS) * nw
            g = g_ref[pl.ds(r0, c), sl[h]].astype(F32)
            o_ref[pl.ds(r0, c), sl[h]] = (on * (g * _sigmoid(g))).astype(o_ref.dtype)
        return carry

    lax.fori_loop(0, rows // c, body, 0)


def _hgrn(proj, lb, nw, layer, bsz, seq, n_heads, col0, casts=(), rows=512, chunk=128, heads=8):
    t = proj.shape[0]
    kdim = LANES
    width = heads * kdim
    ns = seq // rows
    ng = n_heads // heads
    cb = col0 // width

    def col_spec(j):
        return pl.BlockSpec((rows, width), lambda b, h, s: (b * ns + s, cb + j * ng + h))

    cast_specs, cast_shapes = _cast_specs(casts, bsz * ng * ns,
                                          lambda b, h, s: (b * ng + h) * ns + s)
    kern = functools.partial(_hgrn_kernel, chunk=chunk, layer=layer, heads=heads)
    return pl.pallas_call(
        _with_casts(kern, 7, 1, len(casts)),
        grid=(bsz, ng, ns),
        in_specs=[col_spec(0), col_spec(1), col_spec(2), col_spec(3),
                  pl.BlockSpec((lb.shape[0], width), lambda b, h, s: (0, h)),
                  pl.BlockSpec((1, kdim), lambda b, h, s: (0, 0)),
                  pl.BlockSpec((chunk, chunk), lambda b, h, s: (0, 0))] + cast_specs,
        out_specs=[pl.BlockSpec((rows, width), lambda b, h, s: (b * ns + s, h))] + cast_specs,
        out_shape=[jax.ShapeDtypeStruct((t, n_heads * kdim), BF16)] + cast_shapes,
        scratch_shapes=[pltpu.VMEM((heads, kdim, kdim), F32),
                        pltpu.VMEM((heads, chunk, kdim), F32)],
        compiler_params=_params("parallel", "parallel", "arbitrary"),
        name="hgrn2",
    )(proj, proj, proj, proj, lb, nw.reshape(1, kdim), jnp.tril(jnp.ones((chunk, chunk), BF16)),
      *casts)


def _attn_kernel(q_ref, k_ref, vt_ref, lam_ref, sw_ref, bias_ref, o_ref, qs_ref, m_ref, l_ref,
                 acc_ref, s_ref, mx_ref, *, tq, tk, dh, lam_init, heads):
    qi = pl.program_id(2)
    groups = 2 * heads
    hl = [slice(hh * LANES, (hh + 1) * LANES) for hh in range(heads)]
    for hh in range(heads):
        q = q_ref[:, hl[hh]]
        lane = lax.broadcasted_iota(jnp.int32, q.shape, 1)
        zero = jnp.zeros_like(q)
        qs_ref[2 * hh * tq:(2 * hh + 1) * tq, :] = jnp.where(lane < dh, q, zero)
        qs_ref[(2 * hh + 1) * tq:(2 * hh + 2) * tq, :] = jnp.where(lane < dh, zero, q)
    m_ref[...] = jnp.full_like(m_ref, -jnp.inf)
    l_ref[...] = jnp.zeros_like(l_ref)
    acc_ref[...] = jnp.zeros_like(acc_ref)

    def scores(j, slot):
        k0 = pl.multiple_of(j * tk, tk)
        kb = [k_ref[pl.ds(k0, tk), hl[hh]] for hh in range(heads)]
        for g in range(groups):
            cols = slice(g * tq, (g + 1) * tq)
            st = lax.dot_general(kb[g // 2], qs_ref[cols, :], _NT, preferred_element_type=F32)
            s_ref[slot, :, cols] = st
            mx_ref[slot, :, cols] = jnp.max(st, axis=0, keepdims=True)

    def step(j, slot, masked):
        k0 = pl.multiple_of(j * tk, tk)
        vts = [vt_ref[hl[hh], pl.ds(k0, tk)] for hh in range(heads)]
        for g in range(groups):
            cols = slice(g * tq, (g + 1) * tq)
            vt = vts[g // 2]
            st = s_ref[slot, :, cols]
            if masked:
                st = st + bias_ref[...]
                m_cur = jnp.max(st, axis=0, keepdims=True)
            else:
                m_cur = mx_ref[slot, :, cols]
            m_old = m_ref[:, cols]
            m_new = jnp.maximum(m_old, m_cur)
            alpha = jnp.exp2(m_old - m_new)
            pt = jnp.exp2(st - m_new)
            l_ref[:, cols] = alpha * l_ref[:, cols] + jnp.sum(pt, axis=0, keepdims=True)
            acc_ref[:, cols] = alpha * acc_ref[:, cols] + _dot(vt, pt.astype(BF16))
            m_ref[:, cols] = m_new

    n_full = (qi * tq) // tk
    odd = n_full % 2

    @pl.when(odd == 0)
    def _():
        scores(0, 0)

    @pl.when(odd == 1)
    def _():
        scores(0, 1)
        scores(1, 0)
        step(0, 1, False)

    def pair(i, carry):
        j = odd + 2 * i
        scores(j + 1, 1)
        step(j, 0, False)
        scores(j + 2, 0)
        step(j + 1, 1, False)
        return carry

    lax.fori_loop(0, n_full // 2, pair, 0)
    step(n_full, 0, True)

    lv = lam_ref[...]
    lam = (jnp.exp(jnp.sum(lv[0:1, :] * lv[1:2, :], axis=-1, keepdims=True))
           - jnp.exp(jnp.sum(lv[2:3, :] * lv[3:4, :], axis=-1, keepdims=True)) + lam_init)
    ot = acc_ref[...] / l_ref[...]
    for hh in range(heads):
        c0 = 2 * hh * tq
        o = (ot[:, c0:c0 + tq] - lam * ot[:, c0 + tq:c0 + 2 * tq]).T
        var = jnp.mean(o * o, axis=-1, keepdims=True)
        o_ref[:, hl[hh]] = (o * lax.rsqrt(var + EPS) * sw_ref[...]
                            * (1.0 - lam_init)).astype(o_ref.dtype)


def _attn(q, k, vt, lamv, sw, bsz, seq, n_heads, dh, lam_init, tq=512, tk=512, heads=2):
    t = q.shape[0]
    nq = seq // tq
    assert tq == tk
    kern = functools.partial(_attn_kernel, tq=tq, tk=tk, dh=dh, lam_init=lam_init, heads=heads)
    key = jnp.arange(tk)[:, None]
    qry = jnp.arange(tq)[None, :]
    bias = jnp.where(key <= qry, 0.0, -jnp.inf).astype(F32)
    width = heads * LANES
    cols = 2 * heads * tq
    return pl.pallas_call(
        kern,
        grid=(bsz, n_heads // heads, nq),
        in_specs=[pl.BlockSpec((tq, width), lambda b, h, i: (b * nq + i, h)),
                  pl.BlockSpec((seq, width), lambda b, h, i: (b, h)),
                  pl.BlockSpec((width, seq), lambda b, h, i: (h, b)),
                  pl.BlockSpec(lamv.shape, lambda b, h, i: (0, 0)),
                  pl.BlockSpec((1, LANES), lambda b, h, i: (0, 0)),
                  pl.BlockSpec((tk, tq), lambda b, h, i: (0, 0))],
        out_specs=pl.BlockSpec((tq, width), lambda b, h, i: (b * nq + i, h)),
        out_shape=jax.ShapeDtypeStruct((t, n_heads * LANES), BF16),
        scratch_shapes=[pltpu.VMEM((cols, LANES), BF16),
                        pltpu.VMEM((1, cols), F32),
                        pltpu.VMEM((1, cols), F32),
                        pltpu.VMEM((LANES, cols), F32),
                        pltpu.VMEM((2, tk, cols), F32),
                        pltpu.VMEM((2, 1, cols), F32)],
        compiler_params=_params("parallel", "parallel", "arbitrary"),
        name="diff_attn",
    )(q, k, vt, lamv, sw.reshape(1, LANES), bias)


def _merge_kernel(oa_ref, ob_ref, wa_ref, wb_ref, wo_ref, *rest, n_slabs):
    ga_refs = rest[:n_slabs]
    gb_refs = rest[n_slabs:2 * n_slabs]
    h_ref, g_ref, o_ref, mix_ref = rest[2 * n_slabs:]
    oa = oa_ref[...]
    ob = ob_ref[...]
    tn = ga_refs[0].shape[1]
    for j in range(n_slabs):
        cols = slice(j * tn, (j + 1) * tn)
        ya = _dot(oa, wa_ref[:, cols])
        yb = _dot(ob, wb_ref[:, cols])
        ga = ga_refs[j][...].astype(F32)
        gb = gb_refs[j][...].astype(F32)
        mix_ref[:, cols] = (_sigmoid(ga) * ya + _sigmoid(gb) * yb).astype(mix_ref.dtype)
    o_ref[...] = h_ref[...] + g_ref[...] * _dot(mix_ref[...], wo_ref[...])


def _merge(oa, ob, wa, wb, wo, proj, col_ga, col_gb, h, mods, jg, seq, tm=512, tn=1024):
    t, kdim = oa.shape
    d = wa.shape[1]
    n_slabs = d // tn
    per_b = seq // tm
    resident = lambda shape: pl.BlockSpec(shape, lambda m: (0, 0), pipeline_mode=pl.Buffered(1))
    gate = lambda col, j: pl.BlockSpec((tm, tn), lambda m: (m, col // tn + j))
    return pl.pallas_call(
        functools.partial(_merge_kernel, n_slabs=n_slabs),
        grid=(t // tm,),
        in_specs=([pl.BlockSpec((tm, kdim), lambda m: (m, 0)),
                   pl.BlockSpec((tm, kdim), lambda m: (m, 0)),
                   resident((kdim, d)), resident((kdim, d)), resident((d, d))]
                  + [gate(col_ga, j) for j in range(n_slabs)]
                  + [gate(col_gb, j) for j in range(n_slabs)]
                  + [pl.BlockSpec((tm, d), lambda m: (m, 0)),
                     pl.BlockSpec((None, 1, d), lambda m: ((m // per_b) * 9 + jg, 0, 0))]),
        out_specs=pl.BlockSpec((tm, d), lambda m: (m, 0)),
        out_shape=jax.ShapeDtypeStruct((t, d), F32),
        scratch_shapes=[pltpu.VMEM((tm, d), BF16)],
        compiler_params=_params("parallel"),
        name="merge_out_proj",
    )(oa, ob, wa, wb, wo, *([proj] * (2 * n_slabs)), h, mods)


def kernel(x, c, positions, ada_w, ada_b, norm1_w, ffn1_w_in, ffn1_w_out, norm2_w, mix_w_in, hgrn_lb, hgrn_norm_w, hgrn_w_out, diff_q_norm_w, diff_k_norm_w, diff_lq1, diff_lk1, diff_lq2, diff_lk2, diff_subln_w, diff_w_out, mix_w_o, norm3_w, ffn2_w_in, ffn2_w_out):
    bsz, seq, d = x.shape
    depth = ada_w.shape[0]
    t = bsz * seq
    kdim = hgrn_norm_w.shape[-1]
    fw = hgrn_lb.shape[-1]
    n_hgrn = fw // kdim
    dh = diff_q_norm_w.shape[-1]
    vdim = diff_subln_w.shape[-1]
    qk_w = diff_w_out.shape[1]
    n_diff = qk_w // vdim
    rope_dim = dh // 4
    half = rope_dim // 2
    assert kdim == LANES and vdim == LANES and 2 * dh == LANES

    inv_freq = ROPE_THETA ** (-jnp.arange(half, dtype=F32) / half)
    lane = jnp.arange(LANES)
    dpos = lane % dh
    frq = jnp.where(dpos < rope_dim, inv_freq[dpos % half], 0.0).reshape(1, LANES)
    msk = jnp.stack([(dpos < half), (dpos >= half) & (dpos < rope_dim)]).astype(F32)
    gm = (lane[:, None] // dh == lane[None, :] // dh).astype(BF16)
    pos2 = positions.reshape(t, 1)

    h = x.reshape(t, d)
    for l in range(depth):
        lam_init = 0.8 - 0.6 * math.exp(-0.3 * l)
        mods = _ada(c, ada_w[l], ada_b[l]).reshape(bsz * 9, 1, d)
        h, w_mix = _ffn(h, norm1_w[l], mods, 0, 1, 2, ffn1_w_in[l].astype(BF16),
                        ffn1_w_out[l].astype(BF16), seq, casts=(mix_w_in[l],))
        proj = _proj(h, norm2_w[l], mods, 3, 4, w_mix, seq)
        o_a, w2_in, w2_out = _hgrn(proj, hgrn_lb, hgrn_norm_w[l], l, bsz, seq, n_hgrn, 0,
                                   casts=(ffn2_w_in[l], ffn2_w_out[l]))
        qw = jnp.tile(diff_q_norm_w[l], LANES // dh).reshape(1, LANES)
        kw = jnp.tile(diff_k_norm_w[l], LANES // dh).reshape(1, LANES)
        qr, kr, vr, w_a, w_b, w_o = _qkv(proj, pos2, frq, gm, qw, kw, msk, qk_w, 4 * fw, dh,
                                         casts=(hgrn_w_out[l], diff_w_out[l], mix_w_o[l]))
        lamv = jnp.stack([diff_lq1[l], diff_lk1[l], diff_lq2[l], diff_lk2[l]])
        o_b = _attn(qr, kr, vr, lamv, diff_subln_w[l], bsz, seq, n_diff, dh, lam_init)
        h = _merge(o_a, o_b, w_a, w_b, w_o, proj, 4 * fw + 3 * qk_w, 4 * fw + 3 * qk_w + d,
                   h, mods, 5, seq)
        h = _ffn(h, norm3_w[l], mods, 6, 7, 8, w2_in, w2_out, seq)[0]
    return h.reshape(bsz, seq, d)
```

```python
import functools
import math

import jax
import jax.numpy as jnp
from jax import lax
from jax.experimental import pallas as pl
from jax.experimental.pallas import tpu as pltpu

F32 = jnp.float32
BF16 = jnp.bfloat16

EPS = 1e-6
ROPE_THETA = 500000.0
LANES = 128
VMEM_LIMIT = 56 * 1024 * 1024

_NT = (((1,), (1,)), ((), ()))
_TN = (((0,), (0,)), ((), ()))


def _params(*sem):
    return pltpu.CompilerParams(dimension_semantics=sem, vmem_limit_bytes=VMEM_LIMIT)


def _dot(a, b):
    return jnp.dot(a, b, preferred_element_type=F32)


def _sigmoid(x):
    return 0.5 * jnp.tanh(0.5 * x) + 0.5


def _norm_mod(h, nw, sh, sc):
    var = jnp.mean(h * h, axis=-1, keepdims=True)
    return (h * lax.rsqrt(var + EPS) * nw) * (1.0 + sc) + sh


def _with_casts(kernel_fn, n_in, n_out, n_cast):
    def wrapped(*refs):
        ins = refs[:n_in]
        cast_in = refs[n_in:n_in + n_cast]
        outs = refs[n_in + n_cast:n_in + n_cast + n_out]
        cast_out = refs[n_in + n_cast + n_out:n_in + 2 * n_cast + n_out]
        scratch = refs[n_in + 2 * n_cast + n_out:]
        for src, dst in zip(cast_in, cast_out):
            dst[...] = src[...].astype(dst.dtype)
        kernel_fn(*ins, *outs, *scratch)
    return wrapped


def _cast_specs(weights, grid, major_steps=None):
    lead = grid if major_steps is None else grid[:major_steps]
    n_row = math.prod(lead)
    n_col = 1 if major_steps is None else math.prod(grid[major_steps:])

    def index_map(*idx):
        step = 0
        for size, i in zip(lead, idx):
            step = step * size + i
        col = 0
        for size, i in zip(grid[len(lead):], idx[len(lead):]):
            col = col * size + i
        return step, col

    specs, shapes = [], []
    for w in weights:
        r, c = w.shape
        rb, cb = r // n_row, c // n_col
        assert rb * n_row == r and rb % 16 == 0 and cb * n_col == c and cb % LANES == 0
        specs.append(pl.BlockSpec((rb, cb), index_map))
        shapes.append(jax.ShapeDtypeStruct((r, c), BF16))
    return specs, shapes


NORM_ROWS = 16


def _norm_mod_to(h_ref, nw_ref, sh_ref, sc_ref, xn_ref):
    nw = nw_ref[...]
    sh = sh_ref[...]
    sc = sc_ref[...]

    def body(i, carry):
        r = pl.multiple_of(i * NORM_ROWS, NORM_ROWS)
        xn = _norm_mod(h_ref[pl.ds(r, NORM_ROWS), :], nw, sh, sc)
        xn_ref[pl.ds(r, NORM_ROWS), :] = xn.astype(xn_ref.dtype)
        return carry

    lax.fori_loop(0, h_ref.shape[0] // NORM_ROWS, body, 0, unroll=8)


def _ada_kernel(c_ref, w_ref, b_ref, o_ref):
    c = c_ref[...]
    a = (c * _sigmoid(c)).astype(BF16)
    o_ref[...] = _dot(a, w_ref[...].astype(BF16)) + b_ref[...]


def _ada(c, w, b, tn=1024):
    bsz, d = c.shape
    n = w.shape[1]
    rows = 8
    cp = jnp.zeros((rows, d), F32).at[:bsz].set(c)
    out = pl.pallas_call(
        _ada_kernel,
        grid=(n // tn,),
        in_specs=[pl.BlockSpec((rows, d), lambda j: (0, 0)),
                  pl.BlockSpec((d, tn), lambda j: (0, j)),
                  pl.BlockSpec((1, tn), lambda j: (0, j))],
        out_specs=pl.BlockSpec((rows, tn), lambda j: (0, j)),
        out_shape=jax.ShapeDtypeStruct((rows, n), F32),
        compiler_params=_params("parallel"),
        name="adaln",
    )(cp, w, b.reshape(1, n))
    return out[:bsz]


def _ffn_kernel(h_ref, nw_ref, sh_ref, sc_ref, g_ref, wg_ref, wu_ref, wo_ref, o_ref,
                xn_ref, acc_ref):
    f = pl.program_id(1)

    @pl.when(f == 0)
    def _():
        _norm_mod_to(h_ref, nw_ref, sh_ref, sc_ref, xn_ref)
        acc_ref[...] = jnp.zeros_like(acc_ref)

    x = xn_ref[...]
    g = _dot(x, wg_ref[...])
    u = _dot(x, wu_ref[...])
    a = (g * _sigmoid(g)) * u
    acc_ref[...] += _dot(a.astype(BF16), wo_ref[...])

    @pl.when(f == pl.num_programs(1) - 1)
    def _():
        o_ref[...] = h_ref[...] + (0.5 * g_ref[...]) * acc_ref[...]


def _ffn(h, nw, mods, jsh, jsc, jg, w_in, w_out, seq, casts=(), tm=512, tf=512):
    t, d = h.shape
    dff = w_out.shape[0]
    nf = dff // tf
    per_b = seq // tm

    def mod_spec(j):
        return pl.BlockSpec((None, 1, d), lambda m, f: ((m // per_b) * 9 + j, 0, 0))

    cast_specs, cast_shapes = _cast_specs(casts, (t // tm, nf), major_steps=1)
    return pl.pallas_call(
        _with_casts(_ffn_kernel, 8, 1, len(casts)),
        grid=(t // tm, nf),
        in_specs=[pl.BlockSpec((tm, d), lambda m, f: (m, 0)),
                  pl.BlockSpec((1, d), lambda m, f: (0, 0)),
                  mod_spec(jsh), mod_spec(jsc), mod_spec(jg),
                  pl.BlockSpec((d, tf), lambda m, f: (0, f)),
                  pl.BlockSpec((d, tf), lambda m, f: (0, f + nf)),
                  pl.BlockSpec((tf, d), lambda m, f: (f, 0))] + cast_specs,
        out_specs=[pl.BlockSpec((tm, d), lambda m, f: (m, 0))] + cast_specs,
        out_shape=[jax.ShapeDtypeStruct((t, d), F32)] + cast_shapes,
        scratch_shapes=[pltpu.VMEM((tm, d), BF16), pltpu.VMEM((tm, d), F32)],
        compiler_params=_params("parallel", "arbitrary"),
        name="ffn",
    )(h, nw.reshape(1, d), mods, mods, mods, w_in, w_in, w_out, *casts)


def _proj_kernel(h_ref, nw_ref, sh_ref, sc_ref, w_ref, o_ref, xn_ref):
    @pl.when(pl.program_id(1) == 0)
    def _():
        _norm_mod_to(h_ref, nw_ref, sh_ref, sc_ref, xn_ref)

    o_ref[...] = _dot(xn_ref[...], w_ref[...]).astype(o_ref.dtype)


def _proj(h, nw, mods, jsh, jsc, w, seq, tm=1024, tn=1024):
    t, d = h.shape
    n = w.shape[1]
    per_b = seq // tm

    def mod_spec(j):
        return pl.BlockSpec((None, 1, d), lambda m, k: ((m // per_b) * 9 + j, 0, 0))

    return pl.pallas_call(
        _proj_kernel,
        grid=(t // tm, n // tn),
        in_specs=[pl.BlockSpec((tm, d), lambda m, k: (m, 0)),
                  pl.BlockSpec((1, d), lambda m, k: (0, 0)),
                  mod_spec(jsh), mod_spec(jsc),
                  pl.BlockSpec((d, tn), lambda m, k: (0, k))],
        out_specs=pl.BlockSpec((tm, tn), lambda m, k: (m, k)),
        out_shape=jax.ShapeDtypeStruct((t, n), BF16),
        scratch_shapes=[pltpu.VMEM((tm, d), BF16)],
        compiler_params=_params("parallel", "arbitrary"),
        name="mix_proj",
    )(h, nw.reshape(1, d), mods, mods, w)


def _qkv_kernel(q_ref, k_ref, v_ref, pos_ref, frq_ref, gm_ref, qw_ref, kw_ref, sel_ref, one_ref,
                swap_ref, qo_ref, ko_ref, vo_ref, *, n_heads, dh, q_scale):
    ang = frq_ref[...] * pos_ref[...].astype(F32)
    cs = jnp.concatenate([jnp.cos(ang), jnp.sin(ang)], axis=0)
    hi = cs.astype(BF16)
    lo = (cs - hi.astype(F32)).astype(BF16)
    sel = sel_ref[...]
    coef = (lax.dot_general(hi, sel, _TN, preferred_element_type=F32)
            + lax.dot_general(lo, sel, _TN, preferred_element_type=F32))
    c_full = coef[:, 0:LANES] + one_ref[...]
    s_full = coef[:, LANES:2 * LANES]
    gm = gm_ref[...]
    swap = swap_ref[...]

    def prep(x, w, scale):
        ss = _dot((x * x).astype(BF16), gm)
        y = x * lax.rsqrt(ss * (1.0 / dh) + EPS) * w
        y_hi = y.astype(BF16)
        y_lo = (y - y_hi.astype(F32)).astype(BF16)
        partner = _dot(y_hi, swap) + _dot(y_lo, swap)
        return ((y * c_full + partner * s_full) * scale).astype(BF16)

    qw = qw_ref[...]
    kw = kw_ref[...]
    for h in range(n_heads):
        sl = slice(h * LANES, (h + 1) * LANES)
        qo_ref[:, sl] = prep(q_ref[:, sl].astype(F32), qw, q_scale)
        ko_ref[:, sl] = prep(k_ref[:, sl].astype(F32), kw, 1.0)
        vo_ref[sl, :] = v_ref[:, sl].T


def _qkv(proj, positions, frq, gm, qw, kw, sel, one, swap, width, col0, dh, casts=(), tm=512):
    t = proj.shape[0]
    n_heads = width // LANES
    cb = col0 // width
    kern = functools.partial(_qkv_kernel, n_heads=n_heads, dh=dh,
                             q_scale=math.log2(math.e) / math.sqrt(dh))
    const = lambda shape: pl.BlockSpec(shape, lambda m: (0, 0))
    out = jax.ShapeDtypeStruct((t, width), BF16)
    cast_specs, cast_shapes = _cast_specs(casts, (t // tm,))
    return pl.pallas_call(
        _with_casts(kern, 11, 3, len(casts)),
        grid=(t // tm,),
        in_specs=[pl.BlockSpec((tm, width), lambda m: (m, cb)),
                  pl.BlockSpec((tm, width), lambda m: (m, cb + 1)),
                  pl.BlockSpec((tm, width), lambda m: (m, cb + 2)),
                  pl.BlockSpec((1, tm), lambda m: (0, m)),
                  const(frq.shape), const((LANES, LANES)), const((1, LANES)), const((1, LANES)),
                  const(sel.shape), const((1, LANES)), const((LANES, LANES))] + cast_specs,
        out_specs=[pl.BlockSpec((tm, width), lambda m: (m, 0)),
                   pl.BlockSpec((tm, width), lambda m: (m, 0)),
                   pl.BlockSpec((width, tm), lambda m: (0, m))] + cast_specs,
        out_shape=[out, out, jax.ShapeDtypeStruct((width, t), BF16)] + cast_shapes,
        compiler_params=_params("parallel"),
        name="qkv_prep",
    )(proj, proj, proj, positions, frq, gm, qw, kw, sel, one, swap, *casts)


def _boundary_rows(b_ref, b, level, c):
    h = 1 << level
    if h == 1:
        odd = (lax.broadcasted_iota(jnp.int32, b.shape, 0) & 1) == 1
        return jnp.where(odd, pltpu.roll(b, 1, axis=0), b)
    if h == 2:
        sub = lax.broadcasted_iota(jnp.int32, (8, LANES), 0)
        pieces = []
        for j in range(c // 8):
            lo = jnp.broadcast_to(b_ref[8 * j + 1:8 * j + 2, :], (8, LANES))
            hi = jnp.broadcast_to(b_ref[8 * j + 5:8 * j + 6, :], (8, LANES))
            pieces.append(jnp.where(sub < 4, lo, hi))
        return jnp.concatenate(pieces, axis=0)
    blk = max(2 * h, 8)
    pieces = []
    for j in range(c // blk):
        r = (j * blk // (2 * h)) * 2 * h + h - 1
        pieces.append(jnp.broadcast_to(b_ref[r:r + 1, :], (blk, LANES)))
    return jnp.concatenate(pieces, axis=0)


def _hgrn_kernel(q_ref, f_ref, i_ref, g_ref, lb_ref, nw_ref, tri_ref, o_ref, st_ref, b_ref, *,
                 chunk, layer, heads):
    c = chunk
    rows = q_ref.shape[0]
    n_levels = c.bit_length() - 1
    hs = range(heads)

    @pl.when(pl.program_id(2) == 0)
    def _():
        st_ref[...] = jnp.zeros_like(st_ref)

    lbx = lb_ref[...]
    mx = jnp.max(lbx, axis=0, keepdims=True)
    ex = jnp.exp(lbx - mx)
    lb_all = (jnp.sum(ex[0:layer + 1, :], axis=0, keepdims=True)
              / jnp.sum(ex, axis=0, keepdims=True))
    nw = nw_ref[...]

    ti = lax.broadcasted_iota(jnp.int32, (c, c), 0)
    si = lax.broadcasted_iota(jnp.int32, (c, c), 1)
    lvl = jnp.where(ti > si, 31 - lax.clz(ti ^ si), jnp.where(ti == si, -1, -2))
    row = lax.broadcasted_iota(jnp.int32, (c, LANES), 0)
    sgn = [jnp.where(((row >> level) & 1) == 1, 1.0, -1.0) for level in range(n_levels)]

    def body(ci, carry):
        r0 = pl.multiple_of(ci * c, c)
        tri = tri_ref[...]
        sl = [slice(h * LANES, (h + 1) * LANES) for h in hs]
        qb = [q_ref[pl.ds(r0, c), sl[h]] for h in hs]
        vb = [i_ref[pl.ds(r0, c), sl[h]] for h in hs]
        q = [qb[h].astype(F32) for h in hs]
        lf, k = [], []
        for h in hs:
            lb = lb_all[:, sl[h]]
            f = lb + (1.0 - lb) * _sigmoid(f_ref[pl.ds(r0, c), sl[h]].astype(F32))
            lf.append(jnp.log2(f))
            k.append(1.0 - f)
        parts = []
        for h in hs:
            p1 = lf[h].astype(BF16)
            parts += [p1, (lf[h] - p1.astype(F32)).astype(BF16)]
        cs = _dot(tri, jnp.concatenate(parts, axis=1))
        b = []
        for h in hs:
            o2 = 2 * h * LANES
            b.append(cs[:, o2:o2 + LANES] + cs[:, o2 + LANES:o2 + 2 * LANES])
            b_ref[h] = b[h]
        kb = [k[h].astype(BF16) for h in hs]
        inter = []
        for h in hs:
            b_last = b[h][c - 1:c, :]
            st = st_ref[h]
            inter.append(lax.dot_general((q[h] * jnp.exp2(b[h])).astype(BF16), st.astype(BF16),
                                         _NT, preferred_element_type=F32))
            kh = (k[h] * jnp.exp2(b_last - b[h])).astype(BF16)
            st_ref[h] = st * jnp.exp2(b_last) + lax.dot_general(vb[h], kh, _TN,
                                                                preferred_element_type=F32)
        scores = [jnp.where(lvl == -1, jnp.sum(q[h] * k[h], axis=-1, keepdims=True), 0.0)
                  for h in hs]
        for level in range(n_levels):
            for h in hs:
                m = _boundary_rows(b_ref.at[h], b[h], level, c)
                e = jnp.exp2((b[h] - m) * sgn[level]).astype(BF16)
                s_l = lax.dot_general(qb[h] * e, kb[h] * e, _NT, preferred_element_type=F32)
                scores[h] = jnp.where(lvl == level, s_l, scores[h])
        for h in hs:
            o = inter[h] + _dot(scores[h].astype(BF16), vb[h])
            var = jnp.mean(o * o, axis=-1, keepdims=True)
            on = o * lax.rsqrt(var + EPS) * nw
            g = g_ref[pl.ds(r0, c), sl[h]].astype(F32)
            o_ref[pl.ds(r0, c), sl[h]] = (on * (g * _sigmoid(g))).astype(o_ref.dtype)
        return carry

    lax.fori_loop(0, rows // c, body, 0)


def _hgrn(proj, lb, nw, layer, bsz, seq, n_heads, col0, casts=(), rows=512, chunk=128, heads=8):
    t = proj.shape[0]
    kdim = LANES
    width = heads * kdim
    ns = seq // rows
    ng = n_heads // heads
    cb = col0 // width

    def col_spec(j):
        return pl.BlockSpec((rows, width), lambda b, h, s: (b * ns + s, cb + j * ng + h))

    cast_specs, cast_shapes = _cast_specs(casts, (bsz, ng, ns))
    kern = functools.partial(_hgrn_kernel, chunk=chunk, layer=layer, heads=heads)
    return pl.pallas_call(
        _with_casts(kern, 7, 1, len(casts)),
        grid=(bsz, ng, ns),
        in_specs=[col_spec(0), col_spec(1), col_spec(2), col_spec(3),
                  pl.BlockSpec((lb.shape[0], width), lambda b, h, s: (0, h)),
                  pl.BlockSpec((1, kdim), lambda b, h, s: (0, 0)),
                  pl.BlockSpec((chunk, chunk), lambda b, h, s: (0, 0))] + cast_specs,
        out_specs=[pl.BlockSpec((rows, width), lambda b, h, s: (b * ns + s, h))] + cast_specs,
        out_shape=[jax.ShapeDtypeStruct((t, n_heads * kdim), BF16)] + cast_shapes,
        scratch_shapes=[pltpu.VMEM((heads, kdim, kdim), F32),
                        pltpu.VMEM((heads, chunk, kdim), F32)],
        compiler_params=_params("parallel", "parallel", "arbitrary"),
        name="hgrn2",
    )(proj, proj, proj, proj, lb, nw.reshape(1, kdim), jnp.tril(jnp.ones((chunk, chunk), BF16)),
      *casts)


def _attn_kernel(q_ref, k_ref, vt_ref, lam_ref, sw_ref, bias_ref, o_ref, qs_ref, m_ref, l_ref,
                 acc_ref, s_ref, mx_ref, *, tq, tk, dh, lam_init, heads):
    qi = pl.program_id(2)
    groups = 2 * heads
    hl = [slice(hh * LANES, (hh + 1) * LANES) for hh in range(heads)]
    for hh in range(heads):
        q = q_ref[:, hl[hh]]
        lane = lax.broadcasted_iota(jnp.int32, q.shape, 1)
        zero = jnp.zeros_like(q)
        qs_ref[2 * hh * tq:(2 * hh + 1) * tq, :] = jnp.where(lane < dh, q, zero)
        qs_ref[(2 * hh + 1) * tq:(2 * hh + 2) * tq, :] = jnp.where(lane < dh, zero, q)
    m_ref[...] = jnp.full_like(m_ref, -jnp.inf)
    l_ref[...] = jnp.zeros_like(l_ref)
    acc_ref[...] = jnp.zeros_like(acc_ref)

    def scores(j, slot):
        k0 = pl.multiple_of(j * tk, tk)
        kb = [k_ref[pl.ds(k0, tk), hl[hh]] for hh in range(heads)]
        for g in range(groups):
            cols = slice(g * tq, (g + 1) * tq)
            st = lax.dot_general(kb[g // 2], qs_ref[cols, :], _NT, preferred_element_type=F32)
            s_ref[slot, :, cols] = st
            mx_ref[slot, :, cols] = jnp.max(st, axis=0, keepdims=True)

    def step(j, slot, masked):
        k0 = pl.multiple_of(j * tk, tk)
        vts = [vt_ref[hl[hh], pl.ds(k0, tk)] for hh in range(heads)]
        for g in range(groups):
            cols = slice(g * tq, (g + 1) * tq)
            vt = vts[g // 2]
            st = s_ref[slot, :, cols]
            if masked:
                st = st + bias_ref[...]
                m_cur = jnp.max(st, axis=0, keepdims=True)
            else:
                m_cur = mx_ref[slot, :, cols]
            m_old = m_ref[:, cols]
            m_new = jnp.maximum(m_old, m_cur)
            alpha = jnp.exp2(m_old - m_new)
            pt = jnp.exp2(st - m_new)
            l_ref[:, cols] = alpha * l_ref[:, cols] + jnp.sum(pt, axis=0, keepdims=True)
            acc_ref[:, cols] = alpha * acc_ref[:, cols] + _dot(vt, pt.astype(BF16))
            m_ref[:, cols] = m_new

    n_full = (qi * tq) // tk
    odd = n_full % 2

    @pl.when(odd == 0)
    def _():
        scores(0, 0)

    @pl.when(odd == 1)
    def _():
        scores(0, 1)
        scores(1, 0)
        step(0, 1, False)

    def pair(i, carry):
        j = odd + 2 * i
        scores(j + 1, 1)
        step(j, 0, False)
        scores(j + 2, 0)
        step(j + 1, 1, False)
        return carry

    lax.fori_loop(0, n_full // 2, pair, 0)
    step(n_full, 0, True)

    lv = lam_ref[...]
    lam = (jnp.exp(jnp.sum(lv[0:1, :] * lv[1:2, :], axis=-1, keepdims=True))
           - jnp.exp(jnp.sum(lv[2:3, :] * lv[3:4, :], axis=-1, keepdims=True)) + lam_init)
    ot = acc_ref[...] / l_ref[...]
    for hh in range(heads):
        c0 = 2 * hh * tq
        o = (ot[:, c0:c0 + tq] - lam * ot[:, c0 + tq:c0 + 2 * tq]).T
        var = jnp.mean(o * o, axis=-1, keepdims=True)
        o_ref[:, hl[hh]] = (o * lax.rsqrt(var + EPS) * sw_ref[...]
                            * (1.0 - lam_init)).astype(o_ref.dtype)


def _attn(q, k, vt, lamv, sw, bsz, seq, n_heads, dh, lam_init, tq=512, tk=512, heads=2):
    t = q.shape[0]
    nq = seq // tq
    assert tq == tk
    kern = functools.partial(_attn_kernel, tq=tq, tk=tk, dh=dh, lam_init=lam_init, heads=heads)
    key = jnp.arange(tk)[:, None]
    qry = jnp.arange(tq)[None, :]
    bias = jnp.where(key <= qry, 0.0, -jnp.inf).astype(F32)
    width = heads * LANES
    cols = 2 * heads * tq
    return pl.pallas_call(
        kern,
        grid=(bsz, n_heads // heads, nq),
        in_specs=[pl.BlockSpec((tq, width), lambda b, h, i: (b * nq + i, h)),
                  pl.BlockSpec((seq, width), lambda b, h, i: (b, h)),
                  pl.BlockSpec((width, seq), lambda b, h, i: (h, b)),
                  pl.BlockSpec(lamv.shape, lambda b, h, i: (0, 0)),
                  pl.BlockSpec((1, LANES), lambda b, h, i: (0, 0)),
                  pl.BlockSpec((tk, tq), lambda b, h, i: (0, 0))],
        out_specs=pl.BlockSpec((tq, width), lambda b, h, i: (b * nq + i, h)),
        out_shape=jax.ShapeDtypeStruct((t, n_heads * LANES), BF16),
        scratch_shapes=[pltpu.VMEM((cols, LANES), BF16),
                        pltpu.VMEM((1, cols), F32),
                        pltpu.VMEM((1, cols), F32),
                        pltpu.VMEM((LANES, cols), F32),
                        pltpu.VMEM((2, tk, cols), F32),
                        pltpu.VMEM((2, 1, cols), F32)],
        compiler_params=_params("parallel", "parallel", "arbitrary"),
        name="diff_attn",
    )(q, k, vt, lamv, sw.reshape(1, LANES), bias)


def _merge_kernel(oa_ref, ob_ref, wa_ref, wb_ref, wo_ref, *rest, n_slabs):
    ga_refs = rest[:n_slabs]
    gb_refs = rest[n_slabs:2 * n_slabs]
    h_ref, g_ref, o_ref, mix_ref = rest[2 * n_slabs:]
    oa = oa_ref[...]
    ob = ob_ref[...]
    tn = ga_refs[0].shape[1]
    for j in range(n_slabs):
        cols = slice(j * tn, (j + 1) * tn)
        ya = _dot(oa, wa_ref[:, cols])
        yb = _dot(ob, wb_ref[:, cols])
        ga = ga_refs[j][...].astype(F32)
        gb = gb_refs[j][...].astype(F32)
        mix_ref[:, cols] = (_sigmoid(ga) * ya + _sigmoid(gb) * yb).astype(mix_ref.dtype)
    o_ref[...] = h_ref[...] + g_ref[...] * _dot(mix_ref[...], wo_ref[...])


def _merge(oa, ob, wa, wb, wo, proj, col_ga, col_gb, h, mods, jg, seq, tm=512, tn=1024):
    t, kdim = oa.shape
    d = wa.shape[1]
    n_slabs = d // tn
    per_b = seq // tm
    resident = lambda shape: pl.BlockSpec(shape, lambda m: (0, 0), pipeline_mode=pl.Buffered(1))
    gate = lambda col, j: pl.BlockSpec((tm, tn), lambda m: (m, col // tn + j))
    return pl.pallas_call(
        functools.partial(_merge_kernel, n_slabs=n_slabs),
        grid=(t // tm,),
        in_specs=([pl.BlockSpec((tm, kdim), lambda m: (m, 0)),
                   pl.BlockSpec((tm, kdim), lambda m: (m, 0)),
                   resident((kdim, d)), resident((kdim, d)), resident((d, d))]
                  + [gate(col_ga, j) for j in range(n_slabs)]
                  + [gate(col_gb, j) for j in range(n_slabs)]
                  + [pl.BlockSpec((tm, d), lambda m: (m, 0)),
                     pl.BlockSpec((None, 1, d), lambda m: ((m // per_b) * 9 + jg, 0, 0))]),
        out_specs=pl.BlockSpec((tm, d), lambda m: (m, 0)),
        out_shape=jax.ShapeDtypeStruct((t, d), F32),
        scratch_shapes=[pltpu.VMEM((tm, d), BF16)],
        compiler_params=_params("parallel"),
        name="merge_out_proj",
    )(oa, ob, wa, wb, wo, *([proj] * (2 * n_slabs)), h, mods)


def kernel(x, c, positions, ada_w, ada_b, norm1_w, ffn1_w_in, ffn1_w_out, norm2_w, mix_w_in, hgrn_lb, hgrn_norm_w, hgrn_w_out, diff_q_norm_w, diff_k_norm_w, diff_lq1, diff_lk1, diff_lq2, diff_lk2, diff_subln_w, diff_w_out, mix_w_o, norm3_w, ffn2_w_in, ffn2_w_out):
    bsz, seq, d = x.shape
    depth = ada_w.shape[0]
    t = bsz * seq
    kdim = hgrn_norm_w.shape[-1]
    fw = hgrn_lb.shape[-1]
    n_hgrn = fw // kdim
    dh = diff_q_norm_w.shape[-1]
    vdim = diff_subln_w.shape[-1]
    qk_w = diff_w_out.shape[1]
    n_diff = qk_w // vdim
    rope_dim = dh // 4
    half = rope_dim // 2
    assert kdim == LANES and vdim == LANES and 2 * dh == LANES

    frq = (ROPE_THETA ** (-jnp.arange(half, dtype=F32) / half)).reshape(half, 1)
    lane = jnp.arange(LANES)
    dpos = lane % dh
    rot_l = (dpos < rope_dim)[None, :]
    hit = (jnp.arange(half)[:, None] == (dpos % half)[None, :]) & rot_l
    zero = jnp.zeros((half, LANES), F32)
    sel = jnp.concatenate([jnp.concatenate([hit.astype(F32), zero], axis=1),
                           jnp.concatenate([zero, hit.astype(F32)], axis=1)], axis=0).astype(BF16)
    one = jnp.where(dpos < rope_dim, 0.0, 1.0).astype(F32).reshape(1, LANES)
    src, dst = lane[:, None], lane[None, :]
    swap = (jnp.where((dpos < half)[None, :] & (src == dst + half), -1.0, 0.0)
            + jnp.where(((dpos >= half) & (dpos < rope_dim))[None, :] & (src == dst - half),
                        1.0, 0.0)).astype(BF16)
    gm = (lane[:, None] // dh == lane[None, :] // dh).astype(BF16)
    pos2 = positions.reshape(1, t)

    h = x.reshape(t, d)
    for l in range(depth):
        lam_init = 0.8 - 0.6 * math.exp(-0.3 * l)
        mods = _ada(c, ada_w[l], ada_b[l]).reshape(bsz * 9, 1, d)
        h, w_mix = _ffn(h, norm1_w[l], mods, 0, 1, 2, ffn1_w_in[l].astype(BF16),
                        ffn1_w_out[l].astype(BF16), seq, casts=(mix_w_in[l],))
        proj = _proj(h, norm2_w[l], mods, 3, 4, w_mix, seq)
        o_a, w2_in, w2_out = _hgrn(proj, hgrn_lb, hgrn_norm_w[l], l, bsz, seq, n_hgrn, 0,
                                   casts=(ffn2_w_in[l], ffn2_w_out[l]))
        qw = jnp.tile(diff_q_norm_w[l], LANES // dh).reshape(1, LANES)
        kw = jnp.tile(diff_k_norm_w[l], LANES // dh).reshape(1, LANES)
        qr, kr, vr, w_a, w_b, w_o = _qkv(proj, pos2, frq, gm, qw, kw, sel, one, swap, qk_w,
                                         4 * fw, dh,
                                         casts=(hgrn_w_out[l], diff_w_out[l], mix_w_o[l]))
        lamv = jnp.stack([diff_lq1[l], diff_lk1[l], diff_lq2[l], diff_lk2[l]])
        o_b = _attn(qr, kr, vr, lamv, diff_subln_w[l], bsz, seq, n_diff, dh, lam_init)
        h = _merge(o_a, o_b, w_a, w_b, w_o, proj, 4 * fw + 3 * qk_w, 4 * fw + 3 * qk_w + d,
                   h, mods, 5, seq)
        h = _ffn(h, norm3_w[l], mods, 6, 7, 8, w2_in, w2_out, seq)[0]
    return h.reshape(bsz, seq, d)
```

```python
import functools
import math

import jax
import jax.numpy as jnp
from jax import lax
from jax.experimental import pallas as pl
from jax.experimental.pallas import tpu as pltpu

F32 = jnp.float32
BF16 = jnp.bfloat16

EPS = 1e-6
ROPE_THETA = 500000.0
LANES = 128
VMEM_LIMIT = 56 * 1024 * 1024

_NT = (((1,), (1,)), ((), ()))
_TN = (((0,), (0,)), ((), ()))


def _params(*sem):
    return pltpu.CompilerParams(dimension_semantics=sem, vmem_limit_bytes=VMEM_LIMIT)


def _dot(a, b):
    return jnp.dot(a, b, preferred_element_type=F32)


def _sigmoid(x):
    return 0.5 * jnp.tanh(0.5 * x) + 0.5


def _norm_mod(h, nw, sh, sc):
    var = jnp.mean(h * h, axis=-1, keepdims=True)
    return (h * lax.rsqrt(var + EPS) * nw) * (1.0 + sc) + sh


def _with_casts(kernel_fn, n_in, n_out, n_cast):
    def wrapped(*refs):
        ins = refs[:n_in]
        cast_in = refs[n_in:n_in + n_cast]
        outs = refs[n_in + n_cast:n_in + n_cast + n_out]
        cast_out = refs[n_in + n_cast + n_out:n_in + 2 * n_cast + n_out]
        scratch = refs[n_in + 2 * n_cast + n_out:]
        for src, dst in zip(cast_in, cast_out):
            dst[...] = src[...].astype(dst.dtype)
        kernel_fn(*ins, *outs, *scratch)
    return wrapped


def _cast_specs(weights, grid, major_steps=None):
    lead = grid if major_steps is None else grid[:major_steps]
    n_row = math.prod(lead)
    n_col = 1 if major_steps is None else math.prod(grid[major_steps:])

    def index_map(*idx):
        step = 0
        for size, i in zip(lead, idx):
            step = step * size + i
        col = 0
        for size, i in zip(grid[len(lead):], idx[len(lead):]):
            col = col * size + i
        return step, col

    specs, shapes = [], []
    for w in weights:
        r, c = w.shape
        rb, cb = r // n_row, c // n_col
        assert rb * n_row == r and rb % 16 == 0 and cb * n_col == c and cb % LANES == 0
        specs.append(pl.BlockSpec((rb, cb), index_map))
        shapes.append(jax.ShapeDtypeStruct((r, c), BF16))
    return specs, shapes


NORM_ROWS = 16


def _norm_mod_to(h_ref, nw_ref, sh_ref, sc_ref, xn_ref):
    nw = nw_ref[...]
    sh = sh_ref[...]
    sc = sc_ref[...]

    def body(i, carry):
        r = pl.multiple_of(i * NORM_ROWS, NORM_ROWS)
        xn = _norm_mod(h_ref[pl.ds(r, NORM_ROWS), :], nw, sh, sc)
        xn_ref[pl.ds(r, NORM_ROWS), :] = xn.astype(xn_ref.dtype)
        return carry

    lax.fori_loop(0, h_ref.shape[0] // NORM_ROWS, body, 0, unroll=8)


def _ada_kernel(c_ref, w_ref, b_ref, o_ref):
    c = c_ref[...]
    a = (c * _sigmoid(c)).astype(BF16)
    o_ref[...] = _dot(a, w_ref[...].astype(BF16)) + b_ref[...]


def _ada(c, w, b, tn=1024):
    bsz, d = c.shape
    n = w.shape[1]
    rows = 8
    cp = jnp.zeros((rows, d), F32).at[:bsz].set(c)
    out = pl.pallas_call(
        _ada_kernel,
        grid=(n // tn,),
        in_specs=[pl.BlockSpec((rows, d), lambda j: (0, 0)),
                  pl.BlockSpec((d, tn), lambda j: (0, j)),
                  pl.BlockSpec((1, tn), lambda j: (0, j))],
        out_specs=pl.BlockSpec((rows, tn), lambda j: (0, j)),
        out_shape=jax.ShapeDtypeStruct((rows, n), F32),
        compiler_params=_params("parallel"),
        name="adaln",
    )(cp, w, b.reshape(1, n))
    return out[:bsz]


def _ffn_kernel(h_ref, nw_ref, sh_ref, sc_ref, g_ref, wg_ref, wu_ref, wo_ref, o_ref,
                xn_ref, acc_ref):
    f = pl.program_id(1)

    @pl.when(f == 0)
    def _():
        _norm_mod_to(h_ref, nw_ref, sh_ref, sc_ref, xn_ref)
        acc_ref[...] = jnp.zeros_like(acc_ref)

    x = xn_ref[...]
    g = _dot(x, wg_ref[...])
    u = _dot(x, wu_ref[...])
    a = (g * _sigmoid(g)) * u
    acc_ref[...] += _dot(a.astype(BF16), wo_ref[...])

    @pl.when(f == pl.num_programs(1) - 1)
    def _():
        o_ref[...] = h_ref[...] + (0.5 * g_ref[...]) * acc_ref[...]


def _ffn(h, nw, mods, jsh, jsc, jg, w_in, w_out, seq, casts=(), tm=512, tf=512):
    t, d = h.shape
    dff = w_out.shape[0]
    nf = dff // tf
    per_b = seq // tm

    def mod_spec(j):
        return pl.BlockSpec((None, 1, d), lambda m, f: ((m // per_b) * 9 + j, 0, 0))

    cast_specs, cast_shapes = _cast_specs(casts, (t // tm, nf), major_steps=1)
    return pl.pallas_call(
        _with_casts(_ffn_kernel, 8, 1, len(casts)),
        grid=(t // tm, nf),
        in_specs=[pl.BlockSpec((tm, d), lambda m, f: (m, 0)),
                  pl.BlockSpec((1, d), lambda m, f: (0, 0)),
                  mod_spec(jsh), mod_spec(jsc), mod_spec(jg),
                  pl.BlockSpec((d, tf), lambda m, f: (0, f)),
                  pl.BlockSpec((d, tf), lambda m, f: (0, f + nf)),
                  pl.BlockSpec((tf, d), lambda m, f: (f, 0))] + cast_specs,
        out_specs=[pl.BlockSpec((tm, d), lambda m, f: (m, 0))] + cast_specs,
        out_shape=[jax.ShapeDtypeStruct((t, d), F32)] + cast_shapes,
        scratch_shapes=[pltpu.VMEM((tm, d), BF16), pltpu.VMEM((tm, d), F32)],
        compiler_params=_params("parallel", "arbitrary"),
        name="ffn",
    )(h, nw.reshape(1, d), mods, mods, mods, w_in, w_in, w_out, *casts)


def _proj_kernel(h_ref, nw_ref, sh_ref, sc_ref, w_ref, o_ref, xn_ref):
    @pl.when(pl.program_id(1) == 0)
    def _():
        _norm_mod_to(h_ref, nw_ref, sh_ref, sc_ref, xn_ref)

    o_ref[...] = _dot(xn_ref[...], w_ref[...]).astype(o_ref.dtype)


def _proj(h, nw, mods, jsh, jsc, w, seq, tm=1024, tn=1024):
    t, d = h.shape
    n = w.shape[1]
    per_b = seq // tm

    def mod_spec(j):
        return pl.BlockSpec((None, 1, d), lambda m, k: ((m // per_b) * 9 + j, 0, 0))

    return pl.pallas_call(
        _proj_kernel,
        grid=(t // tm, n // tn),
        in_specs=[pl.BlockSpec((tm, d), lambda m, k: (m, 0)),
                  pl.BlockSpec((1, d), lambda m, k: (0, 0)),
                  mod_spec(jsh), mod_spec(jsc),
                  pl.BlockSpec((d, tn), lambda m, k: (0, k))],
        out_specs=pl.BlockSpec((tm, tn), lambda m, k: (m, k)),
        out_shape=jax.ShapeDtypeStruct((t, n), BF16),
        scratch_shapes=[pltpu.VMEM((tm, d), BF16)],
        compiler_params=_params("parallel", "arbitrary"),
        name="mix_proj",
    )(h, nw.reshape(1, d), mods, mods, w)


def _qkv_kernel(q_ref, k_ref, v_ref, pos_ref, frq_ref, gm_ref, qw_ref, kw_ref, sel_ref, one_ref,
                swap_ref, qo_ref, ko_ref, vo_ref, *, n_heads, dh, q_scale):
    ang = frq_ref[...] * pos_ref[...].astype(F32)
    cs = jnp.concatenate([jnp.cos(ang), jnp.sin(ang)], axis=0)
    hi = cs.astype(BF16)
    lo = (cs - hi.astype(F32)).astype(BF16)
    sel = sel_ref[...]
    coef = (lax.dot_general(hi, sel, _TN, preferred_element_type=F32)
            + lax.dot_general(lo, sel, _TN, preferred_element_type=F32))
    c_full = coef[:, 0:LANES] + one_ref[...]
    s_full = coef[:, LANES:2 * LANES]
    gm = gm_ref[...]
    swap = swap_ref[...]

    def prep(x, w, scale):
        ss = _dot((x * x).astype(BF16), gm)
        y = x * lax.rsqrt(ss * (1.0 / dh) + EPS) * w
        y_hi = y.astype(BF16)
        y_lo = (y - y_hi.astype(F32)).astype(BF16)
        partner = _dot(y_hi, swap) + _dot(y_lo, swap)
        return ((y * c_full + partner * s_full) * scale).astype(BF16)

    qw = qw_ref[...]
    kw = kw_ref[...]
    for h in range(n_heads):
        sl = slice(h * LANES, (h + 1) * LANES)
        qo_ref[:, sl] = prep(q_ref[:, sl].astype(F32), qw, q_scale)
        ko_ref[:, sl] = prep(k_ref[:, sl].astype(F32), kw, 1.0)
        vo_ref[sl, :] = v_ref[:, sl].T


def _qkv(proj, positions, frq, gm, qw, kw, sel, one, swap, width, col0, dh, casts=(), tm=512):
    t = proj.shape[0]
    n_heads = width // LANES
    cb = col0 // width
    kern = functools.partial(_qkv_kernel, n_heads=n_heads, dh=dh,
                             q_scale=math.log2(math.e) / math.sqrt(dh))
    const = lambda shape: pl.BlockSpec(shape, lambda m: (0, 0))
    out = jax.ShapeDtypeStruct((t, width), BF16)
    cast_specs, cast_shapes = _cast_specs(casts, (t // tm,))
    return pl.pallas_call(
        _with_casts(kern, 11, 3, len(casts)),
        grid=(t // tm,),
        in_specs=[pl.BlockSpec((tm, width), lambda m: (m, cb)),
                  pl.BlockSpec((tm, width), lambda m: (m, cb + 1)),
                  pl.BlockSpec((tm, width), lambda m: (m, cb + 2)),
                  pl.BlockSpec((1, tm), lambda m: (0, m)),
                  const(frq.shape), const((LANES, LANES)), const((1, LANES)), const((1, LANES)),
                  const(sel.shape), const((1, LANES)), const((LANES, LANES))] + cast_specs,
        out_specs=[pl.BlockSpec((tm, width), lambda m: (m, 0)),
                   pl.BlockSpec((tm, width), lambda m: (m, 0)),
                   pl.BlockSpec((width, tm), lambda m: (0, m))] + cast_specs,
        out_shape=[out, out, jax.ShapeDtypeStruct((width, t), BF16)] + cast_shapes,
        compiler_params=_params("parallel"),
        name="qkv_prep",
    )(proj, proj, proj, positions, frq, gm, qw, kw, sel, one, swap, *casts)


def _boundary_rows(b_ref, b, level, c):
    h = 1 << level
    if h == 1:
        odd = (lax.broadcasted_iota(jnp.int32, b.shape, 0) & 1) == 1
        return jnp.where(odd, pltpu.roll(b, 1, axis=0), b)
    if h == 2:
        sub = lax.broadcasted_iota(jnp.int32, (8, LANES), 0)
        pieces = []
        for j in range(c // 8):
            lo = jnp.broadcast_to(b_ref[8 * j + 1:8 * j + 2, :], (8, LANES))
            hi = jnp.broadcast_to(b_ref[8 * j + 5:8 * j + 6, :], (8, LANES))
            pieces.append(jnp.where(sub < 4, lo, hi))
        return jnp.concatenate(pieces, axis=0)
    blk = max(2 * h, 8)
    pieces = []
    for j in range(c // blk):
        r = (j * blk // (2 * h)) * 2 * h + h - 1
        pieces.append(jnp.broadcast_to(b_ref[r:r + 1, :], (blk, LANES)))
    return jnp.concatenate(pieces, axis=0)


def _hgrn_kernel(q_ref, f_ref, i_ref, g_ref, lb_ref, nw_ref, tri_ref, o_ref, st_ref, b_ref, *,
                 chunk, layer, heads):
    c = chunk
    rows = q_ref.shape[0]
    n_levels = c.bit_length() - 1
    hs = range(heads)

    @pl.when(pl.program_id(2) == 0)
    def _():
        st_ref[...] = jnp.zeros_like(st_ref)

    lbx = lb_ref[...]
    mx = jnp.max(lbx, axis=0, keepdims=True)
    ex = jnp.exp(lbx - mx)
    lb_all = (jnp.sum(ex[0:layer + 1, :], axis=0, keepdims=True)
              / jnp.sum(ex, axis=0, keepdims=True))
    nw = nw_ref[...]

    ti = lax.broadcasted_iota(jnp.int32, (c, c), 0)
    si = lax.broadcasted_iota(jnp.int32, (c, c), 1)
    lvl = jnp.where(ti > si, 31 - lax.clz(ti ^ si), jnp.where(ti == si, -1, -2))
    row = lax.broadcasted_iota(jnp.int32, (c, LANES), 0)
    sgn = [jnp.where(((row >> level) & 1) == 1, 1.0, -1.0) for level in range(n_levels)]

    def body(ci, carry):
        r0 = pl.multiple_of(ci * c, c)
        tri = tri_ref[...]
        sl = [slice(h * LANES, (h + 1) * LANES) for h in hs]
        qb = [q_ref[pl.ds(r0, c), sl[h]] for h in hs]
        vb = [i_ref[pl.ds(r0, c), sl[h]] for h in hs]
        q = [qb[h].astype(F32) for h in hs]
        lf, k = [], []
        for h in hs:
            lb = lb_all[:, sl[h]]
            f = lb + (1.0 - lb) * _sigmoid(f_ref[pl.ds(r0, c), sl[h]].astype(F32))
            lf.append(jnp.log2(f))
            k.append(1.0 - f)
        parts = []
        for h in hs:
            p1 = lf[h].astype(BF16)
            parts += [p1, (lf[h] - p1.astype(F32)).astype(BF16)]
        cs = _dot(tri, jnp.concatenate(parts, axis=1))
        b = []
        for h in hs:
            o2 = 2 * h * LANES
            b.append(cs[:, o2:o2 + LANES] + cs[:, o2 + LANES:o2 + 2 * LANES])
            b_ref[h] = b[h]
        kb = [k[h].astype(BF16) for h in hs]
        inter = []
        for h in hs:
            b_last = b[h][c - 1:c, :]
            st = st_ref[h]
            inter.append(lax.dot_general((q[h] * jnp.exp2(b[h])).astype(BF16), st.astype(BF16),
                                         _NT, preferred_element_type=F32))
            kh = (k[h] * jnp.exp2(b_last - b[h])).astype(BF16)
            st_ref[h] = st * jnp.exp2(b_last) + lax.dot_general(vb[h], kh, _TN,
                                                                preferred_element_type=F32)
        scores = [jnp.where(lvl == -1, jnp.sum(q[h] * k[h], axis=-1, keepdims=True), 0.0)
                  for h in hs]
        for level in range(n_levels):
            for h in hs:
                m = _boundary_rows(b_ref.at[h], b[h], level, c)
                e = jnp.exp2((b[h] - m) * sgn[level]).astype(BF16)
                s_l = lax.dot_general(qb[h] * e, kb[h] * e, _NT, preferred_element_type=F32)
                scores[h] = jnp.where(lvl == level, s_l, scores[h])
        for h in hs:
            o = inter[h] + _dot(scores[h].astype(BF16), vb[h])
            var = jnp.mean(o * o, axis=-1, keepdims=True)
            on = o * lax.rsqrt(var + EPS) * nw
            g = g_ref[pl.ds(r0, c), sl[h]].astype(F32)
            o_ref[pl.ds(r0, c), sl[h]] = (on * (g * _sigmoid(g))).astype(o_ref.dtype)
        return carry

    lax.fori_loop(0, rows // c, body, 0)


def _hgrn(proj, lb, nw, layer, bsz, seq, n_heads, col0, casts=(), rows=512, chunk=128, heads=8):
    t = proj.shape[0]
    kdim = LANES
    width = heads * kdim
    ns = seq // rows
    ng = n_heads // heads
    cb = col0 // width

    def col_spec(j):
        return pl.BlockSpec((rows, width), lambda b, h, s: (b * ns + s, cb + j * ng + h))

    cast_specs, cast_shapes = _cast_specs(casts, (bsz, ng, ns))
    kern = functools.partial(_hgrn_kernel, chunk=chunk, layer=layer, heads=heads)
    return pl.pallas_call(
        _with_casts(kern, 7, 1, len(casts)),
        grid=(bsz, ng, ns),
        in_specs=[col_spec(0), col_spec(1), col_spec(2), col_spec(3),
                  pl.BlockSpec((lb.shape[0], width), lambda b, h, s: (0, h)),
                  pl.BlockSpec((1, kdim), lambda b, h, s: (0, 0)),
                  pl.BlockSpec((chunk, chunk), lambda b, h, s: (0, 0))] + cast_specs,
        out_specs=[pl.BlockSpec((rows, width), lambda b, h, s: (b * ns + s, h))] + cast_specs,
        out_shape=[jax.ShapeDtypeStruct((t, n_heads * kdim), BF16)] + cast_shapes,
        scratch_shapes=[pltpu.VMEM((heads, kdim, kdim), F32),
                        pltpu.VMEM((heads, chunk, kdim), F32)],
        compiler_params=_params("parallel", "parallel", "arbitrary"),
        name="hgrn2",
    )(proj, proj, proj, proj, lb, nw.reshape(1, kdim), jnp.tril(jnp.ones((chunk, chunk), BF16)),
      *casts)


def _attn_kernel(q_ref, qn_ref, k_ref, vt_ref, lam_ref, sw_ref, bias_ref, o_ref, qs_ref, m_ref,
                 l_ref, acc_ref, s_ref, mx_ref, *, tq, tk, dh, lam_init, heads):
    qi = pl.program_id(2)
    groups = 2 * heads
    hl = [slice(hh * LANES, (hh + 1) * LANES) for hh in range(heads)]

    def stack_queries(src_ref):
        for hh in range(heads):
            q = src_ref[:, hl[hh]]
            lane = lax.broadcasted_iota(jnp.int32, q.shape, 1)
            zero = jnp.zeros_like(q)
            qs_ref[2 * hh * tq:(2 * hh + 1) * tq, :] = jnp.where(lane < dh, q, zero)
            qs_ref[(2 * hh + 1) * tq:(2 * hh + 2) * tq, :] = jnp.where(lane < dh, zero, q)

    m_ref[...] = jnp.full_like(m_ref, -jnp.inf)
    l_ref[...] = jnp.zeros_like(l_ref)
    acc_ref[...] = jnp.zeros_like(acc_ref)

    def scores(j, slot):
        k0 = pl.multiple_of(j * tk, tk)
        kb = [k_ref[pl.ds(k0, tk), hl[hh]] for hh in range(heads)]
        for g in range(groups):
            cols = slice(g * tq, (g + 1) * tq)
            st = lax.dot_general(kb[g // 2], qs_ref[cols, :], _NT, preferred_element_type=F32)
            s_ref[slot, :, cols] = st
            mx_ref[slot, :, cols] = jnp.max(st, axis=0, keepdims=True)

    def step(j, slot, masked):
        k0 = pl.multiple_of(j * tk, tk)
        vts = [vt_ref[hl[hh], pl.ds(k0, tk)] for hh in range(heads)]
        for g in range(groups):
            cols = slice(g * tq, (g + 1) * tq)
            vt = vts[g // 2]
            st = s_ref[slot, :, cols]
            if masked:
                st = st + bias_ref[...]
                m_cur = jnp.max(st, axis=0, keepdims=True)
            else:
                m_cur = mx_ref[slot, :, cols]
            m_old = m_ref[:, cols]
            m_new = jnp.maximum(m_old, m_cur)
            alpha = jnp.exp2(m_old - m_new)
            pt = jnp.exp2(st - m_new)
            l_ref[:, cols] = alpha * l_ref[:, cols] + jnp.sum(pt, axis=0, keepdims=True)
            acc_ref[:, cols] = alpha * acc_ref[:, cols] + _dot(vt, pt.astype(BF16))
            m_ref[:, cols] = m_new

    n_full = (qi * tq) // tk
    odd = n_full % 2

    @pl.when(qi == 0)
    def _():
        stack_queries(q_ref)
        scores(0, 0)

    @pl.when(odd == 1)
    def _():
        scores(1, 0)
        step(0, 1, False)

    def pair(i, carry):
        j = odd + 2 * i
        scores(j + 1, 1)
        step(j, 0, False)
        scores(j + 2, 0)
        step(j + 1, 1, False)
        return carry

    lax.fori_loop(0, n_full // 2, pair, 0)
    step(n_full, 0, True)

    stack_queries(qn_ref)
    scores(0, (n_full + 1) % 2)

    lv = lam_ref[...]
    lam = (jnp.exp(jnp.sum(lv[0:1, :] * lv[1:2, :], axis=-1, keepdims=True))
           - jnp.exp(jnp.sum(lv[2:3, :] * lv[3:4, :], axis=-1, keepdims=True)) + lam_init)
    ot = acc_ref[...] / l_ref[...]
    for hh in range(heads):
        c0 = 2 * hh * tq
        o = (ot[:, c0:c0 + tq] - lam * ot[:, c0 + tq:c0 + 2 * tq]).T
        var = jnp.mean(o * o, axis=-1, keepdims=True)
        o_ref[:, hl[hh]] = (o * lax.rsqrt(var + EPS) * sw_ref[...]
                            * (1.0 - lam_init)).astype(o_ref.dtype)


def _attn(q, k, vt, lamv, sw, bsz, seq, n_heads, dh, lam_init, tq=512, tk=512, heads=2):
    t = q.shape[0]
    nq = seq // tq
    assert tq == tk
    kern = functools.partial(_attn_kernel, tq=tq, tk=tk, dh=dh, lam_init=lam_init, heads=heads)
    key = jnp.arange(tk)[:, None]
    qry = jnp.arange(tq)[None, :]
    bias = jnp.where(key <= qry, 0.0, -jnp.inf).astype(F32)
    width = heads * LANES
    cols = 2 * heads * tq
    return pl.pallas_call(
        kern,
        grid=(bsz, n_heads // heads, nq),
        in_specs=[pl.BlockSpec((tq, width), lambda b, h, i: (b * nq + i, h)),
                  pl.BlockSpec((tq, width), lambda b, h, i: (b * nq + jnp.minimum(i + 1, nq - 1), h)),
                  pl.BlockSpec((seq, width), lambda b, h, i: (b, h)),
                  pl.BlockSpec((width, seq), lambda b, h, i: (h, b)),
                  pl.BlockSpec(lamv.shape, lambda b, h, i: (0, 0)),
                  pl.BlockSpec((1, LANES), lambda b, h, i: (0, 0)),
                  pl.BlockSpec((tk, tq), lambda b, h, i: (0, 0))],
        out_specs=pl.BlockSpec((tq, width), lambda b, h, i: (b * nq + i, h)),
        out_shape=jax.ShapeDtypeStruct((t, n_heads * LANES), BF16),
        scratch_shapes=[pltpu.VMEM((cols, LANES), BF16),
                        pltpu.VMEM((1, cols), F32),
                        pltpu.VMEM((1, cols), F32),
                        pltpu.VMEM((LANES, cols), F32),
                        pltpu.VMEM((2, tk, cols), F32),
                        pltpu.VMEM((2, 1, cols), F32)],
        compiler_params=_params("parallel", "parallel", "arbitrary"),
        name="diff_attn",
    )(q, q, k, vt, lamv, sw.reshape(1, LANES), bias)


def _merge_kernel(oa_ref, ob_ref, wa_ref, wb_ref, wo_ref, *rest, n_slabs):
    ga_refs = rest[:n_slabs]
    gb_refs = rest[n_slabs:2 * n_slabs]
    h_ref, g_ref, o_ref, mix_ref = rest[2 * n_slabs:]
    oa = oa_ref[...]
    ob = ob_ref[...]
    tn = ga_refs[0].shape[1]
    for j in range(n_slabs):
        cols = slice(j * tn, (j + 1) * tn)
        ya = _dot(oa, wa_ref[:, cols])
        yb = _dot(ob, wb_ref[:, cols])
        ga = ga_refs[j][...].astype(F32)
        gb = gb_refs[j][...].astype(F32)
        mix_ref[:, cols] = (_sigmoid(ga) * ya + _sigmoid(gb) * yb).astype(mix_ref.dtype)
    o_ref[...] = h_ref[...] + g_ref[...] * _dot(mix_ref[...], wo_ref[...])


def _merge(oa, ob, wa, wb, wo, proj, col_ga, col_gb, h, mods, jg, seq, tm=512, tn=1024):
    t, kdim = oa.shape
    d = wa.shape[1]
    n_slabs = d // tn
    per_b = seq // tm
    resident = lambda shape: pl.BlockSpec(shape, lambda m: (0, 0), pipeline_mode=pl.Buffered(1))
    gate = lambda col, j: pl.BlockSpec((tm, tn), lambda m: (m, col // tn + j))
    return pl.pallas_call(
        functools.partial(_merge_kernel, n_slabs=n_slabs),
        grid=(t // tm,),
        in_specs=([pl.BlockSpec((tm, kdim), lambda m: (m, 0)),
                   pl.BlockSpec((tm, kdim), lambda m: (m, 0)),
                   resident((kdim, d)), resident((kdim, d)), resident((d, d))]
                  + [gate(col_ga, j) for j in range(n_slabs)]
                  + [gate(col_gb, j) for j in range(n_slabs)]
                  + [pl.BlockSpec((tm, d), lambda m: (m, 0)),
                     pl.BlockSpec((None, 1, d), lambda m: ((m // per_b) * 9 + jg, 0, 0))]),
        out_specs=pl.BlockSpec((tm, d), lambda m: (m, 0)),
        out_shape=jax.ShapeDtypeStruct((t, d), F32),
        scratch_shapes=[pltpu.VMEM((tm, d), BF16)],
        compiler_params=_params("parallel"),
        name="merge_out_proj",
    )(oa, ob, wa, wb, wo, *([proj] * (2 * n_slabs)), h, mods)


def kernel(x, c, positions, ada_w, ada_b, norm1_w, ffn1_w_in, ffn1_w_out, norm2_w, mix_w_in, hgrn_lb, hgrn_norm_w, hgrn_w_out, diff_q_norm_w, diff_k_norm_w, diff_lq1, diff_lk1, diff_lq2, diff_lk2, diff_subln_w, diff_w_out, mix_w_o, norm3_w, ffn2_w_in, ffn2_w_out):
    bsz, seq, d = x.shape
    depth = ada_w.shape[0]
    t = bsz * seq
    kdim = hgrn_norm_w.shape[-1]
    fw = hgrn_lb.shape[-1]
    n_hgrn = fw // kdim
    dh = diff_q_norm_w.shape[-1]
    vdim = diff_subln_w.shape[-1]
    qk_w = diff_w_out.shape[1]
    n_diff = qk_w // vdim
    rope_dim = dh // 4
    half = rope_dim // 2
    assert kdim == LANES and vdim == LANES and 2 * dh == LANES

    frq = (ROPE_THETA ** (-jnp.arange(half, dtype=F32) / half)).reshape(half, 1)
    lane = jnp.arange(LANES)
    dpos = lane % dh
    rot_l = (dpos < rope_dim)[None, :]
    hit = (jnp.arange(half)[:, None] == (dpos % half)[None, :]) & rot_l
    zero = jnp.zeros((half, LANES), F32)
    sel = jnp.concatenate([jnp.concatenate([hit.astype(F32), zero], axis=1),
                           jnp.concatenate([zero, hit.astype(F32)], axis=1)], axis=0).astype(BF16)
    one = jnp.where(dpos < rope_dim, 0.0, 1.0).astype(F32).reshape(1, LANES)
    src, dst = lane[:, None], lane[None, :]
    swap = (jnp.where((dpos < half)[None, :] & (src == dst + half), -1.0, 0.0)
            + jnp.where(((dpos >= half) & (dpos < rope_dim))[None, :] & (src == dst - half),
                        1.0, 0.0)).astype(BF16)
    gm = (lane[:, None] // dh == lane[None, :] // dh).astype(BF16)
    pos2 = positions.reshape(1, t)

    h = x.reshape(t, d)
    for l in range(depth):
        lam_init = 0.8 - 0.6 * math.exp(-0.3 * l)
        mods = _ada(c, ada_w[l], ada_b[l]).reshape(bsz * 9, 1, d)
        h, w_mix = _ffn(h, norm1_w[l], mods, 0, 1, 2, ffn1_w_in[l].astype(BF16),
                        ffn1_w_out[l].astype(BF16), seq, casts=(mix_w_in[l],))
        proj = _proj(h, norm2_w[l], mods, 3, 4, w_mix, seq)
        o_a, w2_in, w2_out = _hgrn(proj, hgrn_lb, hgrn_norm_w[l], l, bsz, seq, n_hgrn, 0,
                                   casts=(ffn2_w_in[l], ffn2_w_out[l]))
        qw = jnp.tile(diff_q_norm_w[l], LANES // dh).reshape(1, LANES)
        kw = jnp.tile(diff_k_norm_w[l], LANES // dh).reshape(1, LANES)
        qr, kr, vr, w_a, w_b, w_o = _qkv(proj, pos2, frq, gm, qw, kw, sel, one, swap, qk_w,
                                         4 * fw, dh,
                                         casts=(hgrn_w_out[l], diff_w_out[l], mix_w_o[l]))
        lamv = jnp.stack([diff_lq1[l], diff_lk1[l], diff_lq2[l], diff_lk2[l]])
        o_b = _attn(qr, kr, vr, lamv, diff_subln_w[l], bsz, seq, n_diff, dh, lam_init)
        h = _merge(o_a, o_b, w_a, w_b, w_o, proj, 4 * fw + 3 * qk_w, 4 * fw + 3 * qk_w + d,
                   h, mods, 5, seq)
        h = _ffn(h, norm3_w[l], mods, 6, 7, 8, w2_in, w2_out, seq)[0]
    return h.reshape(bsz, seq, d)
```

```python
import functools
import math

import jax
import jax.numpy as jnp
from jax import lax
from jax.experimental import pallas as pl
from jax.experimental.pallas import tpu as pltpu

F32 = jnp.float32
BF16 = jnp.bfloat16

EPS = 1e-6
ROPE_THETA = 500000.0
LANES = 128
SUBLANES = 8
BF16_ROWS = 2 * SUBLANES
VMEM_LIMIT = 56 * 1024 * 1024
N_MOD = 9

_NT = (((1,), (1,)), ((), ()))
_TN = (((0,), (0,)), ((), ()))


def _params(*sem):
    return pltpu.CompilerParams(dimension_semantics=sem, vmem_limit_bytes=VMEM_LIMIT)


def _dot(a, b):
    return jnp.dot(a, b, preferred_element_type=F32)


def _sigmoid(x):
    return 0.5 * jnp.tanh(0.5 * x) + 0.5


def _norm_mod(h, nw, sh, sc):
    var = jnp.mean(h * h, axis=-1, keepdims=True)
    return (h * lax.rsqrt(var + EPS) * nw) * (1.0 + sc) + sh


def _with_casts(kernel_fn, n_in, n_out, n_cast):
    def wrapped(*refs):
        ins = refs[:n_in]
        cast_in = refs[n_in:n_in + n_cast]
        outs = refs[n_in + n_cast:n_in + n_cast + n_out]
        cast_out = refs[n_in + n_cast + n_out:n_in + 2 * n_cast + n_out]
        scratch = refs[n_in + 2 * n_cast + n_out:]
        for src, dst in zip(cast_in, cast_out):
            dst[...] = src[...].astype(dst.dtype)
        kernel_fn(*ins, *outs, *scratch)
    return wrapped


def _cast_specs(weights, grid, major_steps=None):
    lead = grid if major_steps is None else grid[:major_steps]
    n_row = math.prod(lead)
    n_col = 1 if major_steps is None else math.prod(grid[major_steps:])

    def index_map(*idx):
        step = 0
        for size, i in zip(lead, idx):
            step = step * size + i
        col = 0
        for size, i in zip(grid[len(lead):], idx[len(lead):]):
            col = col * size + i
        return step, col

    specs, shapes = [], []
    for w in weights:
        r, c = w.shape
        rb, cb = r // n_row, c // n_col
        assert rb * n_row == r and rb % BF16_ROWS == 0 and cb * n_col == c and cb % LANES == 0
        specs.append(pl.BlockSpec((rb, cb), index_map))
        shapes.append(jax.ShapeDtypeStruct((r, c), BF16))
    return specs, shapes


NORM_ROWS = BF16_ROWS


def _norm_mod_to(h_ref, nw_ref, sh_ref, sc_ref, xn_ref):
    nw = nw_ref[...]
    sh = sh_ref[...]
    sc = sc_ref[...]

    def body(i, carry):
        r = pl.multiple_of(i * NORM_ROWS, NORM_ROWS)
        xn = _norm_mod(h_ref[pl.ds(r, NORM_ROWS), :], nw, sh, sc)
        xn_ref[pl.ds(r, NORM_ROWS), :] = xn.astype(xn_ref.dtype)
        return carry

    lax.fori_loop(0, h_ref.shape[0] // NORM_ROWS, body, 0, unroll=8)


def _ada_kernel(c_ref, w_ref, b_ref, o_ref):
    c = c_ref[...]
    a = (c * _sigmoid(c)).astype(BF16)
    o_ref[...] = _dot(a, w_ref[...].astype(BF16)) + b_ref[...]


def _ada(c, w, b, tn=1024):
    bsz, d = c.shape
    n = w.shape[1]
    rows = SUBLANES
    assert bsz <= rows
    cp = jnp.zeros((rows, d), F32).at[:bsz].set(c)
    out = pl.pallas_call(
        _ada_kernel,
        grid=(n // tn,),
        in_specs=[pl.BlockSpec((rows, d), lambda j: (0, 0)),
                  pl.BlockSpec((d, tn), lambda j: (0, j)),
                  pl.BlockSpec((1, tn), lambda j: (0, j))],
        out_specs=pl.BlockSpec((rows, tn), lambda j: (0, j)),
        out_shape=jax.ShapeDtypeStruct((rows, n), F32),
        compiler_params=_params("parallel"),
        name="adaln",
    )(cp, w, b.reshape(1, n))
    return out[:bsz]


def _ffn_kernel(h_ref, nw_ref, sh_ref, sc_ref, g_ref, wg_ref, wu_ref, wo_ref, o_ref,
                xn_ref, acc_ref):
    f = pl.program_id(1)

    @pl.when(f == 0)
    def _():
        _norm_mod_to(h_ref, nw_ref, sh_ref, sc_ref, xn_ref)
        acc_ref[...] = jnp.zeros_like(acc_ref)

    x = xn_ref[...]
    g = _dot(x, wg_ref[...])
    u = _dot(x, wu_ref[...])
    a = (g * _sigmoid(g)) * u
    acc_ref[...] += _dot(a.astype(BF16), wo_ref[...])

    @pl.when(f == pl.num_programs(1) - 1)
    def _():
        o_ref[...] = h_ref[...] + (0.5 * g_ref[...]) * acc_ref[...]


def _ffn(h, nw, mods, jsh, jsc, jg, w_in, w_out, seq, casts=(), tm=512, tf=512):
    t, d = h.shape
    dff = w_out.shape[0]
    nf = dff // tf
    per_b = seq // tm

    def mod_spec(j):
        return pl.BlockSpec((None, 1, d), lambda m, f: ((m // per_b) * N_MOD + j, 0, 0))

    cast_specs, cast_shapes = _cast_specs(casts, (t // tm, nf), major_steps=1)
    return pl.pallas_call(
        _with_casts(_ffn_kernel, 8, 1, len(casts)),
        grid=(t // tm, nf),
        in_specs=[pl.BlockSpec((tm, d), lambda m, f: (m, 0)),
                  pl.BlockSpec((1, d), lambda m, f: (0, 0)),
                  mod_spec(jsh), mod_spec(jsc), mod_spec(jg),
                  pl.BlockSpec((d, tf), lambda m, f: (0, f)),
                  pl.BlockSpec((d, tf), lambda m, f: (0, f + nf)),
                  pl.BlockSpec((tf, d), lambda m, f: (f, 0))] + cast_specs,
        out_specs=[pl.BlockSpec((tm, d), lambda m, f: (m, 0))] + cast_specs,
        out_shape=[jax.ShapeDtypeStruct((t, d), F32)] + cast_shapes,
        scratch_shapes=[pltpu.VMEM((tm, d), BF16), pltpu.VMEM((tm, d), F32)],
        compiler_params=_params("parallel", "arbitrary"),
        name="ffn",
    )(h, nw.reshape(1, d), mods, mods, mods, w_in, w_in, w_out, *casts)


def _proj_kernel(h_ref, nw_ref, sh_ref, sc_ref, w_ref, o_ref, xn_ref):
    @pl.when(pl.program_id(1) == 0)
    def _():
        _norm_mod_to(h_ref, nw_ref, sh_ref, sc_ref, xn_ref)

    o_ref[...] = _dot(xn_ref[...], w_ref[...]).astype(o_ref.dtype)


def _proj(h, nw, mods, jsh, jsc, w, seq, tm=1024, tn=1024):
    t, d = h.shape
    n = w.shape[1]
    per_b = seq // tm

    def mod_spec(j):
        return pl.BlockSpec((None, 1, d), lambda m, k: ((m // per_b) * N_MOD + j, 0, 0))

    return pl.pallas_call(
        _proj_kernel,
        grid=(t // tm, n // tn),
        in_specs=[pl.BlockSpec((tm, d), lambda m, k: (m, 0)),
                  pl.BlockSpec((1, d), lambda m, k: (0, 0)),
                  mod_spec(jsh), mod_spec(jsc),
                  pl.BlockSpec((d, tn), lambda m, k: (0, k))],
        out_specs=pl.BlockSpec((tm, tn), lambda m, k: (m, k)),
        out_shape=jax.ShapeDtypeStruct((t, n), BF16),
        scratch_shapes=[pltpu.VMEM((tm, d), BF16)],
        compiler_params=_params("parallel", "arbitrary"),
        name="mix_proj",
    )(h, nw.reshape(1, d), mods, mods, w)


def _qkv_kernel(q_ref, k_ref, v_ref, pos_ref, frq_ref, gm_ref, qw_ref, kw_ref, sel_ref, one_ref,
                swap_ref, qo_ref, ko_ref, vo_ref, *, n_heads, dh, q_scale):
    ang = frq_ref[...] * pos_ref[...].astype(F32)
    cs = jnp.concatenate([jnp.cos(ang), jnp.sin(ang)], axis=0)
    hi = cs.astype(BF16)
    lo = (cs - hi.astype(F32)).astype(BF16)
    sel = sel_ref[...]
    coef = (lax.dot_general(hi, sel, _TN, preferred_element_type=F32)
            + lax.dot_general(lo, sel, _TN, preferred_element_type=F32))
    c_full = coef[:, 0:LANES] + one_ref[...]
    s_full = coef[:, LANES:2 * LANES]
    gm = gm_ref[...]
    swap = swap_ref[...]

    def prep(x, w, scale):
        ss = _dot((x * x).astype(BF16), gm)
        y = x * lax.rsqrt(ss * (1.0 / dh) + EPS) * w
        y_hi = y.astype(BF16)
        y_lo = (y - y_hi.astype(F32)).astype(BF16)
        partner = _dot(y_hi, swap) + _dot(y_lo, swap)
        return ((y * c_full + partner * s_full) * scale).astype(BF16)

    qw = qw_ref[...]
    kw = kw_ref[...]
    for h in range(n_heads):
        sl = slice(h * LANES, (h + 1) * LANES)
        qo_ref[:, sl] = prep(q_ref[:, sl].astype(F32), qw, q_scale)
        ko_ref[:, sl] = prep(k_ref[:, sl].astype(F32), kw, 1.0)
        vo_ref[sl, :] = v_ref[:, sl].T


def _qkv(proj, positions, frq, gm, qw, kw, sel, one, swap, width, col0, dh, casts=(), tm=512):
    t = proj.shape[0]
    n_heads = width // LANES
    cb = col0 // width
    kern = functools.partial(_qkv_kernel, n_heads=n_heads, dh=dh,
                             q_scale=math.log2(math.e) / math.sqrt(dh))
    const = lambda shape: pl.BlockSpec(shape, lambda m: (0, 0))
    out = jax.ShapeDtypeStruct((t, width), BF16)
    cast_specs, cast_shapes = _cast_specs(casts, (t // tm,))
    return pl.pallas_call(
        _with_casts(kern, 11, 3, len(casts)),
        grid=(t // tm,),
        in_specs=[pl.BlockSpec((tm, width), lambda m: (m, cb)),
                  pl.BlockSpec((tm, width), lambda m: (m, cb + 1)),
                  pl.BlockSpec((tm, width), lambda m: (m, cb + 2)),
                  pl.BlockSpec((1, tm), lambda m: (0, m)),
                  const(frq.shape), const((LANES, LANES)), const((1, LANES)), const((1, LANES)),
                  const(sel.shape), const((1, LANES)), const((LANES, LANES))] + cast_specs,
        out_specs=[pl.BlockSpec((tm, width), lambda m: (m, 0)),
                   pl.BlockSpec((tm, width), lambda m: (m, 0)),
                   pl.BlockSpec((width, tm), lambda m: (0, m))] + cast_specs,
        out_shape=[out, out, jax.ShapeDtypeStruct((width, t), BF16)] + cast_shapes,
        compiler_params=_params("parallel"),
        name="qkv_prep",
    )(proj, proj, proj, positions, frq, gm, qw, kw, sel, one, swap, *casts)


def _boundary_rows(b_ref, level, c):
    h = 1 << level
    assert h >= 2
    if h == 2:
        sub = lax.broadcasted_iota(jnp.int32, (SUBLANES, LANES), 0)
        pieces = []
        for j in range(c // SUBLANES):
            r = SUBLANES * j
            lo = jnp.broadcast_to(b_ref[r + 1:r + 2, :], (SUBLANES, LANES))
            hi = jnp.broadcast_to(b_ref[r + 5:r + 6, :], (SUBLANES, LANES))
            pieces.append(jnp.where(sub < 4, lo, hi))
        return jnp.concatenate(pieces, axis=0)
    blk = max(2 * h, SUBLANES)
    pieces = []
    for j in range(c // blk):
        r = (j * blk // (2 * h)) * 2 * h + h - 1
        pieces.append(jnp.broadcast_to(b_ref[r:r + 1, :], (blk, LANES)))
    return jnp.concatenate(pieces, axis=0)


def _hgrn_kernel(q_ref, f_ref, i_ref, g_ref, lb_ref, nw_ref, tri_ref, o_ref, st_ref, b_ref, *,
                 chunk, layer, heads):
    c = chunk
    rows = q_ref.shape[0]
    n_levels = c.bit_length() - 1
    hs = range(heads)

    @pl.when(pl.program_id(2) == 0)
    def _():
        st_ref[...] = jnp.zeros_like(st_ref)

    lbx = lb_ref[...]
    mx = jnp.max(lbx, axis=0, keepdims=True)
    ex = jnp.exp(lbx - mx)
    lb_all = (jnp.sum(ex[0:layer + 1, :], axis=0, keepdims=True)
              / jnp.sum(ex, axis=0, keepdims=True))
    nw = nw_ref[...]

    ti = lax.broadcasted_iota(jnp.int32, (c, c), 0)
    si = lax.broadcasted_iota(jnp.int32, (c, c), 1)
    lvl = jnp.where(ti > si, 31 - lax.clz(ti ^ si), jnp.where(ti == si, -1, -2))
    row = lax.broadcasted_iota(jnp.int32, (c, LANES), 0)
    sgn = [jnp.where(((row >> level) & 1) == 1, 1.0, -1.0) for level in range(n_levels)]

    def body(ci, carry):
        r0 = pl.multiple_of(ci * c, c)
        tri = tri_ref[...]
        sl = [slice(h * LANES, (h + 1) * LANES) for h in hs]
        qb = [q_ref[pl.ds(r0, c), sl[h]] for h in hs]
        vb = [i_ref[pl.ds(r0, c), sl[h]] for h in hs]
        q = [qb[h].astype(F32) for h in hs]
        f, lf, k = [], [], []
        for h in hs:
            lb = lb_all[:, sl[h]]
            f.append(lb + (1.0 - lb) * _sigmoid(f_ref[pl.ds(r0, c), sl[h]].astype(F32)))
            lf.append(jnp.log2(f[h]))
            k.append(1.0 - f[h])
        parts = []
        for h in hs:
            p1 = lf[h].astype(BF16)
            parts += [p1, (lf[h] - p1.astype(F32)).astype(BF16)]
        cs = _dot(tri, jnp.concatenate(parts, axis=1))
        b = []
        for h in hs:
            o2 = 2 * h * LANES
            b.append(cs[:, o2:o2 + LANES] + cs[:, o2 + LANES:o2 + 2 * LANES])
            b_ref[h] = b[h]
        kb = [k[h].astype(BF16) for h in hs]
        inter = []
        for h in hs:
            b_last = b[h][c - 1:c, :]
            st = st_ref[h]
            inter.append(lax.dot_general((q[h] * jnp.exp2(b[h])).astype(BF16), st.astype(BF16),
                                         _NT, preferred_element_type=F32))
            kh = (k[h] * jnp.exp2(b_last - b[h])).astype(BF16)
            st_ref[h] = st * jnp.exp2(b_last) + lax.dot_general(vb[h], kh, _TN,
                                                                preferred_element_type=F32)
        scores = [jnp.where(lvl == -1, jnp.sum(q[h] * k[h], axis=-1, keepdims=True), 0.0)
                  for h in hs]
        for h in hs:
            s_0 = lax.dot_general((q[h] * f[h]).astype(BF16), kb[h], _NT,
                                  preferred_element_type=F32)
            scores[h] = jnp.where(lvl == 0, s_0, scores[h])
        for level in range(1, n_levels):
            for h in hs:
                m = _boundary_rows(b_ref.at[h], level, c)
                e = jnp.exp2((b[h] - m) * sgn[level]).astype(BF16)
                s_l = lax.dot_general(qb[h] * e, kb[h] * e, _NT, preferred_element_type=F32)
                scores[h] = jnp.where(lvl == level, s_l, scores[h])
        for h in hs:
            o = inter[h] + _dot(scores[h].astype(BF16), vb[h])
            var = jnp.mean(o * o, axis=-1, keepdims=True)
            on = o * lax.rsqrt(var + EPS) * nw
            g = g_ref[pl.ds(r0, c), sl[h]].astype(F32)
            o_ref[pl.ds(r0, c), sl[h]] = (on * (g * _sigmoid(g))).astype(o_ref.dtype)
        return carry

    lax.fori_loop(0, rows // c, body, 0)


def _hgrn(proj, lb, nw, layer, bsz, seq, n_heads, col0, casts=(), rows=512, chunk=128, heads=8):
    t = proj.shape[0]
    kdim = LANES
    width = heads * kdim
    ns = seq // rows
    ng = n_heads // heads
    cb = col0 // width

    def col_spec(j):
        return pl.BlockSpec((rows, width), lambda b, h, s: (b * ns + s, cb + j * ng + h))

    cast_specs, cast_shapes = _cast_specs(casts, (bsz, ng, ns))
    kern = functools.partial(_hgrn_kernel, chunk=chunk, layer=layer, heads=heads)
    return pl.pallas_call(
        _with_casts(kern, 7, 1, len(casts)),
        grid=(bsz, ng, ns),
        in_specs=[col_spec(0), col_spec(1), col_spec(2), col_spec(3),
                  pl.BlockSpec((lb.shape[0], width), lambda b, h, s: (0, h)),
                  pl.BlockSpec((1, kdim), lambda b, h, s: (0, 0)),
                  pl.BlockSpec((chunk, chunk), lambda b, h, s: (0, 0))] + cast_specs,
        out_specs=[pl.BlockSpec((rows, width), lambda b, h, s: (b * ns + s, h))] + cast_specs,
        out_shape=[jax.ShapeDtypeStruct((t, n_heads * kdim), BF16)] + cast_shapes,
        scratch_shapes=[pltpu.VMEM((heads, kdim, kdim), F32),
                        pltpu.VMEM((heads, chunk, kdim), F32)],
        compiler_params=_params("parallel", "parallel", "arbitrary"),
        name="hgrn2",
    )(proj, proj, proj, proj, lb, nw.reshape(1, kdim), jnp.tril(jnp.ones((chunk, chunk), BF16)),
      *casts)


def _attn_kernel(q_ref, qn_ref, k_ref, vt_ref, lam_ref, sw_ref, bias_ref, o_ref, qs_ref, m_ref,
                 l_ref, acc_ref, s_ref, mx_ref, *, tq, tk, dh, lam_init, heads):
    qi = pl.program_id(2)
    groups = 2 * heads
    hl = [slice(hh * LANES, (hh + 1) * LANES) for hh in range(heads)]

    def stack_queries(src_ref):
        for hh in range(heads):
            q = src_ref[:, hl[hh]]
            lane = lax.broadcasted_iota(jnp.int32, q.shape, 1)
            zero = jnp.zeros_like(q)
            qs_ref[2 * hh * tq:(2 * hh + 1) * tq, :] = jnp.where(lane < dh, q, zero)
            qs_ref[(2 * hh + 1) * tq:(2 * hh + 2) * tq, :] = jnp.where(lane < dh, zero, q)

    m_ref[...] = jnp.full_like(m_ref, -jnp.inf)
    l_ref[...] = jnp.zeros_like(l_ref)
    acc_ref[...] = jnp.zeros_like(acc_ref)

    def scores(j, slot):
        k0 = pl.multiple_of(j * tk, tk)
        kb = [k_ref[pl.ds(k0, tk), hl[hh]] for hh in range(heads)]
        for g in range(groups):
            cols = slice(g * tq, (g + 1) * tq)
            st = lax.dot_general(kb[g // 2], qs_ref[cols, :], _NT, preferred_element_type=F32)
            s_ref[slot, :, cols] = st
            mx_ref[slot, :, cols] = jnp.max(st, axis=0, keepdims=True)

    def step(j, slot, masked):
        k0 = pl.multiple_of(j * tk, tk)
        vts = [vt_ref[hl[hh], pl.ds(k0, tk)] for hh in range(heads)]
        for g in range(groups):
            cols = slice(g * tq, (g + 1) * tq)
            vt = vts[g // 2]
            st = s_ref[slot, :, cols]
            if masked:
                st = st + bias_ref[...]
                m_cur = jnp.max(st, axis=0, keepdims=True)
            else:
                m_cur = mx_ref[slot, :, cols]
            m_old = m_ref[:, cols]
            m_new = jnp.maximum(m_old, m_cur)
            alpha = jnp.exp2(m_old - m_new)
            pt = jnp.exp2(st - m_new)
            l_ref[:, cols] = alpha * l_ref[:, cols] + jnp.sum(pt, axis=0, keepdims=True)
            acc_ref[:, cols] = alpha * acc_ref[:, cols] + _dot(vt, pt.astype(BF16))
            m_ref[:, cols] = m_new

    n_full = (qi * tq) // tk
    odd = n_full % 2

    @pl.when(qi == 0)
    def _():
        stack_queries(q_ref)
        scores(0, 0)

    @pl.when(odd == 1)
    def _():
        scores(1, 0)
        step(0, 1, False)

    def pair(i, carry):
        j = odd + 2 * i
        scores(j + 1, 1)
        step(j, 0, False)
        scores(j + 2, 0)
        step(j + 1, 1, False)
        return carry

    lax.fori_loop(0, n_full // 2, pair, 0)
    step(n_full, 0, True)

    stack_queries(qn_ref)
    scores(0, (n_full + 1) % 2)

    lv = lam_ref[...]
    lam = (jnp.exp(jnp.sum(lv[0:1, :] * lv[1:2, :], axis=-1, keepdims=True))
           - jnp.exp(jnp.sum(lv[2:3, :] * lv[3:4, :], axis=-1, keepdims=True)) + lam_init)
    ot = acc_ref[...] / l_ref[...]
    for hh in range(heads):
        c0 = 2 * hh * tq
        o = (ot[:, c0:c0 + tq] - lam * ot[:, c0 + tq:c0 + 2 * tq]).T
        var = jnp.mean(o * o, axis=-1, keepdims=True)
        o_ref[:, hl[hh]] = (o * lax.rsqrt(var + EPS) * sw_ref[...]
                            * (1.0 - lam_init)).astype(o_ref.dtype)


def _attn(q, k, vt, lamv, sw, bsz, seq, n_heads, dh, lam_init, tq=512, tk=512, heads=2):
    t = q.shape[0]
    nq = seq // tq
    assert tq == tk
    kern = functools.partial(_attn_kernel, tq=tq, tk=tk, dh=dh, lam_init=lam_init, heads=heads)
    key = jnp.arange(tk)[:, None]
    qry = jnp.arange(tq)[None, :]
    bias = jnp.where(key <= qry, 0.0, -jnp.inf).astype(F32)
    width = heads * LANES
    cols = 2 * heads * tq
    return pl.pallas_call(
        kern,
        grid=(bsz, n_heads // heads, nq),
        in_specs=[pl.BlockSpec((tq, width), lambda b, h, i: (b * nq + i, h)),
                  pl.BlockSpec((tq, width), lambda b, h, i: (b * nq + jnp.minimum(i + 1, nq - 1), h)),
                  pl.BlockSpec((seq, width), lambda b, h, i: (b, h)),
                  pl.BlockSpec((width, seq), lambda b, h, i: (h, b)),
                  pl.BlockSpec(lamv.shape, lambda b, h, i: (0, 0)),
                  pl.BlockSpec((1, LANES), lambda b, h, i: (0, 0)),
                  pl.BlockSpec((tk, tq), lambda b, h, i: (0, 0))],
        out_specs=pl.BlockSpec((tq, width), lambda b, h, i: (b * nq + i, h)),
        out_shape=jax.ShapeDtypeStruct((t, n_heads * LANES), BF16),
        scratch_shapes=[pltpu.VMEM((cols, LANES), BF16),
                        pltpu.VMEM((1, cols), F32),
                        pltpu.VMEM((1, cols), F32),
                        pltpu.VMEM((LANES, cols), F32),
                        pltpu.VMEM((2, tk, cols), F32),
                        pltpu.VMEM((2, 1, cols), F32)],
        compiler_params=_params("parallel", "parallel", "arbitrary"),
        name="diff_attn",
    )(q, q, k, vt, lamv, sw.reshape(1, LANES), bias)


def _merge_kernel(oa_ref, ob_ref, wa_ref, wb_ref, wo_ref, *rest, n_slabs):
    ga_refs = rest[:n_slabs]
    gb_refs = rest[n_slabs:2 * n_slabs]
    h_ref, g_ref, o_ref, mix_ref = rest[2 * n_slabs:]
    oa = oa_ref[...]
    ob = ob_ref[...]
    tn = ga_refs[0].shape[1]
    for j in range(n_slabs):
        cols = slice(j * tn, (j + 1) * tn)
        ya = _dot(oa, wa_ref[:, cols])
        yb = _dot(ob, wb_ref[:, cols])
        ga = ga_refs[j][...].astype(F32)
        gb = gb_refs[j][...].astype(F32)
        mix_ref[:, cols] = (_sigmoid(ga) * ya + _sigmoid(gb) * yb).astype(mix_ref.dtype)
    o_ref[...] = h_ref[...] + g_ref[...] * _dot(mix_ref[...], wo_ref[...])


def _merge(oa, ob, wa, wb, wo, proj, col_ga, col_gb, h, mods, jg, seq, tm=512, tn=1024):
    t, kdim = oa.shape
    d = wa.shape[1]
    n_slabs = d // tn
    per_b = seq // tm
    resident = lambda shape: pl.BlockSpec(shape, lambda m: (0, 0), pipeline_mode=pl.Buffered(1))
    gate = lambda col, j: pl.BlockSpec((tm, tn), lambda m: (m, col // tn + j))
    return pl.pallas_call(
        functools.partial(_merge_kernel, n_slabs=n_slabs),
        grid=(t // tm,),
        in_specs=([pl.BlockSpec((tm, kdim), lambda m: (m, 0)),
                   pl.BlockSpec((tm, kdim), lambda m: (m, 0)),
                   resident((kdim, d)), resident((kdim, d)), resident((d, d))]
                  + [gate(col_ga, j) for j in range(n_slabs)]
                  + [gate(col_gb, j) for j in range(n_slabs)]
                  + [pl.BlockSpec((tm, d), lambda m: (m, 0)),
                     pl.BlockSpec((None, 1, d), lambda m: ((m // per_b) * N_MOD + jg, 0, 0))]),
        out_specs=pl.BlockSpec((tm, d), lambda m: (m, 0)),
        out_shape=jax.ShapeDtypeStruct((t, d), F32),
        scratch_shapes=[pltpu.VMEM((tm, d), BF16)],
        compiler_params=_params("parallel"),
        name="merge_out_proj",
    )(oa, ob, wa, wb, wo, *([proj] * (2 * n_slabs)), h, mods)


def kernel(x, c, positions, ada_w, ada_b, norm1_w, ffn1_w_in, ffn1_w_out, norm2_w, mix_w_in, hgrn_lb, hgrn_norm_w, hgrn_w_out, diff_q_norm_w, diff_k_norm_w, diff_lq1, diff_lk1, diff_lq2, diff_lk2, diff_subln_w, diff_w_out, mix_w_o, norm3_w, ffn2_w_in, ffn2_w_out):
    bsz, seq, d = x.shape
    depth = ada_w.shape[0]
    t = bsz * seq
    kdim = hgrn_norm_w.shape[-1]
    fw = hgrn_lb.shape[-1]
    n_hgrn = fw // kdim
    dh = diff_q_norm_w.shape[-1]
    vdim = diff_subln_w.shape[-1]
    qk_w = diff_w_out.shape[1]
    n_diff = qk_w // vdim
    rope_dim = dh // 4
    half = rope_dim // 2
    assert kdim == LANES and vdim == LANES and 2 * dh == LANES

    frq = (ROPE_THETA ** (-jnp.arange(half, dtype=F32) / half)).reshape(half, 1)
    lane = jnp.arange(LANES)
    dpos = lane % dh
    rot_l = (dpos < rope_dim)[None, :]
    hit = (jnp.arange(half)[:, None] == (dpos % half)[None, :]) & rot_l
    zero = jnp.zeros((half, LANES), F32)
    sel = jnp.concatenate([jnp.concatenate([hit.astype(F32), zero], axis=1),
                           jnp.concatenate([zero, hit.astype(F32)], axis=1)], axis=0).astype(BF16)
    one = jnp.where(dpos < rope_dim, 0.0, 1.0).astype(F32).reshape(1, LANES)
    src, dst = lane[:, None], lane[None, :]
    swap = (jnp.where((dpos < half)[None, :] & (src == dst + half), -1.0, 0.0)
            + jnp.where(((dpos >= half) & (dpos < rope_dim))[None, :] & (src == dst - half),
                        1.0, 0.0)).astype(BF16)
    gm = (lane[:, None] // dh == lane[None, :] // dh).astype(BF16)
    pos2 = positions.reshape(1, t)

    h = x.reshape(t, d)
    for l in range(depth):
        lam_init = 0.8 - 0.6 * math.exp(-0.3 * l)
        assert ada_w.shape[-1] == N_MOD * d
        mods = _ada(c, ada_w[l], ada_b[l]).reshape(bsz * N_MOD, 1, d)
        h, w_mix = _ffn(h, norm1_w[l], mods, 0, 1, 2, ffn1_w_in[l].astype(BF16),
                        ffn1_w_out[l].astype(BF16), seq, casts=(mix_w_in[l],))
        proj = _proj(h, norm2_w[l], mods, 3, 4, w_mix, seq)
        o_a, w2_in, w2_out = _hgrn(proj, hgrn_lb, hgrn_norm_w[l], l, bsz, seq, n_hgrn, 0,
                                   casts=(ffn2_w_in[l], ffn2_w_out[l]))
        qw = jnp.tile(diff_q_norm_w[l], LANES // dh).reshape(1, LANES)
        kw = jnp.tile(diff_k_norm_w[l], LANES // dh).reshape(1, LANES)
        qr, kr, vr, w_a, w_b, w_o = _qkv(proj, pos2, frq, gm, qw, kw, sel, one, swap, qk_w,
                                         4 * fw, dh,
                                         casts=(hgrn_w_out[l], diff_w_out[l], mix_w_o[l]))
        lamv = jnp.stack([diff_lq1[l], diff_lk1[l], diff_lq2[l], diff_lk2[l]])
        o_b = _attn(qr, kr, vr, lamv, diff_subln_w[l], bsz, seq, n_diff, dh, lam_init)
        h = _merge(o_a, o_b, w_a, w_b, w_o, proj, 4 * fw + 3 * qk_w, 4 * fw + 3 * qk_w + d,
                   h, mods, 5, seq)
        h = _ffn(h, norm3_w[l], mods, 6, 7, 8, w2_in, w2_out, seq)[0]
    return h.reshape(bsz, seq, d)
```

```python
import functools
import math

import jax
import jax.numpy as jnp
from jax import lax
from jax.experimental import pallas as pl
from jax.experimental.pallas import tpu as pltpu

F32 = jnp.float32
BF16 = jnp.bfloat16

EPS = 1e-6
ROPE_THETA = 500000.0
LANES = 128
SUBLANES = 8
BF16_ROWS = 2 * SUBLANES
VMEM_LIMIT = 56 * 1024 * 1024
N_MOD = 9

_NT = (((1,), (1,)), ((), ()))
_TN = (((0,), (0,)), ((), ()))


def _params(*sem):
    return pltpu.CompilerParams(dimension_semantics=sem, vmem_limit_bytes=VMEM_LIMIT)


def _dot(a, b):
    return jnp.dot(a, b, preferred_element_type=F32)


def _sigmoid(x):
    return 0.5 * jnp.tanh(0.5 * x) + 0.5


def _norm_mod(h, nw, sh, sc):
    var = jnp.mean(h * h, axis=-1, keepdims=True)
    return (h * lax.rsqrt(var + EPS) * nw) * (1.0 + sc) + sh


def _with_casts(kernel_fn, n_in, n_out, n_cast):
    def wrapped(*refs):
        ins = refs[:n_in]
        cast_in = refs[n_in:n_in + n_cast]
        outs = refs[n_in + n_cast:n_in + n_cast + n_out]
        cast_out = refs[n_in + n_cast + n_out:n_in + 2 * n_cast + n_out]
        scratch = refs[n_in + 2 * n_cast + n_out:]
        for src, dst in zip(cast_in, cast_out):
            dst[...] = src[...].astype(dst.dtype)
        kernel_fn(*ins, *outs, *scratch)
    return wrapped


def _cast_specs(weights, grid, major_steps=None):
    lead = grid if major_steps is None else grid[:major_steps]
    n_row = math.prod(lead)
    n_col = 1 if major_steps is None else math.prod(grid[major_steps:])

    def index_map(*idx):
        step = 0
        for size, i in zip(lead, idx):
            step = step * size + i
        col = 0
        for size, i in zip(grid[len(lead):], idx[len(lead):]):
            col = col * size + i
        return step, col

    specs, shapes = [], []
    for w in weights:
        r, c = w.shape
        rb, cb = r // n_row, c // n_col
        assert rb * n_row == r and rb % BF16_ROWS == 0 and cb * n_col == c and cb % LANES == 0
        specs.append(pl.BlockSpec((rb, cb), index_map))
        shapes.append(jax.ShapeDtypeStruct((r, c), BF16))
    return specs, shapes


NORM_ROWS = BF16_ROWS


def _norm_mod_to(h_ref, nw_ref, sh_ref, sc_ref, xn_ref):
    nw = nw_ref[...]
    sh = sh_ref[...]
    sc = sc_ref[...]

    def body(i, carry):
        r = pl.multiple_of(i * NORM_ROWS, NORM_ROWS)
        xn = _norm_mod(h_ref[pl.ds(r, NORM_ROWS), :], nw, sh, sc)
        xn_ref[pl.ds(r, NORM_ROWS), :] = xn.astype(xn_ref.dtype)
        return carry

    lax.fori_loop(0, h_ref.shape[0] // NORM_ROWS, body, 0, unroll=8)


def _ada_kernel(c_ref, w_ref, b_ref, o_ref):
    c = c_ref[...]
    a = (c * _sigmoid(c)).astype(BF16)
    o_ref[...] = _dot(a, w_ref[...].astype(BF16)) + b_ref[...]


def _ada(c, w, b, tn=1024):
    bsz, d = c.shape
    n = w.shape[1]
    rows = SUBLANES
    assert bsz <= rows
    cp = jnp.zeros((rows, d), F32).at[:bsz].set(c)
    out = pl.pallas_call(
        _ada_kernel,
        grid=(n // tn,),
        in_specs=[pl.BlockSpec((rows, d), lambda j: (0, 0)),
                  pl.BlockSpec((d, tn), lambda j: (0, j)),
                  pl.BlockSpec((1, tn), lambda j: (0, j))],
        out_specs=pl.BlockSpec((rows, tn), lambda j: (0, j)),
        out_shape=jax.ShapeDtypeStruct((rows, n), F32),
        compiler_params=_params("parallel"),
        name="adaln",
    )(cp, w, b.reshape(1, n))
    return out[:bsz]


def _ffn_kernel(h_ref, nw_ref, sh_ref, sc_ref, g_ref, wg_ref, wu_ref, wo_ref, o_ref,
                xn_ref, acc_ref):
    f = pl.program_id(1)

    @pl.when(f == 0)
    def _():
        _norm_mod_to(h_ref, nw_ref, sh_ref, sc_ref, xn_ref)
        acc_ref[...] = jnp.zeros_like(acc_ref)

    x = xn_ref[...]
    g = _dot(x, wg_ref[...])
    u = _dot(x, wu_ref[...])
    a = (g * _sigmoid(g)) * u
    acc_ref[...] += _dot(a.astype(BF16), wo_ref[...])

    @pl.when(f == pl.num_programs(1) - 1)
    def _():
        o_ref[...] = h_ref[...] + (0.5 * g_ref[...]) * acc_ref[...]


def _ffn(h, nw, mods, jsh, jsc, jg, w_in, w_out, seq, casts=(), tm=512, tf=512):
    t, d = h.shape
    dff = w_out.shape[0]
    nf = dff // tf
    per_b = seq // tm

    def mod_spec(j):
        return pl.BlockSpec((None, 1, d), lambda m, f: ((m // per_b) * N_MOD + j, 0, 0))

    cast_specs, cast_shapes = _cast_specs(casts, (t // tm, nf), major_steps=1)
    return pl.pallas_call(
        _with_casts(_ffn_kernel, 8, 1, len(casts)),
        grid=(t // tm, nf),
        in_specs=[pl.BlockSpec((tm, d), lambda m, f: (m, 0)),
                  pl.BlockSpec((1, d), lambda m, f: (0, 0)),
                  mod_spec(jsh), mod_spec(jsc), mod_spec(jg),
                  pl.BlockSpec((d, tf), lambda m, f: (0, f)),
                  pl.BlockSpec((d, tf), lambda m, f: (0, f + nf)),
                  pl.BlockSpec((tf, d), lambda m, f: (f, 0))] + cast_specs,
        out_specs=[pl.BlockSpec((tm, d), lambda m, f: (m, 0))] + cast_specs,
        out_shape=[jax.ShapeDtypeStruct((t, d), F32)] + cast_shapes,
        scratch_shapes=[pltpu.VMEM((tm, d), BF16), pltpu.VMEM((tm, d), F32)],
        compiler_params=_params("parallel", "arbitrary"),
        name="ffn",
    )(h, nw.reshape(1, d), mods, mods, mods, w_in, w_in, w_out, *casts)


def _proj_kernel(h_ref, nw_ref, sh_ref, sc_ref, w_ref, o_ref, xn_ref):
    @pl.when(pl.program_id(1) == 0)
    def _():
        _norm_mod_to(h_ref, nw_ref, sh_ref, sc_ref, xn_ref)

    o_ref[...] = _dot(xn_ref[...], w_ref[...]).astype(o_ref.dtype)


def _proj(h, nw, mods, jsh, jsc, w, seq, tm=1024, tn=1024):
    t, d = h.shape
    n = w.shape[1]
    per_b = seq // tm

    def mod_spec(j):
        return pl.BlockSpec((None, 1, d), lambda m, k: ((m // per_b) * N_MOD + j, 0, 0))

    return pl.pallas_call(
        _proj_kernel,
        grid=(t // tm, n // tn),
        in_specs=[pl.BlockSpec((tm, d), lambda m, k: (m, 0)),
                  pl.BlockSpec((1, d), lambda m, k: (0, 0)),
                  mod_spec(jsh), mod_spec(jsc),
                  pl.BlockSpec((d, tn), lambda m, k: (0, k))],
        out_specs=pl.BlockSpec((tm, tn), lambda m, k: (m, k)),
        out_shape=jax.ShapeDtypeStruct((t, n), BF16),
        scratch_shapes=[pltpu.VMEM((tm, d), BF16)],
        compiler_params=_params("parallel", "arbitrary"),
        name="mix_proj",
    )(h, nw.reshape(1, d), mods, mods, w)


def _qkv_kernel(q_ref, k_ref, v_ref, pos_ref, frq_ref, gm_ref, qw_ref, kw_ref, sel_ref, one_ref,
                swap_ref, qo_ref, ko_ref, vo_ref, *, n_heads, dh, q_scale):
    ang = frq_ref[...] * pos_ref[...].astype(F32)
    cs = jnp.concatenate([jnp.cos(ang), jnp.sin(ang)], axis=0)
    hi = cs.astype(BF16)
    lo = (cs - hi.astype(F32)).astype(BF16)
    sel = sel_ref[...]
    coef = (lax.dot_general(hi, sel, _TN, preferred_element_type=F32)
            + lax.dot_general(lo, sel, _TN, preferred_element_type=F32))
    c_full = coef[:, 0:LANES] + one_ref[...]
    s_full = coef[:, LANES:2 * LANES]
    gm = gm_ref[...]
    swap = swap_ref[...]

    def prep(x, w, scale):
        ss = _dot((x * x).astype(BF16), gm)
        y = x * lax.rsqrt(ss * (1.0 / dh) + EPS) * w
        y_hi = y.astype(BF16)
        y_lo = (y - y_hi.astype(F32)).astype(BF16)
        partner = _dot(y_hi, swap) + _dot(y_lo, swap)
        return ((y * c_full + partner * s_full) * scale).astype(BF16)

    qw = qw_ref[...]
    kw = kw_ref[...]
    for h in range(n_heads):
        sl = slice(h * LANES, (h + 1) * LANES)
        qo_ref[:, sl] = prep(q_ref[:, sl].astype(F32), qw, q_scale)
        ko_ref[:, sl] = prep(k_ref[:, sl].astype(F32), kw, 1.0)
        vo_ref[sl, :] = v_ref[:, sl].T


def _qkv(proj, positions, frq, gm, qw, kw, sel, one, swap, width, col0, dh, casts=(), tm=512):
    t = proj.shape[0]
    n_heads = width // LANES
    cb = col0 // width
    kern = functools.partial(_qkv_kernel, n_heads=n_heads, dh=dh,
                             q_scale=math.log2(math.e) / math.sqrt(dh))
    const = lambda shape: pl.BlockSpec(shape, lambda m: (0, 0))
    out = jax.ShapeDtypeStruct((t, width), BF16)
    cast_specs, cast_shapes = _cast_specs(casts, (t // tm,))
    return pl.pallas_call(
        _with_casts(kern, 11, 3, len(casts)),
        grid=(t // tm,),
        in_specs=[pl.BlockSpec((tm, width), lambda m: (m, cb)),
                  pl.BlockSpec((tm, width), lambda m: (m, cb + 1)),
                  pl.BlockSpec((tm, width), lambda m: (m, cb + 2)),
                  pl.BlockSpec((1, tm), lambda m: (0, m)),
                  const(frq.shape), const((LANES, LANES)), const((1, LANES)), const((1, LANES)),
                  const(sel.shape), const((1, LANES)), const((LANES, LANES))] + cast_specs,
        out_specs=[pl.BlockSpec((tm, width), lambda m: (m, 0)),
                   pl.BlockSpec((tm, width), lambda m: (m, 0)),
                   pl.BlockSpec((width, tm), lambda m: (0, m))] + cast_specs,
        out_shape=[out, out, jax.ShapeDtypeStruct((width, t), BF16)] + cast_shapes,
        compiler_params=_params("parallel"),
        name="qkv_prep",
    )(proj, proj, proj, positions, frq, gm, qw, kw, sel, one, swap, *casts)


def _boundary_rows(b_ref, level, c):
    h = 1 << level
    assert h >= 2
    if h == 2:
        sub = lax.broadcasted_iota(jnp.int32, (SUBLANES, LANES), 0)
        pieces = []
        for j in range(c // SUBLANES):
            r = SUBLANES * j
            lo = jnp.broadcast_to(b_ref[r + 1:r + 2, :], (SUBLANES, LANES))
            hi = jnp.broadcast_to(b_ref[r + 5:r + 6, :], (SUBLANES, LANES))
            pieces.append(jnp.where(sub < 4, lo, hi))
        return jnp.concatenate(pieces, axis=0)
    blk = max(2 * h, SUBLANES)
    pieces = []
    for j in range(c // blk):
        r = (j * blk // (2 * h)) * 2 * h + h - 1
        pieces.append(jnp.broadcast_to(b_ref[r:r + 1, :], (blk, LANES)))
    return jnp.concatenate(pieces, axis=0)


def _hgrn_kernel(q_ref, f_ref, i_ref, g_ref, lb_ref, nw_ref, tri_ref, o_ref, st_ref, b_ref, *,
                 chunk, layer, heads):
    c = chunk
    rows = q_ref.shape[0]
    n_levels = c.bit_length() - 1
    hs = range(heads)

    @pl.when(pl.program_id(2) == 0)
    def _():
        st_ref[...] = jnp.zeros_like(st_ref)

    lbx = lb_ref[...]
    mx = jnp.max(lbx, axis=0, keepdims=True)
    ex = jnp.exp(lbx - mx)
    lb_all = (jnp.sum(ex[0:layer + 1, :], axis=0, keepdims=True)
              / jnp.sum(ex, axis=0, keepdims=True))
    nw = nw_ref[...]

    ti = lax.broadcasted_iota(jnp.int32, (c, c), 0)
    si = lax.broadcasted_iota(jnp.int32, (c, c), 1)
    lvl = jnp.where(ti > si, 31 - lax.clz(ti ^ si), jnp.where(ti == si, -1, -2))
    row = lax.broadcasted_iota(jnp.int32, (c, LANES), 0)
    sgn = [jnp.where(((row >> level) & 1) == 1, 1.0, -1.0) for level in range(n_levels)]

    def body(ci, carry):
        r0 = pl.multiple_of(ci * c, c)
        tri = tri_ref[...]
        sl = [slice(h * LANES, (h + 1) * LANES) for h in hs]
        qb = [q_ref[pl.ds(r0, c), sl[h]] for h in hs]
        vb = [i_ref[pl.ds(r0, c), sl[h]] for h in hs]
        q = [qb[h].astype(F32) for h in hs]
        f, lf, k = [], [], []
        for h in hs:
            lb = lb_all[:, sl[h]]
            f.append(lb + (1.0 - lb) * _sigmoid(f_ref[pl.ds(r0, c), sl[h]].astype(F32)))
            lf.append(jnp.log2(f[h]))
            k.append(1.0 - f[h])
        parts = []
        for h in hs:
            p1 = lf[h].astype(BF16)
            parts += [p1, (lf[h] - p1.astype(F32)).astype(BF16)]
        cs = _dot(tri, jnp.concatenate(parts, axis=1))
        b = []
        for h in hs:
            o2 = 2 * h * LANES
            b.append(cs[:, o2:o2 + LANES] + cs[:, o2 + LANES:o2 + 2 * LANES])
            b_ref[h] = b[h]
        kb = [k[h].astype(BF16) for h in hs]
        inter = []
        for h in hs:
            b_last = b[h][c - 1:c, :]
            st = st_ref[h]
            inter.append(lax.dot_general((q[h] * jnp.exp2(b[h])).astype(BF16), st.astype(BF16),
                                         _NT, preferred_element_type=F32))
            kh = (k[h] * jnp.exp2(b_last - b[h])).astype(BF16)
            st_ref[h] = st * jnp.exp2(b_last) + lax.dot_general(vb[h], kh, _TN,
                                                                preferred_element_type=F32)
        scores = [jnp.where(lvl == -1, jnp.sum(q[h] * k[h], axis=-1, keepdims=True), 0.0)
                  for h in hs]
        for h in hs:
            s_0 = lax.dot_general((q[h] * f[h]).astype(BF16), kb[h], _NT,
                                  preferred_element_type=F32)
            scores[h] = jnp.where(lvl == 0, s_0, scores[h])
        for level in range(1, n_levels):
            for h in hs:
                m = _boundary_rows(b_ref.at[h], level, c)
                e = jnp.exp2((b[h] - m) * sgn[level]).astype(BF16)
                s_l = lax.dot_general(qb[h] * e, kb[h] * e, _NT, preferred_element_type=F32)
                scores[h] = jnp.where(lvl == level, s_l, scores[h])
        for h in hs:
            o = inter[h] + _dot(scores[h].astype(BF16), vb[h])
            var = jnp.mean(o * o, axis=-1, keepdims=True)
            on = o * lax.rsqrt(var + EPS) * nw
            g = g_ref[pl.ds(r0, c), sl[h]].astype(F32)
            o_ref[pl.ds(r0, c), sl[h]] = (on * (g * _sigmoid(g))).astype(o_ref.dtype)
        return carry

    lax.fori_loop(0, rows // c, body, 0)


def _hgrn(proj, lb, nw, layer, bsz, seq, n_heads, col0, casts=(), rows=512, chunk=128, heads=8):
    t = proj.shape[0]
    kdim = LANES
    width = heads * kdim
    ns = seq // rows
    ng = n_heads // heads
    cb = col0 // width

    def col_spec(j):
        return pl.BlockSpec((rows, width), lambda b, h, s: (b * ns + s, cb + j * ng + h))

    cast_specs, cast_shapes = _cast_specs(casts, (bsz, ng, ns))
    kern = functools.partial(_hgrn_kernel, chunk=chunk, layer=layer, heads=heads)
    return pl.pallas_call(
        _with_casts(kern, 7, 1, len(casts)),
        grid=(bsz, ng, ns),
        in_specs=[col_spec(0), col_spec(1), col_spec(2), col_spec(3),
                  pl.BlockSpec((lb.shape[0], width), lambda b, h, s: (0, h)),
                  pl.BlockSpec((1, kdim), lambda b, h, s: (0, 0)),
                  pl.BlockSpec((chunk, chunk), lambda b, h, s: (0, 0))] + cast_specs,
        out_specs=[pl.BlockSpec((rows, width), lambda b, h, s: (b * ns + s, h))] + cast_specs,
        out_shape=[jax.ShapeDtypeStruct((t, n_heads * kdim), BF16)] + cast_shapes,
        scratch_shapes=[pltpu.VMEM((heads, kdim, kdim), F32),
                        pltpu.VMEM((heads, chunk, kdim), F32)],
        compiler_params=_params("parallel", "parallel", "arbitrary"),
        name="hgrn2",
    )(proj, proj, proj, proj, lb, nw.reshape(1, kdim), jnp.tril(jnp.ones((chunk, chunk), BF16)),
      *casts)


def _attn_kernel(q_ref, qn_ref, k_ref, vt_ref, lam_ref, sw_ref, bias_ref, o_ref, qs_ref, m_ref,
                 l_ref, acc_ref, s_ref, mx_ref, *, tq, tk, dh, lam_init, heads):
    qi = pl.program_id(2)
    groups = 2 * heads
    hl = [slice(hh * LANES, (hh + 1) * LANES) for hh in range(heads)]

    def stack_queries(src_ref):
        for hh in range(heads):
            q = src_ref[:, hl[hh]]
            lane = lax.broadcasted_iota(jnp.int32, q.shape, 1)
            zero = jnp.zeros_like(q)
            qs_ref[2 * hh * tq:(2 * hh + 1) * tq, :] = jnp.where(lane < dh, q, zero)
            qs_ref[(2 * hh + 1) * tq:(2 * hh + 2) * tq, :] = jnp.where(lane < dh, zero, q)

    m_ref[...] = jnp.full_like(m_ref, -jnp.inf)
    l_ref[...] = jnp.zeros_like(l_ref)
    acc_ref[...] = jnp.zeros_like(acc_ref)

    def scores(j, slot):
        k0 = pl.multiple_of(j * tk, tk)
        kb = [k_ref[pl.ds(k0, tk), hl[hh]] for hh in range(heads)]
        for g in range(groups):
            cols = slice(g * tq, (g + 1) * tq)
            st = lax.dot_general(kb[g // 2], qs_ref[cols, :], _NT, preferred_element_type=F32)
            s_ref[slot, :, cols] = st
            mx_ref[slot, :, cols] = jnp.max(st, axis=0, keepdims=True)

    def update(cols, st, m_cur, vt):
        m_old = m_ref[:, cols]
        m_new = jnp.maximum(m_old, m_cur)
        alpha = jnp.exp2(m_old - m_new)
        pt = jnp.exp2(st - m_new)
        l_ref[:, cols] = alpha * l_ref[:, cols] + jnp.sum(pt, axis=0, keepdims=True)
        acc_ref[:, cols] = alpha * acc_ref[:, cols] + _dot(vt, pt.astype(BF16))
        m_ref[:, cols] = m_new

    def step(j, slot, masked):
        k0 = pl.multiple_of(j * tk, tk)
        vts = [vt_ref[hl[hh], pl.ds(k0, tk)] for hh in range(heads)]
        for g in range(groups):
            vt = vts[g // 2]
            if not masked:
                cols = slice(g * tq, (g + 1) * tq)
                update(cols, s_ref[slot, :, cols], mx_ref[slot, :, cols], vt)
                continue
            hq, hk = tq // 2, tk // 2
            lo = slice(g * tq, g * tq + hq)
            st = s_ref[slot, 0:hk, lo] + bias_ref[0:hk, 0:hq]
            update(lo, st, jnp.max(st, axis=0, keepdims=True), vt[:, 0:hk])
            hi = slice(g * tq + hq, (g + 1) * tq)
            st = s_ref[slot, :, hi] + bias_ref[:, hq:tq]
            update(hi, st, jnp.max(st, axis=0, keepdims=True), vt)

    n_full = (qi * tq) // tk
    odd = n_full % 2

    @pl.when(qi == 0)
    def _():
        stack_queries(q_ref)
        scores(0, 0)

    @pl.when(odd == 1)
    def _():
        scores(1, 0)
        step(0, 1, False)

    def pair(i, carry):
        j = odd + 2 * i
        scores(j + 1, 1)
        step(j, 0, False)
        scores(j + 2, 0)
        step(j + 1, 1, False)
        return carry

    lax.fori_loop(0, n_full // 2, pair, 0)
    step(n_full, 0, True)

    stack_queries(qn_ref)
    scores(0, (n_full + 1) % 2)

    lv = lam_ref[...]
    lam = (jnp.exp(jnp.sum(lv[0:1, :] * lv[1:2, :], axis=-1, keepdims=True))
           - jnp.exp(jnp.sum(lv[2:3, :] * lv[3:4, :], axis=-1, keepdims=True)) + lam_init)
    ot = acc_ref[...] / l_ref[...]
    for hh in range(heads):
        c0 = 2 * hh * tq
        o = (ot[:, c0:c0 + tq] - lam * ot[:, c0 + tq:c0 + 2 * tq]).T
        var = jnp.mean(o * o, axis=-1, keepdims=True)
        o_ref[:, hl[hh]] = (o * lax.rsqrt(var + EPS) * sw_ref[...]
                            * (1.0 - lam_init)).astype(o_ref.dtype)


def _attn(q, k, vt, lamv, sw, bsz, seq, n_heads, dh, lam_init, tq=512, tk=512, heads=2):
    t = q.shape[0]
    nq = seq // tq
    assert tq == tk
    kern = functools.partial(_attn_kernel, tq=tq, tk=tk, dh=dh, lam_init=lam_init, heads=heads)
    key = jnp.arange(tk)[:, None]
    qry = jnp.arange(tq)[None, :]
    bias = jnp.where(key <= qry, 0.0, -jnp.inf).astype(F32)
    width = heads * LANES
    cols = 2 * heads * tq
    return pl.pallas_call(
        kern,
        grid=(bsz, n_heads // heads, nq),
        in_specs=[pl.BlockSpec((tq, width), lambda b, h, i: (b * nq + i, h)),
                  pl.BlockSpec((tq, width), lambda b, h, i: (b * nq + jnp.minimum(i + 1, nq - 1), h)),
                  pl.BlockSpec((seq, width), lambda b, h, i: (b, h)),
                  pl.BlockSpec((width, seq), lambda b, h, i: (h, b)),
                  pl.BlockSpec(lamv.shape, lambda b, h, i: (0, 0)),
                  pl.BlockSpec((1, LANES), lambda b, h, i: (0, 0)),
                  pl.BlockSpec((tk, tq), lambda b, h, i: (0, 0))],
        out_specs=pl.BlockSpec((tq, width), lambda b, h, i: (b * nq + i, h)),
        out_shape=jax.ShapeDtypeStruct((t, n_heads * LANES), BF16),
        scratch_shapes=[pltpu.VMEM((cols, LANES), BF16),
                        pltpu.VMEM((1, cols), F32),
                        pltpu.VMEM((1, cols), F32),
                        pltpu.VMEM((LANES, cols), F32),
                        pltpu.VMEM((2, tk, cols), F32),
                        pltpu.VMEM((2, 1, cols), F32)],
        compiler_params=_params("parallel", "parallel", "arbitrary"),
        name="diff_attn",
    )(q, q, k, vt, lamv, sw.reshape(1, LANES), bias)


def _merge_kernel(oa_ref, ob_ref, wa_ref, wb_ref, wo_ref, *rest, n_slabs):
    ga_refs = rest[:n_slabs]
    gb_refs = rest[n_slabs:2 * n_slabs]
    h_ref, g_ref, o_ref, mix_ref = rest[2 * n_slabs:]
    oa = oa_ref[...]
    ob = ob_ref[...]
    tn = ga_refs[0].shape[1]
    for j in range(n_slabs):
        cols = slice(j * tn, (j + 1) * tn)
        ya = _dot(oa, wa_ref[:, cols])
        yb = _dot(ob, wb_ref[:, cols])
        ga = ga_refs[j][...].astype(F32)
        gb = gb_refs[j][...].astype(F32)
        mix_ref[:, cols] = (_sigmoid(ga) * ya + _sigmoid(gb) * yb).astype(mix_ref.dtype)
    o_ref[...] = h_ref[...] + g_ref[...] * _dot(mix_ref[...], wo_ref[...])


def _merge(oa, ob, wa, wb, wo, proj, col_ga, col_gb, h, mods, jg, seq, tm=512, tn=1024):
    t, kdim = oa.shape
    d = wa.shape[1]
    n_slabs = d // tn
    per_b = seq // tm
    resident = lambda shape: pl.BlockSpec(shape, lambda m: (0, 0), pipeline_mode=pl.Buffered(1))
    gate = lambda col, j: pl.BlockSpec((tm, tn), lambda m: (m, col // tn + j))
    return pl.pallas_call(
        functools.partial(_merge_kernel, n_slabs=n_slabs),
        grid=(t // tm,),
        in_specs=([pl.BlockSpec((tm, kdim), lambda m: (m, 0)),
                   pl.BlockSpec((tm, kdim), lambda m: (m, 0)),
                   resident((kdim, d)), resident((kdim, d)), resident((d, d))]
                  + [gate(col_ga, j) for j in range(n_slabs)]
                  + [gate(col_gb, j) for j in range(n_slabs)]
                  + [pl.BlockSpec((tm, d), lambda m: (m, 0)),
                     pl.BlockSpec((None, 1, d), lambda m: ((m // per_b) * N_MOD + jg, 0, 0))]),
        out_specs=pl.BlockSpec((tm, d), lambda m: (m, 0)),
        out_shape=jax.ShapeDtypeStruct((t, d), F32),
        scratch_shapes=[pltpu.VMEM((tm, d), BF16)],
        compiler_params=_params("parallel"),
        name="merge_out_proj",
    )(oa, ob, wa, wb, wo, *([proj] * (2 * n_slabs)), h, mods)


def kernel(x, c, positions, ada_w, ada_b, norm1_w, ffn1_w_in, ffn1_w_out, norm2_w, mix_w_in, hgrn_lb, hgrn_norm_w, hgrn_w_out, diff_q_norm_w, diff_k_norm_w, diff_lq1, diff_lk1, diff_lq2, diff_lk2, diff_subln_w, diff_w_out, mix_w_o, norm3_w, ffn2_w_in, ffn2_w_out):
    bsz, seq, d = x.shape
    depth = ada_w.shape[0]
    t = bsz * seq
    kdim = hgrn_norm_w.shape[-1]
    fw = hgrn_lb.shape[-1]
    n_hgrn = fw // kdim
    dh = diff_q_norm_w.shape[-1]
    vdim = diff_subln_w.shape[-1]
    qk_w = diff_w_out.shape[1]
    n_diff = qk_w // vdim
    rope_dim = dh // 4
    half = rope_dim // 2
    assert kdim == LANES and vdim == LANES and 2 * dh == LANES

    frq = (ROPE_THETA ** (-jnp.arange(half, dtype=F32) / half)).reshape(half, 1)
    lane = jnp.arange(LANES)
    dpos = lane % dh
    rot_l = (dpos < rope_dim)[None, :]
    hit = (jnp.arange(half)[:, None] == (dpos % half)[None, :]) & rot_l
    zero = jnp.zeros((half, LANES), F32)
    sel = jnp.concatenate([jnp.concatenate([hit.astype(F32), zero], axis=1),
                           jnp.concatenate([zero, hit.astype(F32)], axis=1)], axis=0).astype(BF16)
    one = jnp.where(dpos < rope_dim, 0.0, 1.0).astype(F32).reshape(1, LANES)
    src, dst = lane[:, None], lane[None, :]
    swap = (jnp.where((dpos < half)[None, :] & (src == dst + half), -1.0, 0.0)
            + jnp.where(((dpos >= half) & (dpos < rope_dim))[None, :] & (src == dst - half),
                        1.0, 0.0)).astype(BF16)
    gm = (lane[:, None] // dh == lane[None, :] // dh).astype(BF16)
    pos2 = positions.reshape(1, t)

    h = x.reshape(t, d)
    for l in range(depth):
        lam_init = 0.8 - 0.6 * math.exp(-0.3 * l)
        assert ada_w.shape[-1] == N_MOD * d
        mods = _ada(c, ada_w[l], ada_b[l]).reshape(bsz * N_MOD, 1, d)
        h, w_mix = _ffn(h, norm1_w[l], mods, 0, 1, 2, ffn1_w_in[l].astype(BF16),
                        ffn1_w_out[l].astype(BF16), seq, casts=(mix_w_in[l],))
        proj = _proj(h, norm2_w[l], mods, 3, 4, w_mix, seq)
        o_a, w2_in, w2_out = _hgrn(proj, hgrn_lb, hgrn_norm_w[l], l, bsz, seq, n_hgrn, 0,
                                   casts=(ffn2_w_in[l], ffn2_w_out[l]))
        qw = jnp.tile(diff_q_norm_w[l], LANES // dh).reshape(1, LANES)
        kw = jnp.tile(diff_k_norm_w[l], LANES // dh).reshape(1, LANES)
        qr, kr, vr, w_a, w_b, w_o = _qkv(proj, pos2, frq, gm, qw, kw, sel, one, swap, qk_w,
                                         4 * fw, dh,
                                         casts=(hgrn_w_out[l], diff_w_out[l], mix_w_o[l]))
        lamv = jnp.stack([diff_lq1[l], diff_lk1[l], diff_lq2[l], diff_lk2[l]])
        o_b = _attn(qr, kr, vr, lamv, diff_subln_w[l], bsz, seq, n_diff, dh, lam_init)
        h = _merge(o_a, o_b, w_a, w_b, w_o, proj, 4 * fw + 3 * qk_w, 4 * fw + 3 * qk_w + d,
                   h, mods, 5, seq)
        h = _ffn(h, norm3_w[l], mods, 6, 7, 8, w2_in, w2_out, seq)[0]
    return h.reshape(bsz, seq, d)
```

```python
import functools
import math

import jax
import jax.numpy as jnp
from jax import lax
from jax.experimental import pallas as pl
from jax.experimental.pallas import tpu as pltpu

F32 = jnp.float32
BF16 = jnp.bfloat16

EPS = 1e-6
ROPE_THETA = 500000.0
LANES = 128
SUBLANES = 8
BF16_ROWS = 2 * SUBLANES
VMEM_LIMIT = 56 * 1024 * 1024
N_MOD = 9

_NT = (((1,), (1,)), ((), ()))
_TN = (((0,), (0,)), ((), ()))


def _params(*sem):
    return pltpu.CompilerParams(dimension_semantics=sem, vmem_limit_bytes=VMEM_LIMIT)


def _dot(a, b):
    return jnp.dot(a, b, preferred_element_type=F32)


def _sigmoid(x):
    return 0.5 * jnp.tanh(0.5 * x) + 0.5


def _norm_mod(h, nw, sh, sc):
    var = jnp.mean(h * h, axis=-1, keepdims=True)
    return (h * lax.rsqrt(var + EPS) * nw) * (1.0 + sc) + sh


def _with_casts(kernel_fn, n_in, n_out, n_cast):
    def wrapped(*refs):
        ins = refs[:n_in]
        cast_in = refs[n_in:n_in + n_cast]
        outs = refs[n_in + n_cast:n_in + n_cast + n_out]
        cast_out = refs[n_in + n_cast + n_out:n_in + 2 * n_cast + n_out]
        scratch = refs[n_in + 2 * n_cast + n_out:]
        for src, dst in zip(cast_in, cast_out):
            dst[...] = src[...].astype(dst.dtype)
        kernel_fn(*ins, *outs, *scratch)
    return wrapped


def _cast_specs(weights, grid, major_steps=None):
    lead = grid if major_steps is None else grid[:major_steps]
    n_row = math.prod(lead)
    n_col = 1 if major_steps is None else math.prod(grid[major_steps:])

    def index_map(*idx):
        step = 0
        for size, i in zip(lead, idx):
            step = step * size + i
        col = 0
        for size, i in zip(grid[len(lead):], idx[len(lead):]):
            col = col * size + i
        return step, col

    specs, shapes = [], []
    for w in weights:
        r, c = w.shape
        rb, cb = r // n_row, c // n_col
        assert rb * n_row == r and rb % BF16_ROWS == 0 and cb * n_col == c and cb % LANES == 0
        specs.append(pl.BlockSpec((rb, cb), index_map))
        shapes.append(jax.ShapeDtypeStruct((r, c), BF16))
    return specs, shapes


NORM_ROWS = BF16_ROWS


def _norm_mod_to(h_ref, nw_ref, sh_ref, sc_ref, xn_ref):
    nw = nw_ref[...]
    sh = sh_ref[...]
    sc = sc_ref[...]

    def body(i, carry):
        r = pl.multiple_of(i * NORM_ROWS, NORM_ROWS)
        xn = _norm_mod(h_ref[pl.ds(r, NORM_ROWS), :], nw, sh, sc)
        xn_ref[pl.ds(r, NORM_ROWS), :] = xn.astype(xn_ref.dtype)
        return carry

    lax.fori_loop(0, h_ref.shape[0] // NORM_ROWS, body, 0, unroll=8)


NEXT_CHUNKS = 8


def _norm_mod_next_chunk(step, hn_ref, nw_ref, shn_ref, scn_ref, xnext_ref):
    rows = hn_ref.shape[0]
    base = jnp.minimum(step, NEXT_CHUNKS - 1) * rows
    nw = nw_ref[...]
    sh = shn_ref[...]
    sc = scn_ref[...]
    for i in range(rows // NORM_ROWS):
        src = slice(i * NORM_ROWS, (i + 1) * NORM_ROWS)
        dst = pl.ds(pl.multiple_of(base + i * NORM_ROWS, NORM_ROWS), NORM_ROWS)
        xn = _norm_mod(hn_ref[src, :], nw, sh, sc)
        xnext_ref[dst, :] = xn.astype(xnext_ref.dtype)


def _ada_kernel(c_ref, w_ref, b_ref, o_ref):
    c = c_ref[...]
    a = (c * _sigmoid(c)).astype(BF16)
    o_ref[...] = _dot(a, w_ref[...].astype(BF16)) + b_ref[...]


def _ada(c, w, b, tn=1024):
    bsz, d = c.shape
    n = w.shape[1]
    rows = SUBLANES
    assert bsz <= rows
    cp = jnp.zeros((rows, d), F32).at[:bsz].set(c)
    out = pl.pallas_call(
        _ada_kernel,
        grid=(n // tn,),
        in_specs=[pl.BlockSpec((rows, d), lambda j: (0, 0)),
                  pl.BlockSpec((d, tn), lambda j: (0, j)),
                  pl.BlockSpec((1, tn), lambda j: (0, j))],
        out_specs=pl.BlockSpec((rows, tn), lambda j: (0, j)),
        out_shape=jax.ShapeDtypeStruct((rows, n), F32),
        compiler_params=_params("parallel"),
        name="adaln",
    )(cp, w, b.reshape(1, n))
    return out[:bsz]


def _ffn_kernel(h_ref, hn_ref, nw_ref, sh_ref, sc_ref, shn_ref, scn_ref, g_ref, wg_ref, wu_ref,
                wo_ref, o_ref, xn_ref, xnext_ref, acc_ref):
    m = pl.program_id(0)
    f = pl.program_id(1)

    @pl.when((f == 0) & (m == 0))
    def _():
        _norm_mod_to(h_ref, nw_ref, sh_ref, sc_ref, xn_ref)

    @pl.when((f == 0) & (m > 0))
    def _():
        xn_ref[...] = xnext_ref[...]

    @pl.when(f == 0)
    def _():
        acc_ref[...] = jnp.zeros_like(acc_ref)

    x = xn_ref[...]
    g = _dot(x, wg_ref[...])
    u = _dot(x, wu_ref[...])
    a = (g * _sigmoid(g)) * u
    acc_ref[...] += _dot(a.astype(BF16), wo_ref[...])
    _norm_mod_next_chunk(f, hn_ref, nw_ref, shn_ref, scn_ref, xnext_ref)

    @pl.when(f == pl.num_programs(1) - 1)
    def _():
        o_ref[...] = h_ref[...] + (0.5 * g_ref[...]) * acc_ref[...]


def _ffn(h, nw, mods, jsh, jsc, jg, w_in, w_out, seq, casts=(), tm=512, tf=512):
    t, d = h.shape
    dff = w_out.shape[0]
    nf = dff // tf
    nm = t // tm
    per_b = seq // tm
    assert nf >= NEXT_CHUNKS and tm % (NEXT_CHUNKS * NORM_ROWS) == 0
    nxt = lambda m: jnp.minimum(m + 1, nm - 1)

    def mod_spec(j, tile=lambda m: m):
        return pl.BlockSpec((None, 1, d), lambda m, f: ((tile(m) // per_b) * N_MOD + j, 0, 0))

    cast_specs, cast_shapes = _cast_specs(casts, (nm, nf), major_steps=1)
    return pl.pallas_call(
        _with_casts(_ffn_kernel, 11, 1, len(casts)),
        grid=(nm, nf),
        in_specs=[pl.BlockSpec((tm, d), lambda m, f: (m, 0)),
                  pl.BlockSpec((tm // NEXT_CHUNKS, d), lambda m, f: (
                      nxt(m) * NEXT_CHUNKS + jnp.minimum(f, NEXT_CHUNKS - 1), 0)),
                  pl.BlockSpec((1, d), lambda m, f: (0, 0)),
                  mod_spec(jsh), mod_spec(jsc), mod_spec(jsh, nxt), mod_spec(jsc, nxt),
                  mod_spec(jg),
                  pl.BlockSpec((d, tf), lambda m, f: (0, f)),
                  pl.BlockSpec((d, tf), lambda m, f: (0, f + nf)),
                  pl.BlockSpec((tf, d), lambda m, f: (f, 0))] + cast_specs,
        out_specs=[pl.BlockSpec((tm, d), lambda m, f: (m, 0))] + cast_specs,
        out_shape=[jax.ShapeDtypeStruct((t, d), F32)] + cast_shapes,
        scratch_shapes=[pltpu.VMEM((tm, d), BF16), pltpu.VMEM((tm, d), BF16),
                        pltpu.VMEM((tm, d), F32)],
        compiler_params=_params("arbitrary", "arbitrary"),
        name="ffn",
    )(h, h, nw.reshape(1, d), mods, mods, mods, mods, mods, w_in, w_in, w_out, *casts)


def _proj_kernel(h_ref, hn_ref, nw_ref, sh_ref, sc_ref, shn_ref, scn_ref, w_ref, o_ref,
                 xn_ref, xnext_ref):
    m = pl.program_id(0)
    k = pl.program_id(1)

    @pl.when((k == 0) & (m == 0))
    def _():
        _norm_mod_to(h_ref, nw_ref, sh_ref, sc_ref, xn_ref)

    @pl.when((k == 0) & (m > 0))
    def _():
        xn_ref[...] = xnext_ref[...]

    o_ref[...] = _dot(xn_ref[...], w_ref[...]).astype(o_ref.dtype)
    _norm_mod_next_chunk(k, hn_ref, nw_ref, shn_ref, scn_ref, xnext_ref)


def _proj(h, nw, mods, jsh, jsc, w, seq, tm=1024, tn=1024):
    t, d = h.shape
    n = w.shape[1]
    nm = t // tm
    per_b = seq // tm
    assert n // tn >= NEXT_CHUNKS and tm % (NEXT_CHUNKS * NORM_ROWS) == 0
    nxt = lambda m: jnp.minimum(m + 1, nm - 1)

    def mod_spec(j, tile=lambda m: m):
        return pl.BlockSpec((None, 1, d), lambda m, k: ((tile(m) // per_b) * N_MOD + j, 0, 0))

    return pl.pallas_call(
        _proj_kernel,
        grid=(nm, n // tn),
        in_specs=[pl.BlockSpec((tm, d), lambda m, k: (m, 0)),
                  pl.BlockSpec((tm // NEXT_CHUNKS, d), lambda m, k: (
                      nxt(m) * NEXT_CHUNKS + jnp.minimum(k, NEXT_CHUNKS - 1), 0)),
                  pl.BlockSpec((1, d), lambda m, k: (0, 0)),
                  mod_spec(jsh), mod_spec(jsc), mod_spec(jsh, nxt), mod_spec(jsc, nxt),
                  pl.BlockSpec((d, tn), lambda m, k: (0, k))],
        out_specs=pl.BlockSpec((tm, tn), lambda m, k: (m, k)),
        out_shape=jax.ShapeDtypeStruct((t, n), BF16),
        scratch_shapes=[pltpu.VMEM((tm, d), BF16), pltpu.VMEM((tm, d), BF16)],
        compiler_params=_params("arbitrary", "arbitrary"),
        name="mix_proj",
    )(h, h, nw.reshape(1, d), mods, mods, mods, mods, w)


def _qkv_kernel(q_ref, k_ref, v_ref, pos_ref, frq_ref, gm_ref, qw_ref, kw_ref, sel_ref, one_ref,
                swap_ref, qo_ref, ko_ref, vo_ref, *, n_heads, dh, q_scale):
    ang = frq_ref[...] * pos_ref[...].astype(F32)
    cs = jnp.concatenate([jnp.cos(ang), jnp.sin(ang)], axis=0)
    hi = cs.astype(BF16)
    lo = (cs - hi.astype(F32)).astype(BF16)
    sel = sel_ref[...]
    coef = (lax.dot_general(hi, sel, _TN, preferred_element_type=F32)
            + lax.dot_general(lo, sel, _TN, preferred_element_type=F32))
    c_full = coef[:, 0:LANES] + one_ref[...]
    s_full = coef[:, LANES:2 * LANES]
    gm = gm_ref[...]
    swap = swap_ref[...]

    def prep(x, w, scale):
        ss = _dot((x * x).astype(BF16), gm)
        y = x * lax.rsqrt(ss * (1.0 / dh) + EPS) * w
        y_hi = y.astype(BF16)
        y_lo = (y - y_hi.astype(F32)).astype(BF16)
        partner = _dot(y_hi, swap) + _dot(y_lo, swap)
        return ((y * c_full + partner * s_full) * scale).astype(BF16)

    qw = qw_ref[...]
    kw = kw_ref[...]
    for h in range(n_heads):
        sl = slice(h * LANES, (h + 1) * LANES)
        qo_ref[:, sl] = prep(q_ref[:, sl].astype(F32), qw, q_scale)
        ko_ref[:, sl] = prep(k_ref[:, sl].astype(F32), kw, 1.0)
        vo_ref[sl, :] = v_ref[:, sl].T


def _qkv(proj, positions, frq, gm, qw, kw, sel, one, swap, width, col0, dh, casts=(), tm=512):
    t = proj.shape[0]
    n_heads = width // LANES
    cb = col0 // width
    kern = functools.partial(_qkv_kernel, n_heads=n_heads, dh=dh,
                             q_scale=math.log2(math.e) / math.sqrt(dh))
    const = lambda shape: pl.BlockSpec(shape, lambda m: (0, 0))
    out = jax.ShapeDtypeStruct((t, width), BF16)
    cast_specs, cast_shapes = _cast_specs(casts, (t // tm,))
    return pl.pallas_call(
        _with_casts(kern, 11, 3, len(casts)),
        grid=(t // tm,),
        in_specs=[pl.BlockSpec((tm, width), lambda m: (m, cb)),
                  pl.BlockSpec((tm, width), lambda m: (m, cb + 1)),
                  pl.BlockSpec((tm, width), lambda m: (m, cb + 2)),
                  pl.BlockSpec((1, tm), lambda m: (0, m)),
                  const(frq.shape), const((LANES, LANES)), const((1, LANES)), const((1, LANES)),
                  const(sel.shape), const((1, LANES)), const((LANES, LANES))] + cast_specs,
        out_specs=[pl.BlockSpec((tm, width), lambda m: (m, 0)),
                   pl.BlockSpec((tm, width), lambda m: (m, 0)),
                   pl.BlockSpec((width, tm), lambda m: (0, m))] + cast_specs,
        out_shape=[out, out, jax.ShapeDtypeStruct((width, t), BF16)] + cast_shapes,
        compiler_params=_params("parallel"),
        name="qkv_prep",
    )(proj, proj, proj, positions, frq, gm, qw, kw, sel, one, swap, *casts)


def _boundary_rows(b_ref, level, c):
    h = 1 << level
    assert h >= 2
    if h == 2:
        sub = lax.broadcasted_iota(jnp.int32, (SUBLANES, LANES), 0)
        pieces = []
        for j in range(c // SUBLANES):
            r = SUBLANES * j
            lo = jnp.broadcast_to(b_ref[r + 1:r + 2, :], (SUBLANES, LANES))
            hi = jnp.broadcast_to(b_ref[r + 5:r + 6, :], (SUBLANES, LANES))
            pieces.append(jnp.where(sub < 4, lo, hi))
        return jnp.concatenate(pieces, axis=0)
    blk = max(2 * h, SUBLANES)
    pieces = []
    for j in range(c // blk):
        r = (j * blk // (2 * h)) * 2 * h + h - 1
        pieces.append(jnp.broadcast_to(b_ref[r:r + 1, :], (blk, LANES)))
    return jnp.concatenate(pieces, axis=0)


def _hgrn_kernel(q_ref, f_ref, i_ref, g_ref, lb_ref, nw_ref, tri_ref, o_ref, st_ref, b_ref, *,
                 chunk, layer, heads):
    c = chunk
    rows = q_ref.shape[0]
    n_levels = c.bit_length() - 1
    hs = range(heads)

    @pl.when(pl.program_id(2) == 0)
    def _():
        st_ref[...] = jnp.zeros_like(st_ref)

    lbx = lb_ref[...]
    mx = jnp.max(lbx, axis=0, keepdims=True)
    ex = jnp.exp(lbx - mx)
    lb_all = (jnp.sum(ex[0:layer + 1, :], axis=0, keepdims=True)
              / jnp.sum(ex, axis=0, keepdims=True))
    nw = nw_ref[...]

    ti = lax.broadcasted_iota(jnp.int32, (c, c), 0)
    si = lax.broadcasted_iota(jnp.int32, (c, c), 1)
    lvl = jnp.where(ti > si, 31 - lax.clz(ti ^ si), jnp.where(ti == si, -1, -2))
    row = lax.broadcasted_iota(jnp.int32, (c, LANES), 0)
    sgn = [jnp.where(((row >> level) & 1) == 1, 1.0, -1.0) for level in range(n_levels)]

    def body(ci, carry):
        r0 = pl.multiple_of(ci * c, c)
        tri = tri_ref[...]
        sl = [slice(h * LANES, (h + 1) * LANES) for h in hs]
        qb = [q_ref[pl.ds(r0, c), sl[h]] for h in hs]
        vb = [i_ref[pl.ds(r0, c), sl[h]] for h in hs]
        q = [qb[h].astype(F32) for h in hs]
        f, lf, k = [], [], []
        for h in hs:
            lb = lb_all[:, sl[h]]
            f.append(lb + (1.0 - lb) * _sigmoid(f_ref[pl.ds(r0, c), sl[h]].astype(F32)))
            lf.append(jnp.log2(f[h]))
            k.append(1.0 - f[h])
        parts = []
        for h in hs:
            p1 = lf[h].astype(BF16)
            parts += [p1, (lf[h] - p1.astype(F32)).astype(BF16)]
        cs = _dot(tri, jnp.concatenate(parts, axis=1))
        b = []
        for h in hs:
            o2 = 2 * h * LANES
            b.append(cs[:, o2:o2 + LANES] + cs[:, o2 + LANES:o2 + 2 * LANES])
            b_ref[h] = b[h]
        kb = [k[h].astype(BF16) for h in hs]
        inter = []
        for h in hs:
            b_last = b[h][c - 1:c, :]
            st = st_ref[h]
            inter.append(lax.dot_general((q[h] * jnp.exp2(b[h])).astype(BF16), st.astype(BF16),
                                         _NT, preferred_element_type=F32))
            kh = (k[h] * jnp.exp2(b_last - b[h])).astype(BF16)
            st_ref[h] = st * jnp.exp2(b_last) + lax.dot_general(vb[h], kh, _TN,
                                                                preferred_element_type=F32)
        scores = [jnp.where(lvl == -1, jnp.sum(q[h] * k[h], axis=-1, keepdims=True), 0.0)
                  for h in hs]
        for h in hs:
            s_0 = lax.dot_general((q[h] * f[h]).astype(BF16), kb[h], _NT,
                                  preferred_element_type=F32)
            scores[h] = jnp.where(lvl == 0, s_0, scores[h])
        for level in range(1, n_levels):
            for h in hs:
                m = _boundary_rows(b_ref.at[h], level, c)
                e = jnp.exp2((b[h] - m) * sgn[level]).astype(BF16)
                s_l = lax.dot_general(qb[h] * e, kb[h] * e, _NT, preferred_element_type=F32)
                scores[h] = jnp.where(lvl == level, s_l, scores[h])
        for h in hs:
            o = inter[h] + _dot(scores[h].astype(BF16), vb[h])
            var = jnp.mean(o * o, axis=-1, keepdims=True)
            on = o * lax.rsqrt(var + EPS) * nw
            g = g_ref[pl.ds(r0, c), sl[h]].astype(F32)
            o_ref[pl.ds(r0, c), sl[h]] = (on * (g * _sigmoid(g))).astype(o_ref.dtype)
        return carry

    lax.fori_loop(0, rows // c, body, 0)


def _hgrn(proj, lb, nw, layer, bsz, seq, n_heads, col0, casts=(), rows=512, chunk=128, heads=8):
    t = proj.shape[0]
    kdim = LANES
    width = heads * kdim
    ns = seq // rows
    ng = n_heads // heads
    cb = col0 // width

    def col_spec(j):
        return pl.BlockSpec((rows, width), lambda b, h, s: (b * ns + s, cb + j * ng + h))

    cast_specs, cast_shapes = _cast_specs(casts, (bsz, ng, ns))
    kern = functools.partial(_hgrn_kernel, chunk=chunk, layer=layer, heads=heads)
    return pl.pallas_call(
        _with_casts(kern, 7, 1, len(casts)),
        grid=(bsz, ng, ns),
        in_specs=[col_spec(0), col_spec(1), col_spec(2), col_spec(3),
                  pl.BlockSpec((lb.shape[0], width), lambda b, h, s: (0, h)),
                  pl.BlockSpec((1, kdim), lambda b, h, s: (0, 0)),
                  pl.BlockSpec((chunk, chunk), lambda b, h, s: (0, 0))] + cast_specs,
        out_specs=[pl.BlockSpec((rows, width), lambda b, h, s: (b * ns + s, h))] + cast_specs,
        out_shape=[jax.ShapeDtypeStruct((t, n_heads * kdim), BF16)] + cast_shapes,
        scratch_shapes=[pltpu.VMEM((heads, kdim, kdim), F32),
                        pltpu.VMEM((heads, chunk, kdim), F32)],
        compiler_params=_params("parallel", "parallel", "arbitrary"),
        name="hgrn2",
    )(proj, proj, proj, proj, lb, nw.reshape(1, kdim), jnp.tril(jnp.ones((chunk, chunk), BF16)),
      *casts)


def _attn_kernel(q_ref, qn_ref, k_ref, vt_ref, lam_ref, sw_ref, bias_ref, o_ref, qs_ref, m_ref,
                 l_ref, acc_ref, s_ref, mx_ref, *, tq, tk, dh, lam_init, heads):
    qi = pl.program_id(2)
    groups = 2 * heads
    hl = [slice(hh * LANES, (hh + 1) * LANES) for hh in range(heads)]

    def stack_queries(src_ref):
        for hh in range(heads):
            q = src_ref[:, hl[hh]]
            lane = lax.broadcasted_iota(jnp.int32, q.shape, 1)
            zero = jnp.zeros_like(q)
            qs_ref[2 * hh * tq:(2 * hh + 1) * tq, :] = jnp.where(lane < dh, q, zero)
            qs_ref[(2 * hh + 1) * tq:(2 * hh + 2) * tq, :] = jnp.where(lane < dh, zero, q)

    m_ref[...] = jnp.full_like(m_ref, -jnp.inf)
    l_ref[...] = jnp.zeros_like(l_ref)
    acc_ref[...] = jnp.zeros_like(acc_ref)

    def scores(j, slot):
        k0 = pl.multiple_of(j * tk, tk)
        kb = [k_ref[pl.ds(k0, tk), hl[hh]] for hh in range(heads)]
        for g in range(groups):
            cols = slice(g * tq, (g + 1) * tq)
            st = lax.dot_general(kb[g // 2], qs_ref[cols, :], _NT, preferred_element_type=F32)
            s_ref[slot, :, cols] = st
            mx_ref[slot, :, cols] = jnp.max(st, axis=0, keepdims=True)

    def update(cols, st, m_cur, vt):
        m_old = m_ref[:, cols]
        m_new = jnp.maximum(m_old, m_cur)
        alpha = jnp.exp2(m_old - m_new)
        pt = jnp.exp2(st - m_new)
        l_ref[:, cols] = alpha * l_ref[:, cols] + jnp.sum(pt, axis=0, keepdims=True)
        acc_ref[:, cols] = alpha * acc_ref[:, cols] + _dot(vt, pt.astype(BF16))
        m_ref[:, cols] = m_new

    def step(j, slot, masked):
        k0 = pl.multiple_of(j * tk, tk)
        vts = [vt_ref[hl[hh], pl.ds(k0, tk)] for hh in range(heads)]
        for g in range(groups):
            vt = vts[g // 2]
            if not masked:
                cols = slice(g * tq, (g + 1) * tq)
                update(cols, s_ref[slot, :, cols], mx_ref[slot, :, cols], vt)
                continue
            hq, hk = tq // 2, tk // 2
            lo = slice(g * tq, g * tq + hq)
            st = s_ref[slot, 0:hk, lo] + bias_ref[0:hk, 0:hq]
            update(lo, st, jnp.max(st, axis=0, keepdims=True), vt[:, 0:hk])
            hi = slice(g * tq + hq, (g + 1) * tq)
            st = s_ref[slot, :, hi] + bias_ref[:, hq:tq]
            update(hi, st, jnp.max(st, axis=0, keepdims=True), vt)

    n_full = (qi * tq) // tk
    odd = n_full % 2

    @pl.when(qi == 0)
    def _():
        stack_queries(q_ref)
        scores(0, 0)

    @pl.when(odd == 1)
    def _():
        scores(1, 0)
        step(0, 1, False)

    def pair(i, carry):
        j = odd + 2 * i
        scores(j + 1, 1)
        step(j, 0, False)
        scores(j + 2, 0)
        step(j + 1, 1, False)
        return carry

    lax.fori_loop(0, n_full // 2, pair, 0)
    step(n_full, 0, True)

    stack_queries(qn_ref)
    scores(0, (n_full + 1) % 2)

    lv = lam_ref[...]
    lam = (jnp.exp(jnp.sum(lv[0:1, :] * lv[1:2, :], axis=-1, keepdims=True))
           - jnp.exp(jnp.sum(lv[2:3, :] * lv[3:4, :], axis=-1, keepdims=True)) + lam_init)
    ot = acc_ref[...] / l_ref[...]
    for hh in range(heads):
        c0 = 2 * hh * tq
        o = (ot[:, c0:c0 + tq] - lam * ot[:, c0 + tq:c0 + 2 * tq]).T
        var = jnp.mean(o * o, axis=-1, keepdims=True)
        o_ref[:, hl[hh]] = (o * lax.rsqrt(var + EPS) * sw_ref[...]
                            * (1.0 - lam_init)).astype(o_ref.dtype)


def _attn(q, k, vt, lamv, sw, bsz, seq, n_heads, dh, lam_init, tq=512, tk=512, heads=2):
    t = q.shape[0]
    nq = seq // tq
    assert tq == tk
    kern = functools.partial(_attn_kernel, tq=tq, tk=tk, dh=dh, lam_init=lam_init, heads=heads)
    key = jnp.arange(tk)[:, None]
    qry = jnp.arange(tq)[None, :]
    bias = jnp.where(key <= qry, 0.0, -jnp.inf).astype(F32)
    width = heads * LANES
    cols = 2 * heads * tq
    return pl.pallas_call(
        kern,
        grid=(bsz, n_heads // heads, nq),
        in_specs=[pl.BlockSpec((tq, width), lambda b, h, i: (b * nq + i, h)),
                  pl.BlockSpec((tq, width), lambda b, h, i: (b * nq + jnp.minimum(i + 1, nq - 1), h)),
                  pl.BlockSpec((seq, width), lambda b, h, i: (b, h)),
                  pl.BlockSpec((width, seq), lambda b, h, i: (h, b)),
                  pl.BlockSpec(lamv.shape, lambda b, h, i: (0, 0)),
                  pl.BlockSpec((1, LANES), lambda b, h, i: (0, 0)),
                  pl.BlockSpec((tk, tq), lambda b, h, i: (0, 0))],
        out_specs=pl.BlockSpec((tq, width), lambda b, h, i: (b * nq + i, h)),
        out_shape=jax.ShapeDtypeStruct((t, n_heads * LANES), BF16),
        scratch_shapes=[pltpu.VMEM((cols, LANES), BF16),
                        pltpu.VMEM((1, cols), F32),
                        pltpu.VMEM((1, cols), F32),
                        pltpu.VMEM((LANES, cols), F32),
                        pltpu.VMEM((2, tk, cols), F32),
                        pltpu.VMEM((2, 1, cols), F32)],
        compiler_params=_params("parallel", "parallel", "arbitrary"),
        name="diff_attn",
    )(q, q, k, vt, lamv, sw.reshape(1, LANES), bias)


def _merge_kernel(oa_ref, ob_ref, wa_ref, wb_ref, wo_ref, *rest, n_slabs):
    ga_refs = rest[:n_slabs]
    gb_refs = rest[n_slabs:2 * n_slabs]
    h_ref, g_ref, o_ref, mix_ref = rest[2 * n_slabs:]
    oa = oa_ref[...]
    ob = ob_ref[...]
    tn = ga_refs[0].shape[1]
    for j in range(n_slabs):
        cols = slice(j * tn, (j + 1) * tn)
        ya = _dot(oa, wa_ref[:, cols])
        yb = _dot(ob, wb_ref[:, cols])
        ga = ga_refs[j][...].astype(F32)
        gb = gb_refs[j][...].astype(F32)
        mix_ref[:, cols] = (_sigmoid(ga) * ya + _sigmoid(gb) * yb).astype(mix_ref.dtype)
    o_ref[...] = h_ref[...] + g_ref[...] * _dot(mix_ref[...], wo_ref[...])


def _merge(oa, ob, wa, wb, wo, proj, col_ga, col_gb, h, mods, jg, seq, tm=512, tn=1024):
    t, kdim = oa.shape
    d = wa.shape[1]
    n_slabs = d // tn
    per_b = seq // tm
    resident = lambda shape: pl.BlockSpec(shape, lambda m: (0, 0), pipeline_mode=pl.Buffered(1))
    gate = lambda col, j: pl.BlockSpec((tm, tn), lambda m: (m, col // tn + j))
    return pl.pallas_call(
        functools.partial(_merge_kernel, n_slabs=n_slabs),
        grid=(t // tm,),
        in_specs=([pl.BlockSpec((tm, kdim), lambda m: (m, 0)),
                   pl.BlockSpec((tm, kdim), lambda m: (m, 0)),
                   resident((kdim, d)), resident((kdim, d)), resident((d, d))]
                  + [gate(col_ga, j) for j in range(n_slabs)]
                  + [gate(col_gb, j) for j in range(n_slabs)]
                  + [pl.BlockSpec((tm, d), lambda m: (m, 0)),
                     pl.BlockSpec((None, 1, d), lambda m: ((m // per_b) * N_MOD + jg, 0, 0))]),
        out_specs=pl.BlockSpec((tm, d), lambda m: (m, 0)),
        out_shape=jax.ShapeDtypeStruct((t, d), F32),
        scratch_shapes=[pltpu.VMEM((tm, d), BF16)],
        compiler_params=_params("parallel"),
        name="merge_out_proj",
    )(oa, ob, wa, wb, wo, *([proj] * (2 * n_slabs)), h, mods)


def kernel(x, c, positions, ada_w, ada_b, norm1_w, ffn1_w_in, ffn1_w_out, norm2_w, mix_w_in, hgrn_lb, hgrn_norm_w, hgrn_w_out, diff_q_norm_w, diff_k_norm_w, diff_lq1, diff_lk1, diff_lq2, diff_lk2, diff_subln_w, diff_w_out, mix_w_o, norm3_w, ffn2_w_in, ffn2_w_out):
    bsz, seq, d = x.shape
    depth = ada_w.shape[0]
    t = bsz * seq
    kdim = hgrn_norm_w.shape[-1]
    fw = hgrn_lb.shape[-1]
    n_hgrn = fw // kdim
    dh = diff_q_norm_w.shape[-1]
    vdim = diff_subln_w.shape[-1]
    qk_w = diff_w_out.shape[1]
    n_diff = qk_w // vdim
    rope_dim = dh // 4
    half = rope_dim // 2
    assert kdim == LANES and vdim == LANES and 2 * dh == LANES

    frq = (ROPE_THETA ** (-jnp.arange(half, dtype=F32) / half)).reshape(half, 1)
    lane = jnp.arange(LANES)
    dpos = lane % dh
    rot_l = (dpos < rope_dim)[None, :]
    hit = (jnp.arange(half)[:, None] == (dpos % half)[None, :]) & rot_l
    zero = jnp.zeros((half, LANES), F32)
    sel = jnp.concatenate([jnp.concatenate([hit.astype(F32), zero], axis=1),
                           jnp.concatenate([zero, hit.astype(F32)], axis=1)], axis=0).astype(BF16)
    one = jnp.where(dpos < rope_dim, 0.0, 1.0).astype(F32).reshape(1, LANES)
    src, dst = lane[:, None], lane[None, :]
    swap = (jnp.where((dpos < half)[None, :] & (src == dst + half), -1.0, 0.0)
            + jnp.where(((dpos >= half) & (dpos < rope_dim))[None, :] & (src == dst - half),
                        1.0, 0.0)).astype(BF16)
    gm = (lane[:, None] // dh == lane[None, :] // dh).astype(BF16)
    pos2 = positions.reshape(1, t)

    h = x.reshape(t, d)
    for l in range(depth):
        lam_init = 0.8 - 0.6 * math.exp(-0.3 * l)
        assert ada_w.shape[-1] == N_MOD * d
        mods = _ada(c, ada_w[l], ada_b[l]).reshape(bsz * N_MOD, 1, d)
        h, w_mix = _ffn(h, norm1_w[l], mods, 0, 1, 2, ffn1_w_in[l].astype(BF16),
                        ffn1_w_out[l].astype(BF16), seq, casts=(mix_w_in[l],))
        proj = _proj(h, norm2_w[l], mods, 3, 4, w_mix, seq)
        o_a, w2_in, w2_out = _hgrn(proj, hgrn_lb, hgrn_norm_w[l], l, bsz, seq, n_hgrn, 0,
                                   casts=(ffn2_w_in[l], ffn2_w_out[l]))
        qw = jnp.tile(diff_q_norm_w[l], LANES // dh).reshape(1, LANES)
        kw = jnp.tile(diff_k_norm_w[l], LANES // dh).reshape(1, LANES)
        qr, kr, vr, w_a, w_b, w_o = _qkv(proj, pos2, frq, gm, qw, kw, sel, one, swap, qk_w,
                                         4 * fw, dh,
                                         casts=(hgrn_w_out[l], diff_w_out[l], mix_w_o[l]))
        lamv = jnp.stack([diff_lq1[l], diff_lk1[l], diff_lq2[l], diff_lk2[l]])
        o_b = _attn(qr, kr, vr, lamv, diff_subln_w[l], bsz, seq, n_diff, dh, lam_init)
        h = _merge(o_a, o_b, w_a, w_b, w_o, proj, 4 * fw + 3 * qk_w, 4 * fw + 3 * qk_w + d,
                   h, mods, 5, seq)
        h = _ffn(h, norm3_w[l], mods, 6, 7, 8, w2_in, w2_out, seq)[0]
    return h.reshape(bsz, seq, d)
```

```python
import functools
import math

import jax
import jax.numpy as jnp
from jax import lax
from jax.experimental import pallas as pl
from jax.experimental.pallas import tpu as pltpu

F32 = jnp.float32
BF16 = jnp.bfloat16

EPS = 1e-6
ROPE_THETA = 500000.0
LANES = 128
SUBLANES = 8
BF16_ROWS = 2 * SUBLANES
VMEM_LIMIT = 56 * 1024 * 1024
N_MOD = 9

_NT = (((1,), (1,)), ((), ()))
_TN = (((0,), (0,)), ((), ()))


def _params(*sem):
    return pltpu.CompilerParams(dimension_semantics=sem, vmem_limit_bytes=VMEM_LIMIT)


def _dot(a, b):
    return jnp.dot(a, b, preferred_element_type=F32)


def _sigmoid(x):
    return 0.5 * jnp.tanh(0.5 * x) + 0.5


def _norm_mod(h, nw, sh, sc):
    var = jnp.mean(h * h, axis=-1, keepdims=True)
    return (h * lax.rsqrt(var + EPS) * nw) * (1.0 + sc) + sh


def _with_casts(kernel_fn, n_in, n_out, n_cast):
    def wrapped(*refs):
        ins = refs[:n_in]
        cast_in = refs[n_in:n_in + n_cast]
        outs = refs[n_in + n_cast:n_in + n_cast + n_out]
        cast_out = refs[n_in + n_cast + n_out:n_in + 2 * n_cast + n_out]
        scratch = refs[n_in + 2 * n_cast + n_out:]
        for src, dst in zip(cast_in, cast_out):
            dst[...] = src[...].astype(dst.dtype)
        kernel_fn(*ins, *outs, *scratch)
    return wrapped


def _cast_specs(weights, grid, major_steps=None):
    lead = grid if major_steps is None else grid[:major_steps]
    n_row = math.prod(lead)
    n_col = 1 if major_steps is None else math.prod(grid[major_steps:])

    def index_map(*idx):
        step = 0
        for size, i in zip(lead, idx):
            step = step * size + i
        col = 0
        for size, i in zip(grid[len(lead):], idx[len(lead):]):
            col = col * size + i
        return step, col

    specs, shapes = [], []
    for w in weights:
        r, c = w.shape
        rb, cb = r // n_row, c // n_col
        assert rb * n_row == r and rb % BF16_ROWS == 0 and cb * n_col == c and cb % LANES == 0
        specs.append(pl.BlockSpec((rb, cb), index_map))
        shapes.append(jax.ShapeDtypeStruct((r, c), BF16))
    return specs, shapes


NORM_ROWS = BF16_ROWS


def _norm_mod_to(h_ref, nw_ref, sh_ref, sc_ref, xn_ref):
    nw = nw_ref[...]
    sh = sh_ref[...]
    sc = sc_ref[...]

    def body(i, carry):
        r = pl.multiple_of(i * NORM_ROWS, NORM_ROWS)
        xn = _norm_mod(h_ref[pl.ds(r, NORM_ROWS), :], nw, sh, sc)
        xn_ref[pl.ds(r, NORM_ROWS), :] = xn.astype(xn_ref.dtype)
        return carry

    lax.fori_loop(0, h_ref.shape[0] // NORM_ROWS, body, 0, unroll=8)


def _ada_kernel(c_ref, w_ref, b_ref, o_ref):
    c = c_ref[...]
    a = (c * _sigmoid(c)).astype(BF16)
    o_ref[...] = _dot(a, w_ref[...].astype(BF16)) + b_ref[...]


def _ada(c, w, b, tn=1024):
    bsz, d = c.shape
    n = w.shape[1]
    rows = SUBLANES
    assert bsz <= rows
    cp = jnp.zeros((rows, d), F32).at[:bsz].set(c)
    out = pl.pallas_call(
        _ada_kernel,
        grid=(n // tn,),
        in_specs=[pl.BlockSpec((rows, d), lambda j: (0, 0)),
                  pl.BlockSpec((d, tn), lambda j: (0, j)),
                  pl.BlockSpec((1, tn), lambda j: (0, j))],
        out_specs=pl.BlockSpec((rows, tn), lambda j: (0, j)),
        out_shape=jax.ShapeDtypeStruct((rows, n), F32),
        compiler_params=_params("parallel"),
        name="adaln",
    )(cp, w, b.reshape(1, n))
    return out[:bsz]


def _ffn_kernel(h_ref, nw_ref, sh_ref, sc_ref, g_ref, wg_ref, wu_ref, wo_ref, o_ref,
                xn_ref, acc_ref):
    f = pl.program_id(1)

    @pl.when(f == 0)
    def _():
        _norm_mod_to(h_ref, nw_ref, sh_ref, sc_ref, xn_ref)
        acc_ref[...] = jnp.zeros_like(acc_ref)

    x = xn_ref[...]
    g = _dot(x, wg_ref[...])
    u = _dot(x, wu_ref[...])
    a = (g * _sigmoid(g)) * u
    acc_ref[...] += _dot(a.astype(BF16), wo_ref[...])

    @pl.when(f == pl.num_programs(1) - 1)
    def _():
        o_ref[...] = h_ref[...] + (0.5 * g_ref[...]) * acc_ref[...]


def _ffn(h, nw, mods, jsh, jsc, jg, w_in, w_out, seq, casts=(), tm=512, tf=512):
    t, d = h.shape
    dff = w_out.shape[0]
    nf = dff // tf
    per_b = seq // tm

    def mod_spec(j):
        return pl.BlockSpec((None, 1, d), lambda m, f: ((m // per_b) * N_MOD + j, 0, 0))

    cast_specs, cast_shapes = _cast_specs(casts, (t // tm, nf), major_steps=1)
    return pl.pallas_call(
        _with_casts(_ffn_kernel, 8, 1, len(casts)),
        grid=(t // tm, nf),
        in_specs=[pl.BlockSpec((tm, d), lambda m, f: (m, 0)),
                  pl.BlockSpec((1, d), lambda m, f: (0, 0)),
                  mod_spec(jsh), mod_spec(jsc), mod_spec(jg),
                  pl.BlockSpec((d, tf), lambda m, f: (0, f)),
                  pl.BlockSpec((d, tf), lambda m, f: (0, f + nf)),
                  pl.BlockSpec((tf, d), lambda m, f: (f, 0))] + cast_specs,
        out_specs=[pl.BlockSpec((tm, d), lambda m, f: (m, 0))] + cast_specs,
        out_shape=[jax.ShapeDtypeStruct((t, d), F32)] + cast_shapes,
        scratch_shapes=[pltpu.VMEM((tm, d), BF16), pltpu.VMEM((tm, d), F32)],
        compiler_params=_params("parallel", "arbitrary"),
        name="ffn",
    )(h, nw.reshape(1, d), mods, mods, mods, w_in, w_in, w_out, *casts)


def _proj_kernel(h_ref, nw_ref, sh_ref, sc_ref, w_ref, o_ref, xn_ref):
    @pl.when(pl.program_id(1) == 0)
    def _():
        _norm_mod_to(h_ref, nw_ref, sh_ref, sc_ref, xn_ref)

    o_ref[...] = _dot(xn_ref[...], w_ref[...]).astype(o_ref.dtype)


def _proj(h, nw, mods, jsh, jsc, w, seq, tm=1024, tn=1024):
    t, d = h.shape
    n = w.shape[1]
    per_b = seq // tm

    def mod_spec(j):
        return pl.BlockSpec((None, 1, d), lambda m, k: ((m // per_b) * N_MOD + j, 0, 0))

    return pl.pallas_call(
        _proj_kernel,
        grid=(t // tm, n // tn),
        in_specs=[pl.BlockSpec((tm, d), lambda m, k: (m, 0)),
                  pl.BlockSpec((1, d), lambda m, k: (0, 0)),
                  mod_spec(jsh), mod_spec(jsc),
                  pl.BlockSpec((d, tn), lambda m, k: (0, k))],
        out_specs=pl.BlockSpec((tm, tn), lambda m, k: (m, k)),
        out_shape=jax.ShapeDtypeStruct((t, n), BF16),
        scratch_shapes=[pltpu.VMEM((tm, d), BF16)],
        compiler_params=_params("parallel", "arbitrary"),
        name="mix_proj",
    )(h, nw.reshape(1, d), mods, mods, w)


def _qkv_kernel(q_ref, k_ref, v_ref, pos_ref, frq_ref, gm_ref, qw_ref, kw_ref, sel_ref, one_ref,
                swap_ref, qo_ref, ko_ref, vo_ref, *, n_heads, dh, q_scale):
    ang = frq_ref[...] * pos_ref[...].astype(F32)
    cs = jnp.concatenate([jnp.cos(ang), jnp.sin(ang)], axis=0)
    hi = cs.astype(BF16)
    lo = (cs - hi.astype(F32)).astype(BF16)
    sel = sel_ref[...]
    coef = (lax.dot_general(hi, sel, _TN, preferred_element_type=F32)
            + lax.dot_general(lo, sel, _TN, preferred_element_type=F32))
    c_full = coef[:, 0:LANES] + one_ref[...]
    s_full = coef[:, LANES:2 * LANES]
    gm = gm_ref[...]
    swap = swap_ref[...]

    def prep(x, w):
        ss = _dot((x * x).astype(BF16), gm)
        y = x * lax.rsqrt(ss * (1.0 / dh) + EPS) * w
        partner = _dot(y.astype(BF16), swap)
        return (y * c_full + partner * s_full).astype(BF16)

    qw = qw_ref[...] * q_scale
    kw = kw_ref[...]
    for h in range(n_heads):
        sl = slice(h * LANES, (h + 1) * LANES)
        qo_ref[:, sl] = prep(q_ref[:, sl].astype(F32), qw)
        ko_ref[:, sl] = prep(k_ref[:, sl].astype(F32), kw)
        vo_ref[sl, :] = v_ref[:, sl].T


def _qkv(proj, positions, frq, gm, qw, kw, sel, one, swap, width, col0, dh, casts=(), tm=512):
    t = proj.shape[0]
    n_heads = width // LANES
    cb = col0 // width
    kern = functools.partial(_qkv_kernel, n_heads=n_heads, dh=dh,
                             q_scale=math.log2(math.e) / math.sqrt(dh))
    const = lambda shape: pl.BlockSpec(shape, lambda m: (0, 0))
    out = jax.ShapeDtypeStruct((t, width), BF16)
    cast_specs, cast_shapes = _cast_specs(casts, (t // tm,))
    return pl.pallas_call(
        _with_casts(kern, 11, 3, len(casts)),
        grid=(t // tm,),
        in_specs=[pl.BlockSpec((tm, width), lambda m: (m, cb)),
                  pl.BlockSpec((tm, width), lambda m: (m, cb + 1)),
                  pl.BlockSpec((tm, width), lambda m: (m, cb + 2)),
                  pl.BlockSpec((1, tm), lambda m: (0, m)),
                  const(frq.shape), const((LANES, LANES)), const((1, LANES)), const((1, LANES)),
                  const(sel.shape), const((1, LANES)), const((LANES, LANES))] + cast_specs,
        out_specs=[pl.BlockSpec((tm, width), lambda m: (m, 0)),
                   pl.BlockSpec((tm, width), lambda m: (m, 0)),
                   pl.BlockSpec((width, tm), lambda m: (0, m))] + cast_specs,
        out_shape=[out, out, jax.ShapeDtypeStruct((width, t), BF16)] + cast_shapes,
        compiler_params=_params("parallel"),
        name="qkv_prep",
    )(proj, proj, proj, positions, frq, gm, qw, kw, sel, one, swap, *casts)


def _boundary_rows(b_ref, level, c):
    h = 1 << level
    assert h >= 2
    if h == 2:
        sub = lax.broadcasted_iota(jnp.int32, (SUBLANES, LANES), 0)
        pieces = []
        for j in range(c // SUBLANES):
            r = SUBLANES * j
            lo = jnp.broadcast_to(b_ref[r + 1:r + 2, :], (SUBLANES, LANES))
            hi = jnp.broadcast_to(b_ref[r + 5:r + 6, :], (SUBLANES, LANES))
            pieces.append(jnp.where(sub < 4, lo, hi))
        return jnp.concatenate(pieces, axis=0)
    blk = max(2 * h, SUBLANES)
    pieces = []
    for j in range(c // blk):
        r = (j * blk // (2 * h)) * 2 * h + h - 1
        pieces.append(jnp.broadcast_to(b_ref[r:r + 1, :], (blk, LANES)))
    return jnp.concatenate(pieces, axis=0)


def _hgrn_kernel(q_ref, f_ref, i_ref, g_ref, lb_ref, nw_ref, tri_ref, o_ref, st_ref, b_ref, *,
                 chunk, layer, heads):
    c = chunk
    rows = q_ref.shape[0]
    n_levels = c.bit_length() - 1
    hs = range(heads)

    @pl.when(pl.program_id(2) == 0)
    def _():
        st_ref[...] = jnp.zeros_like(st_ref)

    lbx = lb_ref[...]
    mx = jnp.max(lbx, axis=0, keepdims=True)
    ex = jnp.exp(lbx - mx)
    lb_all = (jnp.sum(ex[0:layer + 1, :], axis=0, keepdims=True)
              / jnp.sum(ex, axis=0, keepdims=True))
    nw = nw_ref[...]

    ti = lax.broadcasted_iota(jnp.int32, (c, c), 0)
    si = lax.broadcasted_iota(jnp.int32, (c, c), 1)
    lvl = jnp.where(ti > si, 31 - lax.clz(ti ^ si), jnp.where(ti == si, -1, -2))
    row = lax.broadcasted_iota(jnp.int32, (c, LANES), 0)
    sgn = [jnp.where(((row >> level) & 1) == 1, 1.0, -1.0) for level in range(n_levels)]

    def body(ci, carry):
        r0 = pl.multiple_of(ci * c, c)
        tri = tri_ref[...]
        sl = [slice(h * LANES, (h + 1) * LANES) for h in hs]
        qb = [q_ref[pl.ds(r0, c), sl[h]] for h in hs]
        vb = [i_ref[pl.ds(r0, c), sl[h]] for h in hs]
        q = [qb[h].astype(F32) for h in hs]
        f, lf, k = [], [], []
        for h in hs:
            lb = lb_all[:, sl[h]]
            f.append(lb + (1.0 - lb) * _sigmoid(f_ref[pl.ds(r0, c), sl[h]].astype(F32)))
            lf.append(jnp.log2(f[h]))
            k.append(1.0 - f[h])
        parts = []
        for h in hs:
            p1 = lf[h].astype(BF16)
            parts += [p1, (lf[h] - p1.astype(F32)).astype(BF16)]
        cs = _dot(tri, jnp.concatenate(parts, axis=1))
        b = []
        for h in hs:
            o2 = 2 * h * LANES
            b.append(cs[:, o2:o2 + LANES] + cs[:, o2 + LANES:o2 + 2 * LANES])
            b_ref[h] = b[h]
        kb = [k[h].astype(BF16) for h in hs]
        inter = []
        for h in hs:
            b_last = b[h][c - 1:c, :]
            st = st_ref[h]
            inter.append(lax.dot_general((q[h] * jnp.exp2(b[h])).astype(BF16), st.astype(BF16),
                                         _NT, preferred_element_type=F32))
            kh = (k[h] * jnp.exp2(b_last - b[h])).astype(BF16)
            st_ref[h] = st * jnp.exp2(b_last) + lax.dot_general(vb[h], kh, _TN,
                                                                preferred_element_type=F32)
        scores = [jnp.where(lvl == -1, jnp.sum(q[h] * k[h], axis=-1, keepdims=True), 0.0)
                  for h in hs]
        for h in hs:
            s_0 = lax.dot_general((q[h] * f[h]).astype(BF16), kb[h], _NT,
                                  preferred_element_type=F32)
            scores[h] = jnp.where(lvl == 0, s_0, scores[h])
        for level in range(1, n_levels):
            for h in hs:
                m = _boundary_rows(b_ref.at[h], level, c)
                e = jnp.exp2((b[h] - m) * sgn[level]).astype(BF16)
                s_l = lax.dot_general(qb[h] * e, kb[h] * e, _NT, preferred_element_type=F32)
                half = 1 << level
                if half < SUBLANES:
                    scores[h] = jnp.where(lvl == level, s_l, scores[h])
                    continue
                pieces = []
                for blk in range(0, c, 2 * half):
                    right = slice(blk + half, blk + 2 * half)
                    pieces.append(scores[h][blk:blk + half])
                    pieces.append(jnp.where(lvl[right] == level, s_l[right], scores[h][right]))
                scores[h] = jnp.concatenate(pieces, axis=0)
        for h in hs:
            o = inter[h] + _dot(scores[h].astype(BF16), vb[h])
            var = jnp.mean(o * o, axis=-1, keepdims=True)
            on = o * lax.rsqrt(var + EPS) * nw
            g = g_ref[pl.ds(r0, c), sl[h]].astype(F32)
            o_ref[pl.ds(r0, c), sl[h]] = (on * (g * _sigmoid(g))).astype(o_ref.dtype)
        return carry

    lax.fori_loop(0, rows // c, body, 0)


def _hgrn(proj, lb, nw, layer, bsz, seq, n_heads, col0, casts=(), rows=512, chunk=128, heads=8):
    t = proj.shape[0]
    kdim = LANES
    width = heads * kdim
    ns = seq // rows
    ng = n_heads // heads
    cb = col0 // width

    def col_spec(j):
        return pl.BlockSpec((rows, width), lambda b, h, s: (b * ns + s, cb + j * ng + h))

    cast_specs, cast_shapes = _cast_specs(casts, (bsz, ng, ns))
    kern = functools.partial(_hgrn_kernel, chunk=chunk, layer=layer, heads=heads)
    return pl.pallas_call(
        _with_casts(kern, 7, 1, len(casts)),
        grid=(bsz, ng, ns),
        in_specs=[col_spec(0), col_spec(1), col_spec(2), col_spec(3),
                  pl.BlockSpec((lb.shape[0], width), lambda b, h, s: (0, h)),
                  pl.BlockSpec((1, kdim), lambda b, h, s: (0, 0)),
                  pl.BlockSpec((chunk, chunk), lambda b, h, s: (0, 0))] + cast_specs,
        out_specs=[pl.BlockSpec((rows, width), lambda b, h, s: (b * ns + s, h))] + cast_specs,
        out_shape=[jax.ShapeDtypeStruct((t, n_heads * kdim), BF16)] + cast_shapes,
        scratch_shapes=[pltpu.VMEM((heads, kdim, kdim), F32),
                        pltpu.VMEM((heads, chunk, kdim), F32)],
        compiler_params=_params("parallel", "parallel", "arbitrary"),
        name="hgrn2",
    )(proj, proj, proj, proj, lb, nw.reshape(1, kdim), jnp.tril(jnp.ones((chunk, chunk), BF16)),
      *casts)


def _attn_kernel(q_ref, qn_ref, k_ref, vt_ref, lam_ref, sw_ref, bias_ref, o_ref, qs_ref, m_ref,
                 l_ref, acc_ref, s_ref, mx_ref, *, tq, tk, dh, lam_init, heads):
    qi = pl.program_id(2)
    groups = 2 * heads
    hl = [slice(hh * LANES, (hh + 1) * LANES) for hh in range(heads)]

    def stack_queries(src_ref):
        for hh in range(heads):
            q = src_ref[:, hl[hh]]
            lane = lax.broadcasted_iota(jnp.int32, q.shape, 1)
            zero = jnp.zeros_like(q)
            qs_ref[2 * hh * tq:(2 * hh + 1) * tq, :] = jnp.where(lane < dh, q, zero)
            qs_ref[(2 * hh + 1) * tq:(2 * hh + 2) * tq, :] = jnp.where(lane < dh, zero, q)

    m_ref[...] = jnp.full_like(m_ref, -jnp.inf)
    l_ref[...] = jnp.zeros_like(l_ref)
    acc_ref[...] = jnp.zeros_like(acc_ref)

    def scores(j, slot):
        k0 = pl.multiple_of(j * tk, tk)
        kb = [k_ref[pl.ds(k0, tk), hl[hh]] for hh in range(heads)]
        for g in range(groups):
            cols = slice(g * tq, (g + 1) * tq)
            st = lax.dot_general(kb[g // 2], qs_ref[cols, :], _NT, preferred_element_type=F32)
            s_ref[slot, :, cols] = st
            mx_ref[slot, :, cols] = jnp.max(st, axis=0, keepdims=True)

    def update(cols, st, m_cur, vt):
        m_old = m_ref[:, cols]
        m_new = jnp.maximum(m_old, m_cur)
        alpha = jnp.exp2(m_old - m_new)
        pt = jnp.exp2(st - m_new)
        l_ref[:, cols] = alpha * l_ref[:, cols] + jnp.sum(pt, axis=0, keepdims=True)
        acc_ref[:, cols] = alpha * acc_ref[:, cols] + _dot(vt, pt.astype(BF16))
        m_ref[:, cols] = m_new

    def step(j, slot, masked):
        k0 = pl.multiple_of(j * tk, tk)
        vts = [vt_ref[hl[hh], pl.ds(k0, tk)] for hh in range(heads)]
        for g in range(groups):
            vt = vts[g // 2]
            if not masked:
                cols = slice(g * tq, (g + 1) * tq)
                update(cols, s_ref[slot, :, cols], mx_ref[slot, :, cols], vt)
                continue
            hq, hk = tq // 2, tk // 2
            lo = slice(g * tq, g * tq + hq)
            st = s_ref[slot, 0:hk, lo] + bias_ref[0:hk, 0:hq]
            update(lo, st, jnp.max(st, axis=0, keepdims=True), vt[:, 0:hk])
            hi = slice(g * tq + hq, (g + 1) * tq)
            st = s_ref[slot, :, hi] + bias_ref[:, hq:tq]
            update(hi, st, jnp.max(st, axis=0, keepdims=True), vt)

    n_full = (qi * tq) // tk
    odd = n_full % 2

    @pl.when(qi == 0)
    def _():
        stack_queries(q_ref)
        scores(0, 0)

    @pl.when(odd == 1)
    def _():
        scores(1, 0)
        step(0, 1, False)

    def pair(i, carry):
        j = odd + 2 * i
        scores(j + 1, 1)
        step(j, 0, False)
        scores(j + 2, 0)
        step(j + 1, 1, False)
        return carry

    lax.fori_loop(0, n_full // 2, pair, 0)
    step(n_full, 0, True)

    stack_queries(qn_ref)
    scores(0, (n_full + 1) % 2)

    lv = lam_ref[...]
    lam = (jnp.exp(jnp.sum(lv[0:1, :] * lv[1:2, :], axis=-1, keepdims=True))
           - jnp.exp(jnp.sum(lv[2:3, :] * lv[3:4, :], axis=-1, keepdims=True)) + lam_init)
    ot = acc_ref[...] / l_ref[...]
    for hh in range(heads):
        c0 = 2 * hh * tq
        o = (ot[:, c0:c0 + tq] - lam * ot[:, c0 + tq:c0 + 2 * tq]).T
        var = jnp.mean(o * o, axis=-1, keepdims=True)
        o_ref[:, hl[hh]] = (o * lax.rsqrt(var + EPS) * sw_ref[...]
                            * (1.0 - lam_init)).astype(o_ref.dtype)


def _attn(q, k, vt, lamv, sw, bsz, seq, n_heads, dh, lam_init, tq=512, tk=512, heads=2):
    t = q.shape[0]
    nq = seq // tq
    assert tq == tk
    kern = functools.partial(_attn_kernel, tq=tq, tk=tk, dh=dh, lam_init=lam_init, heads=heads)
    key = jnp.arange(tk)[:, None]
    qry = jnp.arange(tq)[None, :]
    bias = jnp.where(key <= qry, 0.0, -jnp.inf).astype(F32)
    width = heads * LANES
    cols = 2 * heads * tq
    return pl.pallas_call(
        kern,
        grid=(bsz, n_heads // heads, nq),
        in_specs=[pl.BlockSpec((tq, width), lambda b, h, i: (b * nq + i, h)),
                  pl.BlockSpec((tq, width), lambda b, h, i: (b * nq + jnp.minimum(i + 1, nq - 1), h)),
                  pl.BlockSpec((seq, width), lambda b, h, i: (b, h)),
                  pl.BlockSpec((width, seq), lambda b, h, i: (h, b)),
                  pl.BlockSpec(lamv.shape, lambda b, h, i: (0, 0)),
                  pl.BlockSpec((1, LANES), lambda b, h, i: (0, 0)),
                  pl.BlockSpec((tk, tq), lambda b, h, i: (0, 0))],
        out_specs=pl.BlockSpec((tq, width), lambda b, h, i: (b * nq + i, h)),
        out_shape=jax.ShapeDtypeStruct((t, n_heads * LANES), BF16),
        scratch_shapes=[pltpu.VMEM((cols, LANES), BF16),
                        pltpu.VMEM((1, cols), F32),
                        pltpu.VMEM((1, cols), F32),
                        pltpu.VMEM((LANES, cols), F32),
                        pltpu.VMEM((2, tk, cols), F32),
                        pltpu.VMEM((2, 1, cols), F32)],
        compiler_params=_params("parallel", "parallel", "arbitrary"),
        name="diff_attn",
    )(q, q, k, vt, lamv, sw.reshape(1, LANES), bias)


def _merge_kernel(oa_ref, ob_ref, wa_ref, wb_ref, wo_ref, *rest, n_slabs):
    ga_refs = rest[:n_slabs]
    gb_refs = rest[n_slabs:2 * n_slabs]
    h_ref, g_ref, o_ref, mix_ref = rest[2 * n_slabs:]
    oa = oa_ref[...]
    ob = ob_ref[...]
    tn = ga_refs[0].shape[1]
    for j in range(n_slabs):
        cols = slice(j * tn, (j + 1) * tn)
        ya = _dot(oa, wa_ref[:, cols])
        yb = _dot(ob, wb_ref[:, cols])
        ga = ga_refs[j][...].astype(F32)
        gb = gb_refs[j][...].astype(F32)
        mix_ref[:, cols] = (_sigmoid(ga) * ya + _sigmoid(gb) * yb).astype(mix_ref.dtype)
    o_ref[...] = h_ref[...] + g_ref[...] * _dot(mix_ref[...], wo_ref[...])


def _merge(oa, ob, wa, wb, wo, proj, col_ga, col_gb, h, mods, jg, seq, tm=512, tn=1024):
    t, kdim = oa.shape
    d = wa.shape[1]
    n_slabs = d // tn
    per_b = seq // tm
    resident = lambda shape: pl.BlockSpec(shape, lambda m: (0, 0), pipeline_mode=pl.Buffered(1))
    gate = lambda col, j: pl.BlockSpec((tm, tn), lambda m: (m, col // tn + j))
    return pl.pallas_call(
        functools.partial(_merge_kernel, n_slabs=n_slabs),
        grid=(t // tm,),
        in_specs=([pl.BlockSpec((tm, kdim), lambda m: (m, 0)),
                   pl.BlockSpec((tm, kdim), lambda m: (m, 0)),
                   resident((kdim, d)), resident((kdim, d)), resident((d, d))]
                  + [gate(col_ga, j) for j in range(n_slabs)]
                  + [gate(col_gb, j) for j in range(n_slabs)]
                  + [pl.BlockSpec((tm, d), lambda m: (m, 0)),
                     pl.BlockSpec((None, 1, d), lambda m: ((m // per_b) * N_MOD + jg, 0, 0))]),
        out_specs=pl.BlockSpec((tm, d), lambda m: (m, 0)),
        out_shape=jax.ShapeDtypeStruct((t, d), F32),
        scratch_shapes=[pltpu.VMEM((tm, d), BF16)],
        compiler_params=_params("parallel"),
        name="merge_out_proj",
    )(oa, ob, wa, wb, wo, *([proj] * (2 * n_slabs)), h, mods)


def kernel(x, c, positions, ada_w, ada_b, norm1_w, ffn1_w_in, ffn1_w_out, norm2_w, mix_w_in, hgrn_lb, hgrn_norm_w, hgrn_w_out, diff_q_norm_w, diff_k_norm_w, diff_lq1, diff_lk1, diff_lq2, diff_lk2, diff_subln_w, diff_w_out, mix_w_o, norm3_w, ffn2_w_in, ffn2_w_out):
    bsz, seq, d = x.shape
    depth = ada_w.shape[0]
    t = bsz * seq
    kdim = hgrn_norm_w.shape[-1]
    fw = hgrn_lb.shape[-1]
    n_hgrn = fw // kdim
    dh = diff_q_norm_w.shape[-1]
    vdim = diff_subln_w.shape[-1]
    qk_w = diff_w_out.shape[1]
    n_diff = qk_w // vdim
    rope_dim = dh // 4
    half = rope_dim // 2
    assert kdim == LANES and vdim == LANES and 2 * dh == LANES

    frq = (ROPE_THETA ** (-jnp.arange(half, dtype=F32) / half)).reshape(half, 1)
    lane = jnp.arange(LANES)
    dpos = lane % dh
    rot_l = (dpos < rope_dim)[None, :]
    hit = (jnp.arange(half)[:, None] == (dpos % half)[None, :]) & rot_l
    zero = jnp.zeros((half, LANES), F32)
    sel = jnp.concatenate([jnp.concatenate([hit.astype(F32), zero], axis=1),
                           jnp.concatenate([zero, hit.astype(F32)], axis=1)], axis=0).astype(BF16)
    one = jnp.where(dpos < rope_dim, 0.0, 1.0).astype(F32).reshape(1, LANES)
    src, dst = lane[:, None], lane[None, :]
    swap = (jnp.where((dpos < half)[None, :] & (src == dst + half), -1.0, 0.0)
            + jnp.where(((dpos >= half) & (dpos < rope_dim))[None, :] & (src == dst - half),
                        1.0, 0.0)).astype(BF16)
    gm = (lane[:, None] // dh == lane[None, :] // dh).astype(BF16)
    pos2 = positions.reshape(1, t)

    h = x.reshape(t, d)
    for l in range(depth):
        lam_init = 0.8 - 0.6 * math.exp(-0.3 * l)
        assert ada_w.shape[-1] == N_MOD * d
        mods = _ada(c, ada_w[l], ada_b[l]).reshape(bsz * N_MOD, 1, d)
        h, w_mix = _ffn(h, norm1_w[l], mods, 0, 1, 2, ffn1_w_in[l].astype(BF16),
                        ffn1_w_out[l].astype(BF16), seq, casts=(mix_w_in[l],))
        proj = _proj(h, norm2_w[l], mods, 3, 4, w_mix, seq)
        o_a, w2_in, w2_out = _hgrn(proj, hgrn_lb, hgrn_norm_w[l], l, bsz, seq, n_hgrn, 0,
                                   casts=(ffn2_w_in[l], ffn2_w_out[l]))
        qw = jnp.tile(diff_q_norm_w[l], LANES // dh).reshape(1, LANES)
        kw = jnp.tile(diff_k_norm_w[l], LANES // dh).reshape(1, LANES)
        qr, kr, vr, w_a, w_b, w_o = _qkv(proj, pos2, frq, gm, qw, kw, sel, one, swap, qk_w,
                                         4 * fw, dh,
                                         casts=(hgrn_w_out[l], diff_w_out[l], mix_w_o[l]))
        lamv = jnp.stack([diff_lq1[l], diff_lk1[l], diff_lq2[l], diff_lk2[l]])
        o_b = _attn(qr, kr, vr, lamv, diff_subln_w[l], bsz, seq, n_diff, dh, lam_init)
        h = _merge(o_a, o_b, w_a, w_b, w_o, proj, 4 * fw + 3 * qk_w, 4 * fw + 3 * qk_w + d,
                   h, mods, 5, seq)
        h = _ffn(h, norm3_w[l], mods, 6, 7, 8, w2_in, w2_out, seq)[0]
    return h.reshape(bsz, seq, d)
```

```python
import functools
import math

import jax
import jax.numpy as jnp
from jax import lax
from jax.experimental import pallas as pl
from jax.experimental.pallas import tpu as pltpu

F32 = jnp.float32
BF16 = jnp.bfloat16

EPS = 1e-6
ROPE_THETA = 500000.0
LANES = 128
SUBLANES = 8
BF16_ROWS = 2 * SUBLANES
VMEM_LIMIT = 56 * 1024 * 1024
N_MOD = 9

_NT = (((1,), (1,)), ((), ()))
_TN = (((0,), (0,)), ((), ()))


def _params(*sem):
    return pltpu.CompilerParams(dimension_semantics=sem, vmem_limit_bytes=VMEM_LIMIT)


def _dot(a, b):
    return jnp.dot(a, b, preferred_element_type=F32)


def _sigmoid(x):
    return 0.5 * jnp.tanh(0.5 * x) + 0.5


def _norm_mod(h, nw, sh, sc):
    var = jnp.mean(h * h, axis=-1, keepdims=True)
    return (h * lax.rsqrt(var + EPS) * nw) * (1.0 + sc) + sh


def _with_casts(kernel_fn, n_in, n_out, n_cast):
    def wrapped(*refs):
        ins = refs[:n_in]
        cast_in = refs[n_in:n_in + n_cast]
        outs = refs[n_in + n_cast:n_in + n_cast + n_out]
        cast_out = refs[n_in + n_cast + n_out:n_in + 2 * n_cast + n_out]
        scratch = refs[n_in + 2 * n_cast + n_out:]
        for src, dst in zip(cast_in, cast_out):
            dst[...] = src[...].astype(dst.dtype)
        kernel_fn(*ins, *outs, *scratch)
    return wrapped


def _cast_specs(weights, grid, major_steps=None):
    lead = grid if major_steps is None else grid[:major_steps]
    n_row = math.prod(lead)
    n_col = 1 if major_steps is None else math.prod(grid[major_steps:])

    def index_map(*idx):
        step = 0
        for size, i in zip(lead, idx):
            step = step * size + i
        col = 0
        for size, i in zip(grid[len(lead):], idx[len(lead):]):
            col = col * size + i
        return step, col

    specs, shapes = [], []
    for w in weights:
        r, c = w.shape
        rb, cb = r // n_row, c // n_col
        assert rb * n_row == r and rb % BF16_ROWS == 0 and cb * n_col == c and cb % LANES == 0
        specs.append(pl.BlockSpec((rb, cb), index_map))
        shapes.append(jax.ShapeDtypeStruct((r, c), BF16))
    return specs, shapes


NORM_ROWS = BF16_ROWS


def _norm_mod_to(h_ref, nw_ref, sh_ref, sc_ref, xn_ref):
    nw = nw_ref[...]
    sh = sh_ref[...]
    sc = sc_ref[...]

    def body(i, carry):
        r = pl.multiple_of(i * NORM_ROWS, NORM_ROWS)
        xn = _norm_mod(h_ref[pl.ds(r, NORM_ROWS), :], nw, sh, sc)
        xn_ref[pl.ds(r, NORM_ROWS), :] = xn.astype(xn_ref.dtype)
        return carry

    lax.fori_loop(0, h_ref.shape[0] // NORM_ROWS, body, 0, unroll=8)


def _ada_kernel(c_ref, w_ref, b_ref, o_ref):
    c = c_ref[...]
    a = (c * _sigmoid(c)).astype(BF16)
    o_ref[...] = _dot(a, w_ref[...].astype(BF16)) + b_ref[...]


def _ada(c, w, b, tn=1024):
    bsz, d = c.shape
    n = w.shape[1]
    rows = SUBLANES
    assert bsz <= rows
    cp = jnp.zeros((rows, d), F32).at[:bsz].set(c)
    out = pl.pallas_call(
        _ada_kernel,
        grid=(n // tn,),
        in_specs=[pl.BlockSpec((rows, d), lambda j: (0, 0)),
                  pl.BlockSpec((d, tn), lambda j: (0, j)),
                  pl.BlockSpec((1, tn), lambda j: (0, j))],
        out_specs=pl.BlockSpec((rows, tn), lambda j: (0, j)),
        out_shape=jax.ShapeDtypeStruct((rows, n), F32),
        compiler_params=_params("parallel"),
        name="adaln",
    )(cp, w, b.reshape(1, n))
    return out[:bsz]


def _ffn_kernel(h_ref, nw_ref, sh_ref, sc_ref, g_ref, wg_ref, wu_ref, wo_ref, o_ref,
                xn_ref, acc_ref):
    f = pl.program_id(1)

    @pl.when(f == 0)
    def _():
        _norm_mod_to(h_ref, nw_ref, sh_ref, sc_ref, xn_ref)
        acc_ref[...] = jnp.zeros_like(acc_ref)

    x = xn_ref[...]
    g = _dot(x, wg_ref[...])
    u = _dot(x, wu_ref[...])
    a = (g * _sigmoid(g)) * u
    acc_ref[...] += _dot(a.astype(BF16), wo_ref[...])

    @pl.when(f == pl.num_programs(1) - 1)
    def _():
        o_ref[...] = h_ref[...] + (0.5 * g_ref[...]) * acc_ref[...]


def _ffn(h, nw, mods, jsh, jsc, jg, w_in, w_out, seq, casts=(), tm=512, tf=512):
    t, d = h.shape
    dff = w_out.shape[0]
    nf = dff // tf
    per_b = seq // tm

    def mod_spec(j):
        return pl.BlockSpec((None, 1, d), lambda m, f: ((m // per_b) * N_MOD + j, 0, 0))

    cast_specs, cast_shapes = _cast_specs(casts, (t // tm, nf), major_steps=1)
    return pl.pallas_call(
        _with_casts(_ffn_kernel, 8, 1, len(casts)),
        grid=(t // tm, nf),
        in_specs=[pl.BlockSpec((tm, d), lambda m, f: (m, 0)),
                  pl.BlockSpec((1, d), lambda m, f: (0, 0)),
                  mod_spec(jsh), mod_spec(jsc), mod_spec(jg),
                  pl.BlockSpec((d, tf), lambda m, f: (0, f)),
                  pl.BlockSpec((d, tf), lambda m, f: (0, f + nf)),
                  pl.BlockSpec((tf, d), lambda m, f: (f, 0))] + cast_specs,
        out_specs=[pl.BlockSpec((tm, d), lambda m, f: (m, 0))] + cast_specs,
        out_shape=[jax.ShapeDtypeStruct((t, d), F32)] + cast_shapes,
        scratch_shapes=[pltpu.VMEM((tm, d), BF16), pltpu.VMEM((tm, d), F32)],
        compiler_params=_params("parallel", "arbitrary"),
        name="ffn",
    )(h, nw.reshape(1, d), mods, mods, mods, w_in, w_in, w_out, *casts)


def _proj_kernel(h_ref, nw_ref, sh_ref, sc_ref, w_ref, o_ref, xn_ref):
    @pl.when(pl.program_id(1) == 0)
    def _():
        _norm_mod_to(h_ref, nw_ref, sh_ref, sc_ref, xn_ref)

    o_ref[...] = _dot(xn_ref[...], w_ref[...]).astype(o_ref.dtype)


def _proj(h, nw, mods, jsh, jsc, w, seq, tm=1024, tn=1024):
    t, d = h.shape
    n = w.shape[1]
    per_b = seq // tm

    def mod_spec(j):
        return pl.BlockSpec((None, 1, d), lambda m, k: ((m // per_b) * N_MOD + j, 0, 0))

    return pl.pallas_call(
        _proj_kernel,
        grid=(t // tm, n // tn),
        in_specs=[pl.BlockSpec((tm, d), lambda m, k: (m, 0)),
                  pl.BlockSpec((1, d), lambda m, k: (0, 0)),
                  mod_spec(jsh), mod_spec(jsc),
                  pl.BlockSpec((d, tn), lambda m, k: (0, k))],
        out_specs=pl.BlockSpec((tm, tn), lambda m, k: (m, k)),
        out_shape=jax.ShapeDtypeStruct((t, n), BF16),
        scratch_shapes=[pltpu.VMEM((tm, d), BF16)],
        compiler_params=_params("parallel", "arbitrary"),
        name="mix_proj",
    )(h, nw.reshape(1, d), mods, mods, w)


def _qkv_kernel(q_ref, k_ref, v_ref, pos_ref, frq_ref, gm_ref, qw_ref, kw_ref, sel_ref, one_ref,
                swap_ref, qo_ref, ko_ref, vo_ref, *, n_heads, dh, q_scale):
    ang = frq_ref[...] * pos_ref[...].astype(F32)
    cs = jnp.concatenate([jnp.cos(ang), jnp.sin(ang)], axis=0)
    hi = cs.astype(BF16)
    lo = (cs - hi.astype(F32)).astype(BF16)
    sel = sel_ref[...]
    coef = (lax.dot_general(hi, sel, _TN, preferred_element_type=F32)
            + lax.dot_general(lo, sel, _TN, preferred_element_type=F32))
    c_full = coef[:, 0:LANES] + one_ref[...]
    s_full = coef[:, LANES:2 * LANES]
    gm = gm_ref[...]
    swap = swap_ref[...]

    def prep(x, w):
        ss = _dot((x * x).astype(BF16), gm)
        y = x * lax.rsqrt(ss * (1.0 / dh) + EPS) * w
        partner = _dot(y.astype(BF16), swap)
        return (y * c_full + partner * s_full).astype(BF16)

    qw = qw_ref[...] * q_scale
    kw = kw_ref[...]
    for h in range(n_heads):
        sl = slice(h * LANES, (h + 1) * LANES)
        qo_ref[:, sl] = prep(q_ref[:, sl].astype(F32), qw)
        ko_ref[:, sl] = prep(k_ref[:, sl].astype(F32), kw)
        vo_ref[sl, :] = v_ref[:, sl].T


def _qkv(proj, positions, frq, gm, qw, kw, sel, one, swap, width, col0, dh, casts=(), tm=512):
    t = proj.shape[0]
    n_heads = width // LANES
    cb = col0 // width
    kern = functools.partial(_qkv_kernel, n_heads=n_heads, dh=dh,
                             q_scale=math.log2(math.e) / math.sqrt(dh))
    const = lambda shape: pl.BlockSpec(shape, lambda m: (0, 0))
    out = jax.ShapeDtypeStruct((t, width), BF16)
    cast_specs, cast_shapes = _cast_specs(casts, (t // tm,))
    return pl.pallas_call(
        _with_casts(kern, 11, 3, len(casts)),
        grid=(t // tm,),
        in_specs=[pl.BlockSpec((tm, width), lambda m: (m, cb)),
                  pl.BlockSpec((tm, width), lambda m: (m, cb + 1)),
                  pl.BlockSpec((tm, width), lambda m: (m, cb + 2)),
                  pl.BlockSpec((1, tm), lambda m: (0, m)),
                  const(frq.shape), const((LANES, LANES)), const((1, LANES)), const((1, LANES)),
                  const(sel.shape), const((1, LANES)), const((LANES, LANES))] + cast_specs,
        out_specs=[pl.BlockSpec((tm, width), lambda m: (m, 0)),
                   pl.BlockSpec((tm, width), lambda m: (m, 0)),
                   pl.BlockSpec((width, tm), lambda m: (0, m))] + cast_specs,
        out_shape=[out, out, jax.ShapeDtypeStruct((width, t), BF16)] + cast_shapes,
        compiler_params=_params("parallel"),
        name="qkv_prep",
    )(proj, proj, proj, positions, frq, gm, qw, kw, sel, one, swap, *casts)


def _boundary_rows(b_ref, level, c):
    h = 1 << level
    assert h >= 2
    if h == 2:
        sub = lax.broadcasted_iota(jnp.int32, (SUBLANES, LANES), 0)
        pieces = []
        for j in range(c // SUBLANES):
            r = SUBLANES * j
            lo = jnp.broadcast_to(b_ref[r + 1:r + 2, :], (SUBLANES, LANES))
            hi = jnp.broadcast_to(b_ref[r + 5:r + 6, :], (SUBLANES, LANES))
            pieces.append(jnp.where(sub < 4, lo, hi))
        return jnp.concatenate(pieces, axis=0)
    blk = max(2 * h, SUBLANES)
    pieces = []
    for j in range(c // blk):
        r = (j * blk // (2 * h)) * 2 * h + h - 1
        pieces.append(jnp.broadcast_to(b_ref[r:r + 1, :], (blk, LANES)))
    return jnp.concatenate(pieces, axis=0)


def _hgrn_kernel(q_ref, f_ref, i_ref, g_ref, lb_ref, nw_ref, tri_ref, o_ref, st_ref, b_ref, *,
                 chunk, layer, heads):
    c = chunk
    rows = q_ref.shape[0]
    n_levels = c.bit_length() - 1
    hs = range(heads)

    @pl.when(pl.program_id(2) == 0)
    def _():
        st_ref[...] = jnp.zeros_like(st_ref)

    lbx = lb_ref[...]
    mx = jnp.max(lbx, axis=0, keepdims=True)
    ex = jnp.exp(lbx - mx)
    lb_all = (jnp.sum(ex[0:layer + 1, :], axis=0, keepdims=True)
              / jnp.sum(ex, axis=0, keepdims=True))
    nw = nw_ref[...]

    ti = lax.broadcasted_iota(jnp.int32, (c, c), 0)
    si = lax.broadcasted_iota(jnp.int32, (c, c), 1)
    lvl = jnp.where(ti > si, 31 - lax.clz(ti ^ si), jnp.where(ti == si, -1, -2))
    row = lax.broadcasted_iota(jnp.int32, (c, LANES), 0)
    sgn = [jnp.where(((row >> level) & 1) == 1, 1.0, -1.0) for level in range(n_levels)]

    def body(ci, carry):
        r0 = pl.multiple_of(ci * c, c)
        tri = tri_ref[...]
        sl = [slice(h * LANES, (h + 1) * LANES) for h in hs]
        qb = [q_ref[pl.ds(r0, c), sl[h]] for h in hs]
        vb = [i_ref[pl.ds(r0, c), sl[h]] for h in hs]
        q = [qb[h].astype(F32) for h in hs]
        f, lf, k = [], [], []
        for h in hs:
            lb = lb_all[:, sl[h]]
            f.append(lb + (1.0 - lb) * _sigmoid(f_ref[pl.ds(r0, c), sl[h]].astype(F32)))
            lf.append(jnp.log2(f[h]))
            k.append(1.0 - f[h])
        parts = []
        for h in hs:
            p1 = lf[h].astype(BF16)
            parts += [p1, (lf[h] - p1.astype(F32)).astype(BF16)]
        cs = _dot(tri, jnp.concatenate(parts, axis=1))
        b = []
        for h in hs:
            o2 = 2 * h * LANES
            b.append(cs[:, o2:o2 + LANES] + cs[:, o2 + LANES:o2 + 2 * LANES])
            b_ref[h] = b[h]
        kb = [k[h].astype(BF16) for h in hs]
        inter = []
        for h in hs:
            b_last = b[h][c - 1:c, :]
            st = st_ref[h]
            inter.append(lax.dot_general((q[h] * jnp.exp2(b[h])).astype(BF16), st.astype(BF16),
                                         _NT, preferred_element_type=F32))
            kh = (k[h] * jnp.exp2(b_last - b[h])).astype(BF16)
            st_ref[h] = st * jnp.exp2(b_last) + lax.dot_general(vb[h], kh, _TN,
                                                                preferred_element_type=F32)
        scores = [jnp.where(lvl == -1, jnp.sum(q[h] * k[h], axis=-1, keepdims=True), 0.0)
                  for h in hs]
        for h in hs:
            s_0 = lax.dot_general((q[h] * f[h]).astype(BF16), kb[h], _NT,
                                  preferred_element_type=F32)
            scores[h] = jnp.where(lvl == 0, s_0, scores[h])
        for h in hs:
            for level in range(1, n_levels):
                m = _boundary_rows(b_ref.at[h], level, c)
                e = jnp.exp2((b[h] - m) * sgn[level]).astype(BF16)
                s_l = lax.dot_general(qb[h] * e, kb[h] * e, _NT, preferred_element_type=F32)
                half = 1 << level
                if half < SUBLANES:
                    scores[h] = jnp.where(lvl == level, s_l, scores[h])
                    continue
                pieces = []
                for blk in range(0, c, 2 * half):
                    right = slice(blk + half, blk + 2 * half)
                    pieces.append(scores[h][blk:blk + half])
                    pieces.append(jnp.where(lvl[right] == level, s_l[right], scores[h][right]))
                scores[h] = jnp.concatenate(pieces, axis=0)
        for h in hs:
            o = inter[h] + _dot(scores[h].astype(BF16), vb[h])
            var = jnp.mean(o * o, axis=-1, keepdims=True)
            on = o * lax.rsqrt(var + EPS) * nw
            g = g_ref[pl.ds(r0, c), sl[h]].astype(F32)
            o_ref[pl.ds(r0, c), sl[h]] = (on * (g * _sigmoid(g))).astype(o_ref.dtype)
        return carry

    lax.fori_loop(0, rows // c, body, 0)


def _hgrn(proj, lb, nw, layer, bsz, seq, n_heads, col0, casts=(), rows=512, chunk=128, heads=8):
    t = proj.shape[0]
    kdim = LANES
    width = heads * kdim
    ns = seq // rows
    ng = n_heads // heads
    cb = col0 // width

    def col_spec(j):
        return pl.BlockSpec((rows, width), lambda b, h, s: (b * ns + s, cb + j * ng + h))

    cast_specs, cast_shapes = _cast_specs(casts, (bsz, ng, ns))
    kern = functools.partial(_hgrn_kernel, chunk=chunk, layer=layer, heads=heads)
    return pl.pallas_call(
        _with_casts(kern, 7, 1, len(casts)),
        grid=(bsz, ng, ns),
        in_specs=[col_spec(0), col_spec(1), col_spec(2), col_spec(3),
                  pl.BlockSpec((lb.shape[0], width), lambda b, h, s: (0, h)),
                  pl.BlockSpec((1, kdim), lambda b, h, s: (0, 0)),
                  pl.BlockSpec((chunk, chunk), lambda b, h, s: (0, 0))] + cast_specs,
        out_specs=[pl.BlockSpec((rows, width), lambda b, h, s: (b * ns + s, h))] + cast_specs,
        out_shape=[jax.ShapeDtypeStruct((t, n_heads * kdim), BF16)] + cast_shapes,
        scratch_shapes=[pltpu.VMEM((heads, kdim, kdim), F32),
                        pltpu.VMEM((heads, chunk, kdim), F32)],
        compiler_params=_params("parallel", "parallel", "arbitrary"),
        name="hgrn2",
    )(proj, proj, proj, proj, lb, nw.reshape(1, kdim), jnp.tril(jnp.ones((chunk, chunk), BF16)),
      *casts)


def _attn_kernel(q_ref, qn_ref, k_ref, vt_ref, lam_ref, sw_ref, bias_ref, o_ref, qs_ref, m_ref,
                 l_ref, acc_ref, s_ref, mx_ref, *, tq, tk, dh, lam_init, heads):
    qi = pl.program_id(2)
    groups = 2 * heads
    hl = [slice(hh * LANES, (hh + 1) * LANES) for hh in range(heads)]

    def stack_queries(src_ref):
        for hh in range(heads):
            q = src_ref[:, hl[hh]]
            lane = lax.broadcasted_iota(jnp.int32, q.shape, 1)
            zero = jnp.zeros_like(q)
            qs_ref[2 * hh * tq:(2 * hh + 1) * tq, :] = jnp.where(lane < dh, q, zero)
            qs_ref[(2 * hh + 1) * tq:(2 * hh + 2) * tq, :] = jnp.where(lane < dh, zero, q)

    m_ref[...] = jnp.full_like(m_ref, -jnp.inf)
    l_ref[...] = jnp.zeros_like(l_ref)
    acc_ref[...] = jnp.zeros_like(acc_ref)

    def scores(j, slot):
        k0 = pl.multiple_of(j * tk, tk)
        kb = [k_ref[pl.ds(k0, tk), hl[hh]] for hh in range(heads)]
        for g in range(groups):
            cols = slice(g * tq, (g + 1) * tq)
            st = lax.dot_general(kb[g // 2], qs_ref[cols, :], _NT, preferred_element_type=F32)
            s_ref[slot, :, cols] = st
            mx_ref[slot, :, cols] = jnp.max(st, axis=0, keepdims=True)

    def update(cols, st, m_cur, vt):
        m_old = m_ref[:, cols]
        m_new = jnp.maximum(m_old, m_cur)
        alpha = jnp.exp2(m_old - m_new)
        pt = jnp.exp2(st - m_new)
        l_ref[:, cols] = alpha * l_ref[:, cols] + jnp.sum(pt, axis=0, keepdims=True)
        acc_ref[:, cols] = alpha * acc_ref[:, cols] + _dot(vt, pt.astype(BF16))
        m_ref[:, cols] = m_new

    def step(j, slot, masked):
        k0 = pl.multiple_of(j * tk, tk)
        vts = [vt_ref[hl[hh], pl.ds(k0, tk)] for hh in range(heads)]
        for g in range(groups):
            vt = vts[g // 2]
            if not masked:
                cols = slice(g * tq, (g + 1) * tq)
                update(cols, s_ref[slot, :, cols], mx_ref[slot, :, cols], vt)
                continue
            hq, hk = tq // 2, tk // 2
            lo = slice(g * tq, g * tq + hq)
            st = s_ref[slot, 0:hk, lo] + bias_ref[0:hk, 0:hq]
            update(lo, st, jnp.max(st, axis=0, keepdims=True), vt[:, 0:hk])
            hi = slice(g * tq + hq, (g + 1) * tq)
            st = s_ref[slot, :, hi] + bias_ref[:, hq:tq]
            update(hi, st, jnp.max(st, axis=0, keepdims=True), vt)

    n_full = (qi * tq) // tk
    odd = n_full % 2

    @pl.when(qi == 0)
    def _():
        stack_queries(q_ref)
        scores(0, 0)

    @pl.when(odd == 1)
    def _():
        scores(1, 0)
        step(0, 1, False)

    def pair(i, carry):
        j = odd + 2 * i
        scores(j + 1, 1)
        step(j, 0, False)
        scores(j + 2, 0)
        step(j + 1, 1, False)
        return carry

    lax.fori_loop(0, n_full // 2, pair, 0)
    step(n_full, 0, True)

    stack_queries(qn_ref)
    scores(0, (n_full + 1) % 2)

    lv = lam_ref[...]
    lam = (jnp.exp(jnp.sum(lv[0:1, :] * lv[1:2, :], axis=-1, keepdims=True))
           - jnp.exp(jnp.sum(lv[2:3, :] * lv[3:4, :], axis=-1, keepdims=True)) + lam_init)
    ot = acc_ref[...] / l_ref[...]
    for hh in range(heads):
        c0 = 2 * hh * tq
        o = (ot[:, c0:c0 + tq] - lam * ot[:, c0 + tq:c0 + 2 * tq]).T
        var = jnp.mean(o * o, axis=-1, keepdims=True)
        o_ref[:, hl[hh]] = (o * lax.rsqrt(var + EPS) * sw_ref[...]
                            * (1.0 - lam_init)).astype(o_ref.dtype)


def _attn(q, k, vt, lamv, sw, bsz, seq, n_heads, dh, lam_init, tq=512, tk=512, heads=2):
    t = q.shape[0]
    nq = seq // tq
    assert tq == tk
    kern = functools.partial(_attn_kernel, tq=tq, tk=tk, dh=dh, lam_init=lam_init, heads=heads)
    key = jnp.arange(tk)[:, None]
    qry = jnp.arange(tq)[None, :]
    bias = jnp.where(key <= qry, 0.0, -jnp.inf).astype(F32)
    width = heads * LANES
    cols = 2 * heads * tq
    return pl.pallas_call(
        kern,
        grid=(bsz, n_heads // heads, nq),
        in_specs=[pl.BlockSpec((tq, width), lambda b, h, i: (b * nq + i, h)),
                  pl.BlockSpec((tq, width), lambda b, h, i: (b * nq + jnp.minimum(i + 1, nq - 1), h)),
                  pl.BlockSpec((seq, width), lambda b, h, i: (b, h)),
                  pl.BlockSpec((width, seq), lambda b, h, i: (h, b)),
                  pl.BlockSpec(lamv.shape, lambda b, h, i: (0, 0)),
                  pl.BlockSpec((1, LANES), lambda b, h, i: (0, 0)),
                  pl.BlockSpec((tk, tq), lambda b, h, i: (0, 0))],
        out_specs=pl.BlockSpec((tq, width), lambda b, h, i: (b * nq + i, h)),
        out_shape=jax.ShapeDtypeStruct((t, n_heads * LANES), BF16),
        scratch_shapes=[pltpu.VMEM((cols, LANES), BF16),
                        pltpu.VMEM((1, cols), F32),
                        pltpu.VMEM((1, cols), F32),
                        pltpu.VMEM((LANES, cols), F32),
                        pltpu.VMEM((2, tk, cols), F32),
                        pltpu.VMEM((2, 1, cols), F32)],
        compiler_params=_params("parallel", "parallel", "arbitrary"),
        name="diff_attn",
    )(q, q, k, vt, lamv, sw.reshape(1, LANES), bias)


def _merge_kernel(oa_ref, ob_ref, wa_ref, wb_ref, wo_ref, *rest, n_slabs):
    ga_refs = rest[:n_slabs]
    gb_refs = rest[n_slabs:2 * n_slabs]
    h_ref, g_ref, o_ref, mix_ref = rest[2 * n_slabs:]
    oa = oa_ref[...]
    ob = ob_ref[...]
    tn = ga_refs[0].shape[1]
    for j in range(n_slabs):
        cols = slice(j * tn, (j + 1) * tn)
        ya = _dot(oa, wa_ref[:, cols])
        yb = _dot(ob, wb_ref[:, cols])
        ga = ga_refs[j][...].astype(F32)
        gb = gb_refs[j][...].astype(F32)
        mix_ref[:, cols] = (_sigmoid(ga) * ya + _sigmoid(gb) * yb).astype(mix_ref.dtype)
    o_ref[...] = h_ref[...] + g_ref[...] * _dot(mix_ref[...], wo_ref[...])


def _merge(oa, ob, wa, wb, wo, proj, col_ga, col_gb, h, mods, jg, seq, tm=512, tn=1024):
    t, kdim = oa.shape
    d = wa.shape[1]
    n_slabs = d // tn
    per_b = seq // tm
    resident = lambda shape: pl.BlockSpec(shape, lambda m: (0, 0), pipeline_mode=pl.Buffered(1))
    gate = lambda col, j: pl.BlockSpec((tm, tn), lambda m: (m, col // tn + j))
    return pl.pallas_call(
        functools.partial(_merge_kernel, n_slabs=n_slabs),
        grid=(t // tm,),
        in_specs=([pl.BlockSpec((tm, kdim), lambda m: (m, 0)),
                   pl.BlockSpec((tm, kdim), lambda m: (m, 0)),
                   resident((kdim, d)), resident((kdim, d)), resident((d, d))]
                  + [gate(col_ga, j) for j in range(n_slabs)]
                  + [gate(col_gb, j) for j in range(n_slabs)]
                  + [pl.BlockSpec((tm, d), lambda m: (m, 0)),
                     pl.BlockSpec((None, 1, d), lambda m: ((m // per_b) * N_MOD + jg, 0, 0))]),
        out_specs=pl.BlockSpec((tm, d), lambda m: (m, 0)),
        out_shape=jax.ShapeDtypeStruct((t, d), F32),
        scratch_shapes=[pltpu.VMEM((tm, d), BF16)],
        compiler_params=_params("parallel"),
        name="merge_out_proj",
    )(oa, ob, wa, wb, wo, *([proj] * (2 * n_slabs)), h, mods)


def kernel(x, c, positions, ada_w, ada_b, norm1_w, ffn1_w_in, ffn1_w_out, norm2_w, mix_w_in, hgrn_lb, hgrn_norm_w, hgrn_w_out, diff_q_norm_w, diff_k_norm_w, diff_lq1, diff_lk1, diff_lq2, diff_lk2, diff_subln_w, diff_w_out, mix_w_o, norm3_w, ffn2_w_in, ffn2_w_out):
    bsz, seq, d = x.shape
    depth = ada_w.shape[0]
    t = bsz * seq
    kdim = hgrn_norm_w.shape[-1]
    fw = hgrn_lb.shape[-1]
    n_hgrn = fw // kdim
    dh = diff_q_norm_w.shape[-1]
    vdim = diff_subln_w.shape[-1]
    qk_w = diff_w_out.shape[1]
    n_diff = qk_w // vdim
    rope_dim = dh // 4
    half = rope_dim // 2
    assert kdim == LANES and vdim == LANES and 2 * dh == LANES

    frq = (ROPE_THETA ** (-jnp.arange(half, dtype=F32) / half)).reshape(half, 1)
    lane = jnp.arange(LANES)
    dpos = lane % dh
    rot_l = (dpos < rope_dim)[None, :]
    hit = (jnp.arange(half)[:, None] == (dpos % half)[None, :]) & rot_l
    zero = jnp.zeros((half, LANES), F32)
    sel = jnp.concatenate([jnp.concatenate([hit.astype(F32), zero], axis=1),
                           jnp.concatenate([zero, hit.astype(F32)], axis=1)], axis=0).astype(BF16)
    one = jnp.where(dpos < rope_dim, 0.0, 1.0).astype(F32).reshape(1, LANES)
    src, dst = lane[:, None], lane[None, :]
    swap = (jnp.where((dpos < half)[None, :] & (src == dst + half), -1.0, 0.0)
            + jnp.where(((dpos >= half) & (dpos < rope_dim))[None, :] & (src == dst - half),
                        1.0, 0.0)).astype(BF16)
    gm = (lane[:, None] // dh == lane[None, :] // dh).astype(BF16)
    pos2 = positions.reshape(1, t)

    h = x.reshape(t, d)
    for l in range(depth):
        lam_init = 0.8 - 0.6 * math.exp(-0.3 * l)
        assert ada_w.shape[-1] == N_MOD * d
        mods = _ada(c, ada_w[l], ada_b[l]).reshape(bsz * N_MOD, 1, d)
        h, w_mix = _ffn(h, norm1_w[l], mods, 0, 1, 2, ffn1_w_in[l].astype(BF16),
                        ffn1_w_out[l].astype(BF16), seq, casts=(mix_w_in[l],))
        proj = _proj(h, norm2_w[l], mods, 3, 4, w_mix, seq)
        o_a, w2_in, w2_out = _hgrn(proj, hgrn_lb, hgrn_norm_w[l], l, bsz, seq, n_hgrn, 0,
                                   casts=(ffn2_w_in[l], ffn2_w_out[l]))
        qw = jnp.tile(diff_q_norm_w[l], LANES // dh).reshape(1, LANES)
        kw = jnp.tile(diff_k_norm_w[l], LANES // dh).reshape(1, LANES)
        qr, kr, vr, w_a, w_b, w_o = _qkv(proj, pos2, frq, gm, qw, kw, sel, one, swap, qk_w,
                                         4 * fw, dh,
                                         casts=(hgrn_w_out[l], diff_w_out[l], mix_w_o[l]))
        lamv = jnp.stack([diff_lq1[l], diff_lk1[l], diff_lq2[l], diff_lk2[l]])
        o_b = _attn(qr, kr, vr, lamv, diff_subln_w[l], bsz, seq, n_diff, dh, lam_init)
        h = _merge(o_a, o_b, w_a, w_b, w_o, proj, 4 * fw + 3 * qk_w, 4 * fw + 3 * qk_w + d,
                   h, mods, 5, seq)
        h = _ffn(h, norm3_w[l], mods, 6, 7, 8, w2_in, w2_out, seq)[0]
    return h.reshape(bsz, seq, d)
```
